```python
import functools
import jax, jax.numpy as jnp
from jax import lax
import numpy as np

D_MODEL = 1024
BATCH = 8
SEQ = 2048
DEPTH = 2
DEC_BATCH = 32
DEC_SEQ = 32
PAST_LEN = 4096

CHUNK = 64
Q_BLOCK = 2 * CHUNK
HEAD_DIM = 64
D_R = D_MODEL // 2
D_F = D_MODEL - D_R
D_MIX = D_R + D_F
RWKV_HEADS = D_R // HEAD_DIM
FOX_HEADS = D_F // HEAD_DIM
DECAY_LORA = 64
A_LORA = 64
G_LORA = 128
RWKV_COLS = 3 * D_R + DECAY_LORA + A_LORA + G_LORA
FOX_COLS = 3 * D_F + FOX_HEADS + D_F
IN_COLS = RWKV_COLS + FOX_COLS
SHIFT = 2
N_EXPERTS = 32
N_GROUPS = 4
EXPERTS_PER_GROUP = N_EXPERTS // N_GROUPS
TOP_K = 2
D_EXPERT = D_MODEL // 2
MOE_BLOCK = 128
ALPHA = (2 * DEPTH) ** 0.25
BETA = (8 * DEPTH) ** -0.25
FORGET_BIAS = 3.0
LN_EPS = 1e-5
GN_EPS = 64e-5
QK_EPS = 1e-6
SCALE = HEAD_DIM ** -0.5

kernel_name = 'hybrid_rwkv7_fox_grouped_moe_stream_step'


def layer_norm(x, g, b):
    xf = x.astype(jnp.float32)
    mu = xf.mean(-1, keepdims=True)
    var = jnp.square(xf - mu).mean(-1, keepdims=True)
    return ((xf - mu) * lax.rsqrt(var + LN_EPS) * g.astype(jnp.float32) + b.astype(jnp.float32)).astype(x.dtype)


def rms_norm(x, g):
    xf = x.astype(jnp.float32)
    return (xf * lax.rsqrt(jnp.mean(xf * xf, -1, keepdims=True) + QK_EPS) * g.astype(jnp.float32)).astype(x.dtype)


def rwkv_scan(S0, r, w, k, v, kk, a):
    def step(S, inp):
        r_t, w_t, k_t, v_t, kk_t, a_t = inp
        sa = jnp.einsum('bhvk,bhk->bhv', S, -kk_t)
        S = (S * w_t[:, :, None, :] + sa[..., None] * (kk_t * a_t)[:, :, None, :]
             + v_t[..., None] * k_t[:, :, None, :])
        return S, jnp.einsum('bhvk,bhk->bhv', S, r_t)
    xs = tuple(jnp.swapaxes(t, 0, 1) for t in (r, w, k, v, kk, a))
    S, y = lax.scan(step, S0.astype(jnp.float32), xs)
    return S, jnp.swapaxes(y, 0, 1)


def rwkv_mixer(pr, shift_prev, S0, mu, w0, w2, a0, a2, g2, k_k, k_a, r_k, lnx_w, lnx_b):
    B, T, _ = pr.shape
    f32 = jnp.float32
    prev = jnp.concatenate([shift_prev.astype(pr.dtype), pr[:, :-1]], axis=1)
    xs = pr + (prev - pr) * mu
    xr, xk, xv, xw, xa, xg = jnp.split(
        xs, [D_R, 2 * D_R, 3 * D_R, 3 * D_R + DECAY_LORA, 3 * D_R + DECAY_LORA + A_LORA], axis=-1)
    heads = lambda t: t.astype(f32).reshape(B, T, RWKV_HEADS, HEAD_DIM)
    w = -jax.nn.softplus(-(w0 + jnp.tanh(xw) @ w2).astype(f32)) - 0.5
    decay = jnp.exp(-jnp.exp(w))
    a = jax.nn.sigmoid((a0 + xa @ a2).astype(f32))
    g = jax.nn.sigmoid(xg) @ g2
    kk = heads(xk * k_k)
    kk = kk / jnp.maximum(jnp.linalg.norm(kk, axis=-1, keepdims=True), 1e-12)
    k = xk.astype(f32) * (1.0 + (a - 1.0) * k_a.astype(f32))
    r, k, v, decay, a = heads(xr), heads(k), heads(xv), heads(decay), heads(a)
    S, y = rwkv_scan(S0, r, decay, k, v, kk, a)
    mean = y.mean(-1, keepdims=True)
    var = jnp.square(y - mean).mean(-1, keepdims=True)
    y = ((y - mean) * lax.rsqrt(var + GN_EPS)).reshape(B, T, D_R) * lnx_w.astype(f32) + lnx_b.astype(f32)
    bonus = jnp.sum(r * k * r_k.astype(f32), -1, keepdims=True) * v
    y = (y + bonus.reshape(B, T, D_R)) * g.astype(f32)
    return y.astype(pr.dtype), S, pr[:, -1:]


def fox_project(pf, b_f, q_g, k_g):
    B, T, _ = pf.shape
    q, k, v, fl, og = jnp.split(pf, [D_F, 2 * D_F, 3 * D_F, 3 * D_F + FOX_HEADS], axis=-1)
    hd = lambda t: t.reshape(B, T, FOX_HEADS, HEAD_DIM)
    logf = jax.nn.log_sigmoid((fl + b_f).astype(jnp.float32))
    return rms_norm(hd(q), q_g), rms_norm(hd(k), k_g), hd(v), logf, og


def fox_prompt_attention(q, k, v, logf):
    B, T, H, Dh = q.shape
    nb = T // Q_BLOCK
    c = jnp.swapaxes(jnp.cumsum(logf, axis=1), 1, 2)
    qb = jnp.moveaxis(q.reshape(B, nb, Q_BLOCK, H, Dh), 1, 0)
    cb = jnp.moveaxis(c.reshape(B, H, nb, Q_BLOCK), 2, 0)
    key_pos = jnp.arange(T)

    def block(args):
        q_i, c_i, i = args
        s = (jnp.einsum('bqhd,bkhd->bhqk', q_i, k).astype(jnp.float32) * SCALE
             + c_i[..., :, None] - c[..., None, :])
        q_pos = i * Q_BLOCK + jnp.arange(Q_BLOCK)
        s = jnp.where(key_pos[None, :] <= q_pos[:, None], s, -jnp.inf)
        p = jax.nn.softmax(s, axis=-1).astype(v.dtype)
        return jnp.einsum('bhqk,bkhd->bqhd', p, v)

    o = lax.map(block, (qb, cb, jnp.arange(nb)))
    return jnp.moveaxis(o, 0, 1).reshape(B, T, H, Dh)


def fox_sample_attention(q, k, v, logf, k_past, v_past, logf_past):
    f32 = jnp.float32
    P = k_past.shape[1]
    n = q.shape[1]
    c_new = jnp.swapaxes(jnp.cumsum(logf, axis=1), 1, 2)
    lp = logf_past.astype(f32)
    tail = jnp.swapaxes(lax.cumsum(lp, axis=1, reverse=True) - lp, 1, 2)
    s_past = (jnp.einsum('bqhd,bkhd->bhqk', q, k_past.astype(q.dtype)).astype(f32) * SCALE
              + c_new[..., :, None] + tail[..., None, :])
    s_new = (jnp.einsum('bqhd,bkhd->bhqk', q, k).astype(f32) * SCALE
             + c_new[..., :, None] - c_new[..., None, :])
    s_new = jnp.where(jnp.tril(jnp.ones((n, n), bool)), s_new, -jnp.inf)
    p = jax.nn.softmax(jnp.concatenate([s_past, s_new], axis=-1), axis=-1).astype(v.dtype)
    return (jnp.einsum('bhqk,bkhd->bqhd', p[..., :P], v_past.astype(v.dtype))
            + jnp.einsum('bhqk,bkhd->bqhd', p[..., P:], v))


def grouped_moe(h, router_w, router_b, w1, w3, w2):
    B, T, D = h.shape
    n = B * T
    hf = h.reshape(n, D)
    scores = jax.nn.sigmoid((hf @ router_w).astype(jnp.float32))
    sel = (scores + router_b.astype(jnp.float32)).reshape(n, N_GROUPS, EXPERTS_PER_GROUP)
    g_idx = jnp.argmax(lax.top_k(sel, 2)[0].sum(-1), axis=-1)
    in_group = sel[jnp.arange(n), g_idx]
    eidx = g_idx[:, None] * EXPERTS_PER_GROUP + lax.top_k(in_group, TOP_K)[1]
    gate = jnp.take_along_axis(scores, eidx, axis=1)
    gate = (gate / gate.sum(-1, keepdims=True)).astype(h.dtype)
    A = n * TOP_K
    flat_e = eidx.reshape(A)
    flat_tok = jnp.repeat(jnp.arange(n, dtype=jnp.int32), TOP_K)
    order = jnp.argsort(flat_e)
    se = flat_e[order]
    counts = jnp.bincount(flat_e, length=N_EXPERTS)
    starts = jnp.cumsum(counts) - counts
    padded = (counts + MOE_BLOCK - 1) // MOE_BLOCK * MOE_BLOCK
    pends = jnp.cumsum(padded)
    dest = (pends - padded)[se] + jnp.arange(A) - starts[se]
    nb = -(-A // MOE_BLOCK) + N_EXPERTS
    rows = nb * MOE_BLOCK
    tok_pad = jnp.full((rows,), n, jnp.int32).at[dest].set(flat_tok[order])
    gate_pad = jnp.zeros((rows,), h.dtype).at[dest].set(gate.reshape(A)[order])
    block_expert = jnp.clip(jnp.searchsorted(pends, jnp.arange(nb) * MOE_BLOCK, side='right'), 0, N_EXPERTS - 1)
    h_pad = jnp.concatenate([hf, jnp.zeros((1, D), h.dtype)], axis=0)[tok_pad].reshape(nb, MOE_BLOCK, D)

    def expert_block(args):
        hb, e = args
        return (jax.nn.silu(hb @ w1[e]) * (hb @ w3[e])) @ w2[e]

    y_pad = lax.map(expert_block, (h_pad, block_expert)).reshape(rows, D) * gate_pad[:, None]
    out = jnp.zeros((n + 1, D), h.dtype).at[tok_pad].add(y_pad)[:n]
    return out.reshape(B, T, D)


def trunk_layer(x, shift_prev, rwkv_state, attend, lp, router_w, router_b):
    B, T, _ = x.shape
    p = x @ lp['w_in']
    ry, rwkv_state, shift = rwkv_mixer(p[..., :RWKV_COLS], shift_prev, rwkv_state, lp['mu'], lp['w0'],
                                       lp['w2'], lp['a0'], lp['a2'], lp['g2'], lp['k_k'], lp['k_a'],
                                       lp['r_k'], lp['lnx_w'], lp['lnx_b'])
    q, k, v, logf, og = fox_project(p[..., RWKV_COLS:], lp['b_f'], lp['q_g'], lp['k_g'])
    fy = attend(q, k, v, logf).reshape(B, T, D_F) * jax.nn.sigmoid(og)
    m = jnp.concatenate([ry, fy.astype(ry.dtype)], axis=-1) @ lp['w_out']
    x = layer_norm(ALPHA * x + m, lp['ln1_g'], lp['ln1_b'])
    x = layer_norm(ALPHA * x + grouped_moe(x, router_w, router_b, lp['e_w1'], lp['e_w3'], lp['e_w2']),
                   lp['ln2_g'], lp['ln2_b'])
    return x, k, v, logf, rwkv_state, shift


def setup_inputs(seed: int = 0) -> dict:
    key = jax.random.key(seed)
    ks = iter(jax.random.split(key, 40))
    nrm = lambda shape, s=1.0: s * jax.random.normal(next(ks), shape, jnp.float32)
    col_scale = (jnp.ones((IN_COLS,), jnp.float32).at[2 * D_R:3 * D_R].set(BETA)
                 .at[RWKV_COLS + 2 * D_F:RWKV_COLS + 3 * D_F].set(BETA))
    return {
        'x_prompt': nrm((BATCH, SEQ, D_MODEL)),
        'x_sample': nrm((DEC_BATCH, DEC_SEQ, D_MODEL)),
        'cache_fox_k': nrm((DEPTH, DEC_BATCH, PAST_LEN, FOX_HEADS, HEAD_DIM)),
        'cache_fox_v': nrm((DEPTH, DEC_BATCH, PAST_LEN, FOX_HEADS, HEAD_DIM), BETA),
        'cache_fox_logf': jax.nn.log_sigmoid(FORGET_BIAS + nrm((DEPTH, DEC_BATCH, PAST_LEN, FOX_HEADS))),
        'state_rwkv': nrm((DEPTH, DEC_BATCH, RWKV_HEADS, HEAD_DIM, HEAD_DIM), 0.5),
        'state_rwkv_shift': nrm((DEPTH, DEC_BATCH, SHIFT - 1, RWKV_COLS)),
        'w_in': nrm((DEPTH, D_MODEL, IN_COLS), D_MODEL ** -0.5) * col_scale,
        'rwkv_mu': jax.random.uniform(next(ks), (DEPTH, RWKV_COLS), jnp.float32),
        'rwkv_w0': jax.random.uniform(next(ks), (DEPTH, D_R), jnp.float32, minval=-6.0, maxval=-1.0),
        'rwkv_w2': nrm((DEPTH, DECAY_LORA, D_R), 0.1),
        'rwkv_a0': nrm((DEPTH, D_R), 0.1),
        'rwkv_a2': nrm((DEPTH, A_LORA, D_R), 0.5 * A_LORA ** -0.5),
        'rwkv_g2': nrm((DEPTH, G_LORA, D_R), G_LORA ** -0.5),
        'rwkv_k_k': 0.85 + nrm((DEPTH, D_R), 0.05),
        'rwkv_k_a': 1.0 + nrm((DEPTH, D_R), 0.05),
        'rwkv_r_k': nrm((DEPTH, RWKV_HEADS, HEAD_DIM), 0.1),
        'rwkv_lnx_w': 1.0 + nrm((DEPTH, D_R), 0.05),
        'rwkv_lnx_b': nrm((DEPTH, D_R), 0.02),
        'fox_b_f': FORGET_BIAS + nrm((DEPTH, FOX_HEADS), 0.5),
        'fox_q_g': 1.0 + nrm((DEPTH, HEAD_DIM), 0.05),
        'fox_k_g': 1.0 + nrm((DEPTH, HEAD_DIM), 0.05),
        'w_out': nrm((DEPTH, D_MIX, D_MODEL), BETA * D_MIX ** -0.5),
        'ln1_g': 1.0 + nrm((DEPTH, D_MODEL), 0.05),
        'ln1_b': nrm((DEPTH, D_MODEL), 0.02),
        'ln2_g': 1.0 + nrm((DEPTH, D_MODEL), 0.05),
        'ln2_b': nrm((DEPTH, D_MODEL), 0.02),
        'router_w': nrm((D_MODEL, N_EXPERTS), D_MODEL ** -0.5),
        'router_b': nrm((N_EXPERTS,), 0.01),
        'moe_w1': nrm((DEPTH, N_EXPERTS, D_MODEL, D_EXPERT), BETA * D_MODEL ** -0.5),
        'moe_w3': nrm((DEPTH, N_EXPERTS, D_MODEL, D_EXPERT), BETA * D_MODEL ** -0.5),
        'moe_w2': nrm((DEPTH, N_EXPERTS, D_EXPERT, D_MODEL), BETA * D_EXPERT ** -0.5),
    }


def reference(x_prompt, x_sample, cache_fox_k, cache_fox_v, cache_fox_logf, state_rwkv, state_rwkv_shift,
              w_in, rwkv_mu, rwkv_w0, rwkv_w2, rwkv_a0, rwkv_a2, rwkv_g2, rwkv_k_k, rwkv_k_a, rwkv_r_k,
              rwkv_lnx_w, rwkv_lnx_b, fox_b_f, fox_q_g, fox_k_g, w_out, ln1_g, ln1_b, ln2_g, ln2_b,
              router_w, router_b, moe_w1, moe_w3, moe_w2):
    xp, xs = x_prompt, x_sample
    nb_p = xp.shape[0]
    outs_p, outs_s = [], []
    for l in range(DEPTH):
        lp = dict(w_in=w_in[l], mu=rwkv_mu[l], w0=rwkv_w0[l], w2=rwkv_w2[l], a0=rwkv_a0[l], a2=rwkv_a2[l],
                  g2=rwkv_g2[l], k_k=rwkv_k_k[l], k_a=rwkv_k_a[l], r_k=rwkv_r_k[l], lnx_w=rwkv_lnx_w[l],
                  lnx_b=rwkv_lnx_b[l], b_f=fox_b_f[l], q_g=fox_q_g[l], k_g=fox_k_g[l], w_out=w_out[l],
                  ln1_g=ln1_g[l], ln1_b=ln1_b[l], ln2_g=ln2_g[l], ln2_b=ln2_b[l],
                  e_w1=moe_w1[l], e_w3=moe_w3[l], e_w2=moe_w2[l])
        xp, *st_p = trunk_layer(xp, jnp.zeros((nb_p, SHIFT - 1, RWKV_COLS), xp.dtype),
                                jnp.zeros((nb_p, RWKV_HEADS, HEAD_DIM, HEAD_DIM), jnp.float32),
                                fox_prompt_attention, lp, router_w, router_b)
        attend_s = functools.partial(fox_sample_attention, k_past=cache_fox_k[l], v_past=cache_fox_v[l],
                                     logf_past=cache_fox_logf[l])
        xs, *st_s = trunk_layer(xs, state_rwkv_shift[l], state_rwkv[l], attend_s, lp, router_w, router_b)
        outs_p.append(st_p)
        outs_s.append(st_s)
    stk = lambda outs, i: jnp.stack([o[i] for o in outs], axis=0)
    p_fox_k, p_fox_v, p_fox_logf = stk(outs_p, 0), stk(outs_p, 1), stk(outs_p, 2)
    p_rwkv, p_shift = stk(outs_p, 3), stk(outs_p, 4)
    s_fox_k, s_fox_v, s_fox_logf = stk(outs_s, 0), stk(outs_s, 1), stk(outs_s, 2)
    s_rwkv, s_shift = stk(outs_s, 3), stk(outs_s, 4)
    return (xp, xs, p_fox_k, p_fox_v, p_fox_logf, p_rwkv, p_shift, s_fox_k, s_fox_v, s_fox_logf, s_rwkv, s_shift)
```

```python
import functools

import jax
import jax.numpy as jnp
from jax import lax
from jax.experimental import pallas as pl
from jax.experimental.pallas import tpu as pltpu

F32 = jnp.float32
BF16 = jnp.bfloat16
I32 = jnp.int32

D_MODEL = 1024
HEAD_DIM = 64
N_HEADS = 8
D_R = N_HEADS * HEAD_DIM
D_F = N_HEADS * HEAD_DIM
DECAY_LORA = 64
A_LORA = 64
G_LORA = 128
RWKV_COLS = 3 * D_R + DECAY_LORA + A_LORA + G_LORA
FL_PAD = 128
IN_COLS_PAD = RWKV_COLS + 4 * D_F + FL_PAD
DEPTH = 2
N_EXPERTS = 32
N_GROUPS = 4
EXPERTS_PER_GROUP = N_EXPERTS // N_GROUPS
TOP_K = 2
D_EXPERT = D_MODEL // 2
MOE_BLOCK = 128
ALPHA = (2 * DEPTH) ** 0.25
LN_EPS = 1e-5
GN_EPS = 64e-5
QK_EPS = 1e-6
SCALE = HEAD_DIM ** -0.5
RWKV_CHUNK = 64
INV_BASE = 16
VMEM_LIMIT = 48 * 1024 * 1024

_NN = (((1,), (0,)), ((), ()))
_NT = (((1,), (1,)), ((), ()))
_TN = (((0,), (0,)), ((), ()))


def _dot(a, b, dims=_NN):
    return lax.dot_general(a, b, dims, preferred_element_type=F32)


def _split2(x):
    hi = x.astype(BF16)
    lo = (x - hi.astype(F32)).astype(BF16)
    return hi, lo


def _split3(x):
    hi = x.astype(BF16)
    r = x - hi.astype(F32)
    mid = r.astype(BF16)
    lo = (r - mid.astype(F32)).astype(BF16)
    return hi, mid, lo


def _dot3(a, b, dims=_NN):
    ah, al = _split2(a)
    bh, bl = _split2(b)
    return _dot(ah, bh, dims) + (_dot(ah, bl, dims) + _dot(al, bh, dims))


def _dot_exact_lhs(a_bf16, x, dims=_NN):
    hi, mid, lo = _split3(x)
    return _dot(a_bf16, hi, dims) + (_dot(a_bf16, mid, dims) + _dot(a_bf16, lo, dims))


def _sigmoid(x):
    return 1.0 / (1.0 + jnp.exp(-x))


def _softplus(x):
    return jnp.maximum(x, 0.0) + jnp.log(1.0 + jnp.exp(-jnp.abs(x)))


def _layer_norm(z, g, b):
    mu = jnp.mean(z, axis=-1, keepdims=True)
    zc = z - mu
    var = jnp.mean(zc * zc, axis=-1, keepdims=True)
    return zc * lax.rsqrt(var + LN_EPS) * g + b


def _params(sem):
    return pltpu.CompilerParams(dimension_semantics=sem, vmem_limit_bytes=VMEM_LIMIT)


_IN_SPLITS = (RWKV_COLS, D_F, D_F, D_F, D_F, FL_PAD)


def _in_proj_body(x_ref, w_ref, *out_refs):
    x = x_ref[...].astype(BF16)
    col = 0
    for ref, width in zip(out_refs, _IN_SPLITS):
        for c0 in range(0, width, 512):
            c1 = min(c0 + 512, width)
            ref[:, c0:c1] = _dot(x, w_ref[:, col + c0:col + c1])
        col += width


def _in_proj(x, w):
    n = x.shape[0]
    tm = 256
    return pl.pallas_call(
        _in_proj_body,
        grid=(n // tm,),
        in_specs=[pl.BlockSpec((tm, D_MODEL), lambda i: (i, 0)),
                  pl.BlockSpec((D_MODEL, IN_COLS_PAD), lambda i: (0, 0))],
        out_specs=[pl.BlockSpec((tm, wd), lambda i: (i, 0)) for wd in _IN_SPLITS],
        out_shape=[jax.ShapeDtypeStruct((n, wd), F32) for wd in _IN_SPLITS],
        compiler_params=_params(("parallel",)),
        name="in_proj",
    )(x, w)


def _unit_lower_inverse(L, row, col):
    c = L.shape[0]
    shift = INV_BASE.bit_length() - 1
    same = (row >> shift) == (col >> shift)
    n = jnp.where(same, -L, 0.0)
    x = jnp.where(row == col, 1.0, 0.0) + n
    p = n
    for _ in range(shift - 1):
        p = _dot3(p, p)
        x = x + _dot3(x, p)
    size = 2 * INV_BASE
    while size <= c:
        s_hi = size.bit_length() - 1
        off = ((row >> s_hi) == (col >> s_hi)) & ((row >> (s_hi - 1)) != (col >> (s_hi - 1)))
        q = jnp.where(off, L, 0.0)
        x = x - _dot3(_dot3(x, q), x)
        size *= 2
    return x


def _rwkv_body(c_len, n_chunks, pr_ref, sp_ref, s0_ref, mu_ref, w0_ref, w2_ref, a0_ref, a2_ref, g2_ref,
               kk_ref, ka_ref, rk_ref, lnw_ref, lnb_ref, out_ref, sout_ref, carry_ref, s_ref):
    ci = pl.program_id(1)

    @pl.when(ci == 0)
    def _():
        carry_ref[...] = sp_ref[0]
        s_ref[...] = s0_ref[0]

    pr = pr_ref[...]
    trow = lax.broadcasted_iota(I32, (c_len, 1), 0)
    prev = jnp.where(trow == 0, carry_ref[...], pltpu.roll(pr, 1, 0))
    carry_ref[...] = pr[c_len - 1:c_len, :]
    xs = pr + (prev - pr) * mu_ref[...]
    xr = xs[:, 0:D_R]
    xk = xs[:, D_R:2 * D_R]
    xv = xs[:, 2 * D_R:3 * D_R]
    o = 3 * D_R
    xw = xs[:, o:o + DECAY_LORA]
    xa = xs[:, o + DECAY_LORA:o + DECAY_LORA + A_LORA]
    xg = xs[:, o + DECAY_LORA + A_LORA:RWKV_COLS]

    z = w0_ref[...] + _dot3(jnp.tanh(xw), w2_ref[...])
    lw = -jnp.exp(-_softplus(-z) - 0.5)
    a = _sigmoid(a0_ref[...] + _dot3(xa, a2_ref[...]))
    g = _dot3(_sigmoid(xg), g2_ref[...])
    kk_raw = xk * kk_ref[...]
    k_mod = xk * (1.0 + (a - 1.0) * ka_ref[...])

    row = lax.broadcasted_iota(I32, (c_len, c_len), 0)
    col = lax.broadcasted_iota(I32, (c_len, c_len), 1)
    strict = row > col
    incl = row >= col
    cl = _dot_exact_lhs(jnp.where(incl, 1.0, 0.0).astype(BF16), lw)

    outs = []
    for h in range(N_HEADS):
        sl = slice(h * HEAD_DIM, (h + 1) * HEAD_DIM)
        r_h, k_h, v_h = xr[:, sl], k_mod[:, sl], xv[:, sl]
        kk_h = kk_raw[:, sl]
        nrm = jnp.sqrt(jnp.sum(kk_h * kk_h, axis=-1, keepdims=True))
        kk_h = kk_h / jnp.maximum(nrm, 1e-12)
        b_h = kk_h * a[:, sl]
        cl_h = cl[:, sl]
        eg = jnp.exp(cl_h)
        r_dec = r_h * eg
        kk_dec = kk_h * jnp.exp(cl_h - lw[:, sl])
        e_inv = jnp.exp(-cl_h)
        b_und = b_h * e_inv
        k_und = k_h * e_inv
        lhs = jnp.concatenate([kk_dec, r_dec], axis=0)
        gb = _dot3(lhs, b_und, _NT)
        gk = _dot3(lhs, k_und, _NT)
        l_b = jnp.where(strict, gb[:c_len], 0.0)
        m_rb = jnp.where(incl, gb[c_len:], 0.0)
        l_k = jnp.where(strict, gk[:c_len], 0.0)
        m_rk = jnp.where(incl, gk[c_len:], 0.0)
        t_inv = _unit_lower_inverse(l_b, row, col)
        s = s_ref[h]
        ps = _dot3(lhs, s, _NT)
        u = -_dot3(t_inv, ps[:c_len] + _dot3(l_k, v_h))
        y = ps[c_len:] + _dot3(m_rb, u) + _dot3(m_rk, v_h)
        s_ref[h] = (s + _dot3(u, b_und, _TN) + _dot3(v_h, k_und, _TN)) * eg[c_len - 1:c_len, :]

        mean = jnp.mean(y, axis=-1, keepdims=True)
        yc = y - mean
        var = jnp.mean(yc * yc, axis=-1, keepdims=True)
        yn = yc * lax.rsqrt(var + GN_EPS) * lnw_ref[:, sl] + lnb_ref[:, sl]
        bonus = jnp.sum(r_h * k_h * rk_ref[:, sl], axis=-1, keepdims=True) * v_h
        outs.append((yn + bonus) * g[:, sl])
    out_ref[...] = jnp.concatenate(outs, axis=-1)

    @pl.when(ci == n_chunks - 1)
    def _():
        sout_ref[0] = s_ref[...]


def _rwkv(pr, shift_prev, s0, lp, n_seq, seq_len, row0):
    c_len = min(RWKV_CHUNK, seq_len)
    n_chunks = seq_len // c_len
    blk0 = row0 // c_len
    vec = lambda wd: pl.BlockSpec((1, wd), lambda b, c: (0, 0))
    mat = lambda r, wd: pl.BlockSpec((r, wd), lambda b, c: (0, 0))
    in_specs = [
        pl.BlockSpec((c_len, RWKV_COLS), lambda b, c: (blk0 + b * n_chunks + c, 0)),
        pl.BlockSpec((1, 1, RWKV_COLS), lambda b, c: (b, 0, 0)),
        pl.BlockSpec((1, N_HEADS, HEAD_DIM, HEAD_DIM), lambda b, c: (b, 0, 0, 0)),
        vec(RWKV_COLS), vec(D_R), mat(DECAY_LORA, D_R), vec(D_R), mat(A_LORA, D_R), mat(G_LORA, D_R),
        vec(D_R), vec(D_R), vec(D_R), vec(D_R), vec(D_R),
    ]
    args = [pr, shift_prev, s0, lp['mu'], lp['w0'], lp['w2'], lp['a0'], lp['a2'], lp['g2'],
            lp['k_k'], lp['k_a'], lp['r_k'], lp['lnx_w'], lp['lnx_b']]
    return pl.pallas_call(
        functools.partial(_rwkv_body, c_len, n_chunks),
        grid=(n_seq, n_chunks),
        in_specs=in_specs,
        out_specs=[pl.BlockSpec((c_len, D_R), lambda b, c: (b * n_chunks + c, 0)),
                   pl.BlockSpec((1, N_HEADS, HEAD_DIM, HEAD_DIM), lambda b, c: (b, 0, 0, 0))],
        out_shape=[jax.ShapeDtypeStruct((n_seq * seq_len, D_R), F32),
                   jax.ShapeDtypeStruct((n_seq, N_HEADS, HEAD_DIM, HEAD_DIM), F32)],
        scratch_shapes=[pltpu.VMEM((1, RWKV_COLS), F32), pltpu.VMEM((N_HEADS, HEAD_DIM, HEAD_DIM), F32)],
        compiler_params=_params(("arbitrary", "arbitrary")),
        name="rwkv_mixer",
    )(*args)


def _fox_prep_body(q_ref, k_ref, v_ref, fl_ref, qg_ref, kg_ref, bf_ref, ones_ref,
                   qb_ref, kn_ref, kb_ref, vb_ref, lf_ref):
    ones = ones_ref[...]

    def rms(x, gain):
        hi, lo = _split2(x * x)
        ss = _dot(hi, ones) + _dot(lo, ones)
        return x * lax.rsqrt(ss * (1.0 / HEAD_DIM) + QK_EPS) * gain

    qn = rms(q_ref[...], qg_ref[...])
    kn = rms(k_ref[...], kg_ref[...])
    qb_ref[...] = (qn * SCALE).astype(BF16)
    kn_ref[...] = kn
    kb_ref[...] = kn.astype(BF16)
    vb_ref[...] = v_ref[...].astype(BF16)
    lf_ref[...] = -_softplus(-(fl_ref[...] + bf_ref[...]))


def _fox_prep(q, k, v, fl, q_gain, k_gain, b_f, ones_bd):
    n = q.shape[0]
    tm = 256
    row = lambda wd: pl.BlockSpec((tm, wd), lambda i: (i, 0))
    vec = lambda wd: pl.BlockSpec((1, wd), lambda i: (0, 0))
    return pl.pallas_call(
        _fox_prep_body,
        grid=(n // tm,),
        in_specs=[row(D_F), row(D_F), row(D_F), row(FL_PAD), vec(D_F), vec(D_F), vec(FL_PAD),
                  pl.BlockSpec((D_F, D_F), lambda i: (0, 0))],
        out_specs=[row(D_F), row(D_F), row(D_F), row(D_F), row(FL_PAD)],
        out_shape=[jax.ShapeDtypeStruct((n, D_F), BF16), jax.ShapeDtypeStruct((n, D_F), F32),
                   jax.ShapeDtypeStruct((n, D_F), BF16), jax.ShapeDtypeStruct((n, D_F), BF16),
                   jax.ShapeDtypeStruct((n, FL_PAD), F32)],
        compiler_params=_params(("parallel",)),
        name="fox_prep",
    )(q, k, v, fl, q_gain, k_gain, b_f, ones_bd)


def _cumsum_body(x_ref, o_ref, carry_ref):
    @pl.when(pl.program_id(1) == 0)
    def _():
        carry_ref[...] = jnp.zeros_like(carry_ref)

    x = x_ref[0]
    t = x.shape[0]
    row = lax.broadcasted_iota(I32, (t, t), 0)
    col = lax.broadcasted_iota(I32, (t, t), 1)
    cs = _dot_exact_lhs(jnp.where(row >= col, 1.0, 0.0).astype(BF16), x) + carry_ref[...]
    o_ref[0] = cs
    carry_ref[...] = cs[t - 1:t, :]


def _cumsum_rows(x):
    n_seq, t, w = x.shape
    tc = min(t, 256)
    return pl.pallas_call(
        _cumsum_body,
        grid=(n_seq, t // tc),
        in_specs=[pl.BlockSpec((1, tc, w), lambda b, j: (b, j, 0))],
        out_specs=pl.BlockSpec((1, tc, w), lambda b, j: (b, j, 0)),
        out_shape=jax.ShapeDtypeStruct((n_seq, t, w), F32),
        scratch_shapes=[pltpu.VMEM((1, w), F32)],
        compiler_params=_params(("arbitrary", "arbitrary")),
        name="cumsum_rows",
    )(x)


def _cumsum_lanes_body(x_ref, o_ref, carry_ref):
    @pl.when(pl.program_id(1) == 0)
    def _():
        carry_ref[...] = jnp.zeros_like(carry_ref)

    x = x_ref[0]
    t = x.shape[1]
    row = lax.broadcasted_iota(I32, (t, t), 0)
    col = lax.broadcasted_iota(I32, (t, t), 1)
    upper = jnp.where(row <= col, 1.0, 0.0).astype(BF16)
    hi, mid, lo = _split3(x)
    cs = _dot(hi, upper) + (_dot(mid, upper) + _dot(lo, upper)) + carry_ref[...]
    o_ref[0] = cs
    carry_ref[...] = cs[:, t - 1:t]


def _cumsum_lanes(x):
    n_seq, h, t = x.shape
    tc = min(t, 256)
    return pl.pallas_call(
        _cumsum_lanes_body,
        grid=(n_seq, t // tc),
        in_specs=[pl.BlockSpec((1, h, tc), lambda b, j: (b, 0, j))],
        out_specs=pl.BlockSpec((1, h, tc), lambda b, j: (b, 0, j)),
        out_shape=jax.ShapeDtypeStruct((n_seq, h, t), F32),
        scratch_shapes=[pltpu.VMEM((h, 1), F32)],
        compiler_params=_params(("arbitrary", "arbitrary")),
        name="cumsum_lanes",
    )(x)


ATT_TILE = 128


def _attn_prompt_body(q_ref, k_ref, v_ref, cc_ref, cr_ref, o_ref):
    i = pl.program_id(2)
    t = ATT_TILE
    row = lax.broadcasted_iota(I32, (t, t), 0)
    col = lax.broadcasted_iota(I32, (t, t), 1)
    causal = row >= col
    outs = []
    for hh in range(2):
        sl = slice(hh * HEAD_DIM, (hh + 1) * HEAD_DIM)
        q = q_ref[:, sl]
        c_col = cc_ref[0, :, hh:hh + 1]

        def tile(j, carry, masked):
            m, l, acc = carry
            j0 = pl.multiple_of(j * t, t)
            k = k_ref[pl.ds(j0, t), sl]
            v = v_ref[pl.ds(j0, t), sl]
            s = _dot(q, k, _NT) + c_col - cr_ref[0, 0, hh:hh + 1, pl.ds(j0, t)]
            if masked:
                s = jnp.where(causal, s, -jnp.inf)
            m_new = jnp.maximum(m, jnp.max(s, axis=-1, keepdims=True))
            alpha = jnp.exp(m - m_new)
            p = jnp.exp(s - m_new)
            l = alpha * l + jnp.sum(p, axis=-1, keepdims=True)
            acc = alpha * acc + _dot(p.astype(BF16), v)
            return m_new, l, acc

        init = (jnp.full((t, 1), -jnp.inf, F32), jnp.zeros((t, 1), F32), jnp.zeros((t, HEAD_DIM), F32))
        carry = lax.fori_loop(0, i, lambda j, c: tile(j, c, False), init)
        _, l, acc = tile(i, carry, True)
        outs.append(acc / l)
    o_ref[...] = jnp.concatenate(outs, axis=-1)


def _attn_prompt(qb, kb, vb, c_col, c_row, n_seq, seq_len):
    t = ATT_TILE
    nq = seq_len // t
    return pl.pallas_call(
        _attn_prompt_body,
        grid=(n_seq, N_HEADS // 2, nq),
        in_specs=[pl.BlockSpec((t, 2 * HEAD_DIM), lambda b, p, i: (b * nq + i, p)),
                  pl.BlockSpec((seq_len, 2 * HEAD_DIM), lambda b, p, i: (b, p)),
                  pl.BlockSpec((seq_len, 2 * HEAD_DIM), lambda b, p, i: (b, p)),
                  pl.BlockSpec((1, t, 2), lambda b, p, i: (p, b * nq + i, 0)),
                  pl.BlockSpec((1, 1, 2, seq_len), lambda b, p, i: (b, p, 0, 0))],
        out_specs=pl.BlockSpec((t, 2 * HEAD_DIM), lambda b, p, i: (b * nq + i, p)),
        out_shape=jax.ShapeDtypeStruct((n_seq * seq_len, D_F), F32),
        compiler_params=_params(("parallel", "parallel", "arbitrary")),
        name="fox_attn_prompt",
    )(qb, kb, vb, c_col, c_row)


def _attn_sample_body(q_ref, kn_ref, vn_ref, kc_ref, vc_ref, cc_ref, crn_ref, cpr_ref, o_ref):
    n = q_ref.shape[0]
    past = kc_ref.shape[2]
    row = lax.broadcasted_iota(I32, (n, n), 0)
    col = lax.broadcasted_iota(I32, (n, n), 1)
    outs = []
    for hh in range(2):
        sl = slice(hh * HEAD_DIM, (hh + 1) * HEAD_DIM)
        q = q_ref[:, sl]
        c_col = cc_ref[0, :, hh:hh + 1]
        c_past = cpr_ref[0, 0, hh:hh + 1, :]
        tail = c_past[:, past - 1:past] - c_past
        s_past = _dot(q, kc_ref[0, 0, :, sl].astype(BF16), _NT) + c_col + tail
        s_new = _dot(q, kn_ref[:, sl], _NT) + c_col - crn_ref[0, 0, hh:hh + 1, :]
        s_new = jnp.where(row >= col, s_new, -jnp.inf)
        m = jnp.maximum(jnp.max(s_past, axis=-1, keepdims=True), jnp.max(s_new, axis=-1, keepdims=True))
        p_past = jnp.exp(s_past - m)
        p_new = jnp.exp(s_new - m)
        l = jnp.sum(p_past, axis=-1, keepdims=True) + jnp.sum(p_new, axis=-1, keepdims=True)
        acc = (_dot(p_past.astype(BF16), vc_ref[0, 0, :, sl].astype(BF16))
               + _dot(p_new.astype(BF16), vn_ref[:, sl]))
        outs.append(acc / l)
    o_ref[...] = jnp.concatenate(outs, axis=-1)


def _attn_sample(qb, kb, vb, cache_k, cache_v, layer, c_col, c_row_new, c_row_past, n_seq, n_new, row0):
    past = cache_k.shape[2]
    blk0 = row0 // n_new
    pair = lambda: pl.BlockSpec((n_new, 2 * HEAD_DIM), lambda b, p: (blk0 + b, p))
    cache = lambda: pl.BlockSpec((1, 1, past, 2 * HEAD_DIM), lambda b, p: (layer, b, 0, p))
    return pl.pallas_call(
        _attn_sample_body,
        grid=(n_seq, N_HEADS // 2),
        in_specs=[pair(), pair(), pair(), cache(), cache(),
                  pl.BlockSpec((1, n_new, 2), lambda b, p: (p, b, 0)),
                  pl.BlockSpec((1, 1, 2, n_new), lambda b, p: (b, p, 0, 0)),
                  pl.BlockSpec((1, 1, 2, past), lambda b, p: (b, p, 0, 0))],
        out_specs=pl.BlockSpec((n_new, 2 * HEAD_DIM), lambda b, p: (b, p)),
        out_shape=jax.ShapeDtypeStruct((n_seq * n_new, D_F), F32),
        compiler_params=_params(("parallel", "parallel")),
        name="fox_attn_sample",
    )(qb, kb, vb, cache_k, cache_v, c_col, c_row_new, c_row_past)


def _out_proj_body(tiles_p, ryp_ref, rys_ref, fop_ref, fos_ref, og_ref, x_ref, w_ref, g_ref, b_ref, o_ref):
    from_prompt = pl.program_id(0) < tiles_p
    ry = jnp.where(from_prompt, ryp_ref[...], rys_ref[...]).astype(BF16)
    fo = jnp.where(from_prompt, fop_ref[...], fos_ref[...])
    fy = (fo * _sigmoid(og_ref[...])).astype(BF16)
    m = _dot(ry, w_ref[0:D_R, :]) + _dot(fy, w_ref[D_R:D_R + D_F, :])
    o_ref[...] = _layer_norm(ALPHA * x_ref[...] + m, g_ref[...], b_ref[...])


def _out_proj_ln(ry_p, ry_s, fo_p, fo_s, og, x, w, g, b):
    n = x.shape[0]
    tm = 256
    tiles_p = ry_p.shape[0] // tm
    row = lambda wd: pl.BlockSpec((tm, wd), lambda i: (i, 0))
    row_p = lambda wd: pl.BlockSpec((tm, wd), lambda i: (jnp.minimum(i, tiles_p - 1), 0))
    row_s = lambda wd: pl.BlockSpec((tm, wd), lambda i: (jnp.maximum(i - tiles_p, 0), 0))
    vec = pl.BlockSpec((1, D_MODEL), lambda i: (0, 0))
    return pl.pallas_call(
        functools.partial(_out_proj_body, tiles_p),
        grid=(n // tm,),
        in_specs=[row_p(D_R), row_s(D_R), row_p(D_F), row_s(D_F), row(D_F), row(D_MODEL),
                  pl.BlockSpec((D_R + D_F, D_MODEL), lambda i: (0, 0)), vec, vec],
        out_specs=row(D_MODEL),
        out_shape=jax.ShapeDtypeStruct((n, D_MODEL), F32),
        compiler_params=_params(("parallel",)),
        name="out_proj_ln",
    )(ry_p, ry_s, fo_p, fo_s, og, x, w, g, b)


def _router_body(x_ref, rw_ref, rb_ref, e_ref, g_ref):
    tn = x_ref.shape[0]
    scores = _sigmoid(_dot3(rw_ref[...], x_ref[...], _NT))
    sel = scores + rb_ref[...]
    sel4 = sel.reshape(N_GROUPS, EXPERTS_PER_GROUP, tn)
    sc4 = scores.reshape(N_GROUPS, EXPERTS_PER_GROUP, tn)
    lane_e = lax.broadcasted_iota(I32, (N_GROUPS, EXPERTS_PER_GROUP, tn), 1)

    def top2(vals, idx_iota, axis):
        m1 = jnp.max(vals, axis=axis, keepdims=True)
        i1 = jnp.min(jnp.where(vals == m1, idx_iota, EXPERTS_PER_GROUP), axis=axis, keepdims=True)
        rest = jnp.where(idx_iota == i1, -jnp.inf, vals)
        m2 = jnp.max(rest, axis=axis, keepdims=True)
        i2 = jnp.min(jnp.where(rest == m2, idx_iota, EXPERTS_PER_GROUP), axis=axis, keepdims=True)
        return m1, i1, m2, i2

    m1, _, m2, _ = top2(sel4, lane_e, 1)
    gsum = m1 + m2
    g_iota = lax.broadcasted_iota(I32, (N_GROUPS, 1, tn), 0)
    gmax = jnp.max(gsum, axis=0, keepdims=True)
    g_idx = jnp.min(jnp.where(gsum == gmax, g_iota, N_GROUPS), axis=0, keepdims=True)
    pick = g_iota == g_idx
    sel_g = jnp.max(jnp.where(pick, sel4, -jnp.inf), axis=0)
    sc_g = jnp.max(jnp.where(pick, sc4, -jnp.inf), axis=0)
    e_iota = lax.broadcasted_iota(I32, (EXPERTS_PER_GROUP, tn), 0)
    _, i1, _, i2 = top2(sel_g, e_iota, 0)
    gate1 = jnp.sum(jnp.where(e_iota == i1, sc_g, 0.0), axis=0, keepdims=True)
    gate2 = jnp.sum(jnp.where(e_iota == i2, sc_g, 0.0), axis=0, keepdims=True)
    tot = gate1 + gate2
    base = g_idx[0] * EXPERTS_PER_GROUP
    e_ref[...] = jnp.concatenate([base + i1, base + i2], axis=0)
    g_ref[...] = jnp.concatenate([gate1 / tot, gate2 / tot], axis=0)


def _router(x, rw_t, rb_col):
    n = x.shape[0]
    tn = 512
    return pl.pallas_call(
        _router_body,
        grid=(n // tn,),
        in_specs=[pl.BlockSpec((tn, D_MODEL), lambda i: (i, 0)),
                  pl.BlockSpec((N_EXPERTS, D_MODEL), lambda i: (0, 0)),
                  pl.BlockSpec((N_EXPERTS, 1), lambda i: (0, 0))],
        out_specs=[pl.BlockSpec((TOP_K, tn), lambda i: (0, i)), pl.BlockSpec((TOP_K, tn), lambda i: (0, i))],
        out_shape=[jax.ShapeDtypeStruct((TOP_K, n), I32), jax.ShapeDtypeStruct((TOP_K, n), F32)],
        compiler_params=_params(("parallel",)),
        name="router",
    )(x, rw_t, rb_col)


def _row_copy(src_hbm, src_row, dst, dst_row, sem):
    return pltpu.make_async_copy(src_hbm.at[pl.ds(src_row, 1)], dst.at[pl.ds(dst_row, 1)], sem)


def _gather_body(tok_ref, x_hbm, o_hbm, sem):
    base = pl.program_id(0) * MOE_BLOCK

    def start(r, carry):
        _row_copy(x_hbm, tok_ref[0, 0, r], o_hbm, base + r, sem).start()
        return carry

    def wait(r, carry):
        _row_copy(x_hbm, 0, o_hbm, base + r, sem).wait()
        return carry

    lax.fori_loop(0, MOE_BLOCK, start, 0)
    lax.fori_loop(0, MOE_BLOCK, wait, 0)


def _gather_rows(tok_blocks, x):
    nb = tok_blocks.shape[0]
    return pl.pallas_call(
        _gather_body,
        grid=(nb,),
        in_specs=[pl.BlockSpec((1, 1, MOE_BLOCK), lambda i: (i, 0, 0), memory_space=pltpu.SMEM),
                  pl.BlockSpec(memory_space=pl.ANY)],
        out_specs=pl.BlockSpec(memory_space=pl.ANY),
        out_shape=jax.ShapeDtypeStruct((nb * MOE_BLOCK, D_MODEL), F32),
        scratch_shapes=[pltpu.SemaphoreType.DMA(())],
        compiler_params=_params(("arbitrary",)),
        name="moe_gather",
    )(tok_blocks, x)


def _expert_body(be_ref, h_ref, w1_ref, w3_ref, w2_ref, y_ref):
    del be_ref
    h = h_ref[...].astype(BF16)
    a = _dot(h, w1_ref[0])
    b = _dot(h, w3_ref[0])
    act = (a * _sigmoid(a) * b).astype(BF16)
    y_ref[...] = _dot(act, w2_ref[0])


def _experts(block_expert, h_pad, w1, w3, w2):
    nb = block_expert.shape[0]
    grid_spec = pltpu.PrefetchScalarGridSpec(
        num_scalar_prefetch=1,
        grid=(nb,),
        in_specs=[pl.BlockSpec((MOE_BLOCK, D_MODEL), lambda i, be: (i, 0)),
                  pl.BlockSpec((1, D_MODEL, D_EXPERT), lambda i, be: (be[i], 0, 0)),
                  pl.BlockSpec((1, D_MODEL, D_EXPERT), lambda i, be: (be[i], 0, 0)),
                  pl.BlockSpec((1, D_EXPERT, D_MODEL), lambda i, be: (be[i], 0, 0))],
        out_specs=pl.BlockSpec((MOE_BLOCK, D_MODEL), lambda i, be: (i, 0)),
    )
    return pl.pallas_call(
        _expert_body,
        grid_spec=grid_spec,
        out_shape=jax.ShapeDtypeStruct((nb * MOE_BLOCK, D_MODEL), F32),
        compiler_params=_params(("arbitrary",)),
        name="moe_experts",
    )(block_expert, h_pad, w1, w3, w2)


COMBINE_TILE = 128


def _combine_body(pos_ref, y_hbm, x_ref, gate_ref, g_ref, b_ref, o_ref, buf_ref, sem):
    def start(t, carry):
        for k in range(TOP_K):
            _row_copy(y_hbm, pos_ref[0, 0, TOP_K * t + k], buf_ref.at[k], t, sem).start()
        return carry

    def wait(t, carry):
        for k in range(TOP_K):
            _row_copy(y_hbm, 0, buf_ref.at[k], t, sem).wait()
        return carry

    lax.fori_loop(0, COMBINE_TILE, start, 0)
    lax.fori_loop(0, COMBINE_TILE, wait, 0)
    gate = gate_ref[...]
    y = gate[:, 0:1] * buf_ref[0] + gate[:, 1:2] * buf_ref[1]
    o_ref[...] = _layer_norm(ALPHA * x_ref[...] + y, g_ref[...], b_ref[...])


def _combine_ln(pos_blocks, y_pad, x, gate, g, b):
    n = x.shape[0]
    tm = COMBINE_TILE
    vec = pl.BlockSpec((1, D_MODEL), lambda i: (0, 0))
    return pl.pallas_call(
        _combine_body,
        grid=(n // tm,),
        in_specs=[pl.BlockSpec((1, 1, TOP_K * tm), lambda i: (i, 0, 0), memory_space=pltpu.SMEM),
                  pl.BlockSpec(memory_space=pl.ANY),
                  pl.BlockSpec((tm, D_MODEL), lambda i: (i, 0)),
                  pl.BlockSpec((tm, TOP_K), lambda i: (i, 0)), vec, vec],
        out_specs=pl.BlockSpec((tm, D_MODEL), lambda i: (i, 0)),
        out_shape=jax.ShapeDtypeStruct((n, D_MODEL), F32),
        scratch_shapes=[pltpu.VMEM((TOP_K, tm, D_MODEL), F32), pltpu.SemaphoreType.DMA(())],
        compiler_params=_params(("arbitrary",)),
        name="moe_combine_ln",
    )(pos_blocks, y_pad, x, gate, g, b)


def _dispatch_plan(eidx_t):
    n = eidx_t.shape[1]
    a = n * TOP_K
    flat_e = eidx_t.T.reshape(a)
    flat_tok = jnp.repeat(jnp.arange(n, dtype=I32), TOP_K)
    order = jnp.argsort(flat_e)
    se = flat_e[order]
    counts = jnp.bincount(flat_e, length=N_EXPERTS)
    starts = jnp.cumsum(counts) - counts
    padded = (counts + MOE_BLOCK - 1) // MOE_BLOCK * MOE_BLOCK
    pends = jnp.cumsum(padded)
    dest = ((pends - padded)[se] + jnp.arange(a) - starts[se]).astype(I32)
    nb = -(-a // MOE_BLOCK) + N_EXPERTS
    rows = nb * MOE_BLOCK
    tok_pad = jnp.zeros((rows,), I32).at[dest].set(flat_tok[order])
    pos = jnp.zeros((a,), I32).at[order].set(dest)
    block_expert = jnp.clip(jnp.searchsorted(pends, jnp.arange(nb) * MOE_BLOCK, side='right'),
                            0, N_EXPERTS - 1).astype(I32)
    return tok_pad.reshape(nb, 1, MOE_BLOCK), pos, block_expert


def _grouped_moe_ln(x, rw_t, rb_col, w1, w3, w2, g, b):
    n = x.shape[0]
    eidx_t, gate_t = _router(x, rw_t, rb_col)
    tok_blocks, pos, block_expert = _dispatch_plan(eidx_t)
    h_pad = _gather_rows(tok_blocks, x)
    y_pad = _experts(block_expert, h_pad, w1, w3, w2)
    pos_blocks = pos.reshape(n // COMBINE_TILE, 1, TOP_K * COMBINE_TILE)
    return _combine_ln(pos_blocks, y_pad, x, gate_t.T, g, b)


def _pair_cols(c, n_rows):
    return c[:, :N_HEADS].reshape(n_rows, N_HEADS // 2, 2).transpose(1, 0, 2)


def _pair_rows(c, n_seq, seq_len):
    return c[:, :, :N_HEADS].transpose(0, 2, 1).reshape(n_seq, N_HEADS // 2, 2, seq_len)


def kernel(x_prompt, x_sample, cache_fox_k, cache_fox_v, cache_fox_logf, state_rwkv, state_rwkv_shift, w_in, rwkv_mu, rwkv_w0, rwkv_w2, rwkv_a0, rwkv_a2, rwkv_g2, rwkv_k_k, rwkv_k_a, rwkv_r_k, rwkv_lnx_w, rwkv_lnx_b, fox_b_f, fox_q_g, fox_k_g, w_out, ln1_g, ln1_b, ln2_g, ln2_b, router_w, router_b, moe_w1, moe_w3, moe_w2):
    nb_p, seq, _ = x_prompt.shape
    nb_s, dec, _ = x_sample.shape
    depth = w_in.shape[0]
    past = cache_fox_k.shape[2]
    n_p, n_s = nb_p * seq, nb_s * dec
    n = n_p + n_s

    x = jnp.concatenate([x_prompt.reshape(n_p, D_MODEL), x_sample.reshape(n_s, D_MODEL)], axis=0)
    cache_k = cache_fox_k.reshape(depth, nb_s, past, D_F)
    cache_v = cache_fox_v.reshape(depth, nb_s, past, D_F)
    fox0 = RWKV_COLS
    fl0 = fox0 + 3 * D_F
    w_in_b = jnp.concatenate(
        [w_in[:, :, :fl0], w_in[:, :, fl0 + N_HEADS:], w_in[:, :, fl0:fl0 + N_HEADS],
         jnp.zeros((depth, D_MODEL, FL_PAD - N_HEADS), F32)], axis=-1).astype(BF16)
    w_out_b = w_out.astype(BF16)
    w1_b, w3_b, w2_b = moe_w1.astype(BF16), moe_w3.astype(BF16), moe_w2.astype(BF16)
    rw_t = router_w.T
    rb_col = router_b.reshape(N_EXPERTS, 1)
    head_of = jnp.arange(D_F, dtype=I32) // HEAD_DIM
    ones_bd = (head_of[:, None] == head_of[None, :]).astype(BF16)
    zero_shift = jnp.zeros((nb_p, 1, RWKV_COLS), F32)
    zero_state = jnp.zeros((nb_p, N_HEADS, HEAD_DIM, HEAD_DIM), F32)
    row = lambda v: v.reshape(1, -1)

    outs = {k: [] for k in ('pk', 'pv', 'pl', 'pr', 'ps', 'sk', 'sv', 'sl', 'sr', 'ss')}
    for l in range(depth):
        lp = dict(mu=row(rwkv_mu[l]), w0=row(rwkv_w0[l]), w2=rwkv_w2[l], a0=row(rwkv_a0[l]), a2=rwkv_a2[l],
                  g2=rwkv_g2[l], k_k=row(rwkv_k_k[l]), k_a=row(rwkv_k_a[l]), r_k=row(rwkv_r_k[l]),
                  lnx_w=row(rwkv_lnx_w[l]), lnx_b=row(rwkv_lnx_b[l]))
        pr, q, k, v, og, fl = _in_proj(x, w_in_b[l])

        ry_p, st_p = _rwkv(pr, zero_shift, zero_state, lp, nb_p, seq, 0)
        ry_s, st_s = _rwkv(pr, state_rwkv_shift[l], state_rwkv[l], lp, nb_s, dec, n_p)

        b_f = jnp.concatenate([fox_b_f[l], jnp.zeros((FL_PAD - N_HEADS,), F32)]).reshape(1, FL_PAD)
        qb, kn, kb, vb, lf = _fox_prep(q, k, v, fl, row(jnp.tile(fox_q_g[l], N_HEADS)),
                                       row(jnp.tile(fox_k_g[l], N_HEADS)), b_f, ones_bd)
        c_p = _cumsum_rows(lf[:n_p].reshape(nb_p, seq, FL_PAD))
        c_s = _cumsum_rows(lf[n_p:].reshape(nb_s, dec, FL_PAD))
        c_past = _cumsum_lanes(cache_fox_logf[l].transpose(0, 2, 1)).reshape(nb_s, N_HEADS // 2, 2, past)
        fo_p = _attn_prompt(qb, kb, vb, _pair_cols(c_p.reshape(n_p, FL_PAD), n_p), _pair_rows(c_p, nb_p, seq),
                            nb_p, seq)
        fo_s = _attn_sample(qb, kb, vb, cache_k, cache_v, l, _pair_cols(c_s.reshape(n_s, FL_PAD), n_s),
                            _pair_rows(c_s, nb_s, dec), c_past, nb_s, dec, n_p)

        x1 = _out_proj_ln(ry_p, ry_s, fo_p, fo_s, og, x, w_out_b[l], row(ln1_g[l]), row(ln1_b[l]))
        x = _grouped_moe_ln(x1, rw_t, rb_col, w1_b[l], w3_b[l], w2_b[l], row(ln2_g[l]), row(ln2_b[l]))

        outs['pk'].append(kn[:n_p].reshape(nb_p, seq, N_HEADS, HEAD_DIM))
        outs['pv'].append(v[:n_p].reshape(nb_p, seq, N_HEADS, HEAD_DIM))
        outs['pl'].append(lf[:n_p, :N_HEADS].reshape(nb_p, seq, N_HEADS))
        outs['pr'].append(st_p)
        outs['ps'].append(pr[:n_p].reshape(nb_p, seq, RWKV_COLS)[:, seq - 1:])
        outs['sk'].append(kn[n_p:].reshape(nb_s, dec, N_HEADS, HEAD_DIM))
        outs['sv'].append(v[n_p:].reshape(nb_s, dec, N_HEADS, HEAD_DIM))
        outs['sl'].append(lf[n_p:, :N_HEADS].reshape(nb_s, dec, N_HEADS))
        outs['sr'].append(st_s)
        outs['ss'].append(pr[n_p:].reshape(nb_s, dec, RWKV_COLS)[:, dec - 1:])

    stk = lambda key: jnp.stack(outs[key], axis=0)
    return (x[:n_p].reshape(nb_p, seq, D_MODEL), x[n_p:].reshape(nb_s, dec, D_MODEL),
            stk('pk'), stk('pv'), stk('pl'), stk('pr'), stk('ps'),
            stk('sk'), stk('sv'), stk('sl'), stk('sr'), stk('ss'))
```

```python
import functools

import jax
import jax.numpy as jnp
from jax import lax
from jax.experimental import pallas as pl
from jax.experimental.pallas import tpu as pltpu

F32 = jnp.float32
BF16 = jnp.bfloat16
I32 = jnp.int32

D_MODEL = 1024
HEAD_DIM = 64
N_HEADS = 8
D_R = N_HEADS * HEAD_DIM
D_F = N_HEADS * HEAD_DIM
DECAY_LORA = 64
A_LORA = 64
G_LORA = 128
RWKV_COLS = 3 * D_R + DECAY_LORA + A_LORA + G_LORA
FL_PAD = 128
IN_COLS_PAD = RWKV_COLS + 4 * D_F + FL_PAD
DEPTH = 2
N_EXPERTS = 32
N_GROUPS = 4
EXPERTS_PER_GROUP = N_EXPERTS // N_GROUPS
TOP_K = 2
D_EXPERT = D_MODEL // 2
MOE_BLOCK = 128
ALPHA = (2 * DEPTH) ** 0.25
LN_EPS = 1e-5
GN_EPS = 64e-5
QK_EPS = 1e-6
SCALE = HEAD_DIM ** -0.5
RWKV_CHUNK = 64
INV_BASE = 16
VMEM_LIMIT = 48 * 1024 * 1024

_NN = (((1,), (0,)), ((), ()))
_NT = (((1,), (1,)), ((), ()))
_TN = (((0,), (0,)), ((), ()))


def _dot(a, b, dims=_NN):
    return lax.dot_general(a, b, dims, preferred_element_type=F32)


def _split2(x):
    hi = x.astype(BF16)
    lo = (x - hi.astype(F32)).astype(BF16)
    return hi, lo


def _split3(x):
    hi = x.astype(BF16)
    r = x - hi.astype(F32)
    mid = r.astype(BF16)
    lo = (r - mid.astype(F32)).astype(BF16)
    return hi, mid, lo


def _dot3(a, b, dims=_NN):
    ah, al = _split2(a)
    bh, bl = _split2(b)
    return _dot(ah, bh, dims) + (_dot(ah, bl, dims) + _dot(al, bh, dims))


def _dot_exact_lhs(a_bf16, x, dims=_NN):
    hi, mid, lo = _split3(x)
    return _dot(a_bf16, hi, dims) + (_dot(a_bf16, mid, dims) + _dot(a_bf16, lo, dims))


def _sigmoid(x):
    return 1.0 / (1.0 + jnp.exp(-x))


def _softplus(x):
    return jnp.maximum(x, 0.0) + jnp.log(1.0 + jnp.exp(-jnp.abs(x)))


def _layer_norm(z, g, b):
    mu = jnp.mean(z, axis=-1, keepdims=True)
    zc = z - mu
    var = jnp.mean(zc * zc, axis=-1, keepdims=True)
    return zc * lax.rsqrt(var + LN_EPS) * g + b


def _params(sem):
    return pltpu.CompilerParams(dimension_semantics=sem, vmem_limit_bytes=VMEM_LIMIT)


_IN_SPLITS = (RWKV_COLS, D_F, D_F, D_F, D_F, FL_PAD)


def _in_proj_body(x_ref, w_ref, *out_refs):
    x = x_ref[...].astype(BF16)
    col = 0
    for ref, width in zip(out_refs, _IN_SPLITS):
        for c0 in range(0, width, 512):
            c1 = min(c0 + 512, width)
            ref[:, c0:c1] = _dot(x, w_ref[:, col + c0:col + c1])
        col += width


def _in_proj(x, w):
    n = x.shape[0]
    tm = 256
    return pl.pallas_call(
        _in_proj_body,
        grid=(n // tm,),
        in_specs=[pl.BlockSpec((tm, D_MODEL), lambda i: (i, 0)),
                  pl.BlockSpec((D_MODEL, IN_COLS_PAD), lambda i: (0, 0))],
        out_specs=[pl.BlockSpec((tm, wd), lambda i: (i, 0)) for wd in _IN_SPLITS],
        out_shape=[jax.ShapeDtypeStruct((n, wd), F32) for wd in _IN_SPLITS],
        compiler_params=_params(("parallel",)),
        name="in_proj",
    )(x, w)


def _unit_lower_inverses(ls, row, col):
    c = ls[0].shape[0]
    shift = INV_BASE.bit_length() - 1
    same = (row >> shift) == (col >> shift)
    eye = jnp.where(row == col, 1.0, 0.0)
    p = [jnp.where(same, -l, 0.0) for l in ls]
    x = [eye + n for n in p]
    for _ in range(shift - 1):
        p = [_dot3(pi, pi) for pi in p]
        x = [xi + _dot3(xi, pi) for xi, pi in zip(x, p)]
    size = 2 * INV_BASE
    while size <= c:
        s_hi = size.bit_length() - 1
        off = ((row >> s_hi) == (col >> s_hi)) & ((row >> (s_hi - 1)) != (col >> (s_hi - 1)))
        xq = [_dot3(xi, jnp.where(off, l, 0.0)) for xi, l in zip(x, ls)]
        x = [xi - _dot3(xqi, xi) for xi, xqi in zip(x, xq)]
        size *= 2
    return x


def _rwkv_body(c_len, n_chunks, pr_ref, sp_ref, s0_ref, mu_ref, w0_ref, w2_ref, a0_ref, a2_ref, g2_ref,
               kk_ref, ka_ref, rk_ref, lnw_ref, lnb_ref, ones_ref, out_ref, sout_ref, carry_ref, s_ref):
    ci = pl.program_id(1)

    @pl.when(ci == 0)
    def _():
        carry_ref[...] = sp_ref[0]
        s_ref[...] = s0_ref[0]

    pr = pr_ref[...]
    trow = lax.broadcasted_iota(I32, (c_len, 1), 0)
    prev = jnp.where(trow == 0, carry_ref[...], pltpu.roll(pr, 1, 0))
    carry_ref[...] = pr[c_len - 1:c_len, :]
    xs = pr + (prev - pr) * mu_ref[...]
    xr = xs[:, 0:D_R]
    xk = xs[:, D_R:2 * D_R]
    xv = xs[:, 2 * D_R:3 * D_R]
    o = 3 * D_R
    xw = xs[:, o:o + DECAY_LORA]
    xa = xs[:, o + DECAY_LORA:o + DECAY_LORA + A_LORA]
    xg = xs[:, o + DECAY_LORA + A_LORA:RWKV_COLS]

    z = w0_ref[...] + _dot3(jnp.tanh(xw), w2_ref[...])
    lw = -jnp.exp(-_softplus(-z) - 0.5)
    a = _sigmoid(a0_ref[...] + _dot3(xa, a2_ref[...]))
    g = _dot3(_sigmoid(xg), g2_ref[...])
    kk_raw = xk * kk_ref[...]
    k_mod = xk * (1.0 + (a - 1.0) * ka_ref[...])

    row = lax.broadcasted_iota(I32, (c_len, c_len), 0)
    col = lax.broadcasted_iota(I32, (c_len, c_len), 1)
    strict = row > col
    incl = row >= col
    cl = _dot_exact_lhs(jnp.where(incl, 1.0, 0.0).astype(BF16), lw)

    ones = ones_ref[...]

    def head_sum(x):
        hi, lo = _split2(x)
        return _dot(hi, ones) + _dot(lo, ones)

    kk = kk_raw / jnp.maximum(jnp.sqrt(head_sum(kk_raw * kk_raw)), 1e-12)
    eg = jnp.exp(cl)
    e_inv = jnp.exp(-cl)
    r_dec = xr * eg
    kk_dec = kk * jnp.exp(cl - lw)
    b_und = kk * a * e_inv
    k_und = k_mod * e_inv

    hs = [slice(h * HEAD_DIM, (h + 1) * HEAD_DIM) for h in range(N_HEADS)]
    lhs = [jnp.concatenate([kk_dec[:, s], r_dec[:, s]], axis=0) for s in hs]
    bu = [b_und[:, s] for s in hs]
    ku = [k_und[:, s] for s in hs]
    vh = [xv[:, s] for s in hs]
    gb = [_dot3(lhs[h], bu[h], _NT) for h in range(N_HEADS)]
    gk = [_dot3(lhs[h], ku[h], _NT) for h in range(N_HEADS)]
    l_b = [jnp.where(strict, m[:c_len], 0.0) for m in gb]
    m_rb = [jnp.where(incl, m[c_len:], 0.0) for m in gb]
    l_k = [jnp.where(strict, m[:c_len], 0.0) for m in gk]
    m_rk = [jnp.where(incl, m[c_len:], 0.0) for m in gk]
    t_inv = _unit_lower_inverses(l_b, row, col)
    lkv = [_dot3(l_k[h], vh[h]) for h in range(N_HEADS)]
    mkv = [_dot3(m_rk[h], vh[h]) for h in range(N_HEADS)]
    vk = [_dot3(vh[h], ku[h], _TN) for h in range(N_HEADS)]
    s0 = [s_ref[h] for h in range(N_HEADS)]
    ps = [_dot3(lhs[h], s0[h], _NT) for h in range(N_HEADS)]
    u = [-_dot3(t_inv[h], ps[h][:c_len] + lkv[h]) for h in range(N_HEADS)]
    y = [ps[h][c_len:] + mkv[h] + _dot3(m_rb[h], u[h]) for h in range(N_HEADS)]
    for h in range(N_HEADS):
        s_ref[h] = (s0[h] + vk[h] + _dot3(u[h], bu[h], _TN)) * eg[c_len - 1:c_len, hs[h]]

    y = jnp.concatenate(y, axis=-1)
    yc = y - head_sum(y) * (1.0 / HEAD_DIM)
    var = head_sum(yc * yc) * (1.0 / HEAD_DIM)
    yn = yc * lax.rsqrt(var + GN_EPS) * lnw_ref[...] + lnb_ref[...]
    bonus = head_sum(xr * k_mod * rk_ref[...]) * xv
    out_ref[...] = (yn + bonus) * g

    @pl.when(ci == n_chunks - 1)
    def _():
        sout_ref[0] = s_ref[...]


def _rwkv(pr, shift_prev, s0, lp, n_seq, seq_len, row0):
    c_len = min(RWKV_CHUNK, seq_len)
    n_chunks = seq_len // c_len
    blk0 = row0 // c_len
    vec = lambda wd: pl.BlockSpec((1, wd), lambda b, c: (0, 0))
    mat = lambda r, wd: pl.BlockSpec((r, wd), lambda b, c: (0, 0))
    in_specs = [
        pl.BlockSpec((c_len, RWKV_COLS), lambda b, c: (blk0 + b * n_chunks + c, 0)),
        pl.BlockSpec((1, 1, RWKV_COLS), lambda b, c: (b, 0, 0)),
        pl.BlockSpec((1, N_HEADS, HEAD_DIM, HEAD_DIM), lambda b, c: (b, 0, 0, 0)),
        vec(RWKV_COLS), vec(D_R), mat(DECAY_LORA, D_R), vec(D_R), mat(A_LORA, D_R), mat(G_LORA, D_R),
        vec(D_R), vec(D_R), vec(D_R), vec(D_R), vec(D_R), mat(D_R, D_R),
    ]
    args = [pr, shift_prev, s0, lp['mu'], lp['w0'], lp['w2'], lp['a0'], lp['a2'], lp['g2'],
            lp['k_k'], lp['k_a'], lp['r_k'], lp['lnx_w'], lp['lnx_b'], lp['ones_bd']]
    return pl.pallas_call(
        functools.partial(_rwkv_body, c_len, n_chunks),
        grid=(n_seq, n_chunks),
        in_specs=in_specs,
        out_specs=[pl.BlockSpec((c_len, D_R), lambda b, c: (b * n_chunks + c, 0)),
                   pl.BlockSpec((1, N_HEADS, HEAD_DIM, HEAD_DIM), lambda b, c: (b, 0, 0, 0))],
        out_shape=[jax.ShapeDtypeStruct((n_seq * seq_len, D_R), F32),
                   jax.ShapeDtypeStruct((n_seq, N_HEADS, HEAD_DIM, HEAD_DIM), F32)],
        scratch_shapes=[pltpu.VMEM((1, RWKV_COLS), F32), pltpu.VMEM((N_HEADS, HEAD_DIM, HEAD_DIM), F32)],
        compiler_params=_params(("arbitrary", "arbitrary")),
        name="rwkv_mixer",
    )(*args)


SLOT = 2 * HEAD_DIM
C_LANE = HEAD_DIM


def _fox_prep_body(q_ref, k_ref, v_ref, c_ref, qg_ref, kg_ref, ones_ref, place_ref, pcq_ref, pck_ref, oneq_ref,
                   onek_ref, qa_ref, kn_ref, ka_ref, vb_ref):
    ones = ones_ref[...]

    def rms(x, gain):
        hi, lo = _split2(x * x)
        ss = _dot(hi, ones) + _dot(lo, ones)
        return x * lax.rsqrt(ss * (1.0 / HEAD_DIM) + QK_EPS) * gain

    qn = rms(q_ref[...], qg_ref[...]) * SCALE
    kn = rms(k_ref[...], kg_ref[...])
    kn_ref[...] = kn
    vb_ref[...] = v_ref[...].astype(BF16)
    c_parts = _split3(c_ref[...])

    def slots(xb, pc_ref, one_ref):
        acc = _dot(xb, place_ref[...]) + one_ref[...]
        for j in range(3):
            acc = acc + _dot(c_parts[j], pc_ref[j])
        return acc.astype(BF16)

    qa_ref[...] = slots(qn.astype(BF16), pcq_ref, oneq_ref)
    ka_ref[...] = slots(kn.astype(BF16), pck_ref, onek_ref)


def _slot_constants():
    d = jnp.arange(D_F, dtype=I32)
    lane = jnp.arange(N_HEADS * SLOT, dtype=I32)
    place = (lane[None, :] == (d // HEAD_DIM * SLOT + d % HEAD_DIM)[:, None]).astype(BF16)
    h = jnp.arange(FL_PAD, dtype=I32)[None, :, None]
    j = jnp.arange(3, dtype=I32)[:, None, None]
    is_head = h < N_HEADS
    pcq = ((lane[None, None, :] == h * SLOT + C_LANE + j) & is_head).astype(BF16)
    pck = -((lane[None, None, :] == h * SLOT + C_LANE + 3 + j) & is_head).astype(BF16)
    in_slot = lane % SLOT
    oneq = ((in_slot >= C_LANE + 3) & (in_slot < C_LANE + 6)).astype(F32).reshape(1, -1)
    onek = ((in_slot >= C_LANE) & (in_slot < C_LANE + 3)).astype(F32).reshape(1, -1)
    return place, pcq, pck, oneq, onek


def _fox_prep(q, k, v, c, q_gain, k_gain, ones_bd, slot_consts):
    n = q.shape[0]
    tm = 256
    wide = N_HEADS * SLOT
    row = lambda wd: pl.BlockSpec((tm, wd), lambda i: (i, 0))
    vec = lambda wd: pl.BlockSpec((1, wd), lambda i: (0, 0))
    full = lambda *shape: pl.BlockSpec(shape, lambda i: (0,) * len(shape))
    return pl.pallas_call(
        _fox_prep_body,
        grid=(n // tm,),
        in_specs=[row(D_F), row(D_F), row(D_F), row(FL_PAD), vec(D_F), vec(D_F), full(D_F, D_F),
                  full(D_F, wide), full(3, FL_PAD, wide), full(3, FL_PAD, wide), vec(wide), vec(wide)],
        out_specs=[row(wide), row(D_F), row(wide), row(D_F)],
        out_shape=[jax.ShapeDtypeStruct((n, wide), BF16), jax.ShapeDtypeStruct((n, D_F), F32),
                   jax.ShapeDtypeStruct((n, wide), BF16), jax.ShapeDtypeStruct((n, D_F), BF16)],
        compiler_params=_params(("parallel",)),
        name="fox_prep",
    )(q, k, v, c, q_gain, k_gain, ones_bd, *slot_consts)


def _logf_cumsum_body(fl_ref, bf_ref, lf_ref, c_ref, carry_ref):
    @pl.when(pl.program_id(1) == 0)
    def _():
        carry_ref[...] = jnp.zeros_like(carry_ref)

    lf = -_softplus(-(fl_ref[...] + bf_ref[...]))
    lf_ref[...] = lf
    t = lf.shape[0]
    row = lax.broadcasted_iota(I32, (t, t), 0)
    col = lax.broadcasted_iota(I32, (t, t), 1)
    cs = _dot_exact_lhs(jnp.where(row >= col, 1.0, 0.0).astype(BF16), lf) + carry_ref[...]
    c_ref[...] = cs
    carry_ref[...] = cs[t - 1:t, :]


def _logf_cumsum(fl, b_f, n_seq, seq_len, row0):
    tc = min(seq_len, 256)
    nt = seq_len // tc
    blk0 = row0 // tc
    out = pl.BlockSpec((tc, FL_PAD), lambda b, j: (b * nt + j, 0))
    return pl.pallas_call(
        _logf_cumsum_body,
        grid=(n_seq, nt),
        in_specs=[pl.BlockSpec((tc, FL_PAD), lambda b, j: (blk0 + b * nt + j, 0)),
                  pl.BlockSpec((1, FL_PAD), lambda b, j: (0, 0))],
        out_specs=[out, out],
        out_shape=[jax.ShapeDtypeStruct((n_seq * seq_len, FL_PAD), F32)] * 2,
        scratch_shapes=[pltpu.VMEM((1, FL_PAD), F32)],
        compiler_params=_params(("arbitrary", "arbitrary")),
        name="logf_cumsum",
    )(fl, b_f)


def _cumsum_lanes_body(x_ref, o_ref, carry_ref):
    @pl.when(pl.program_id(1) == 0)
    def _():
        carry_ref[...] = jnp.zeros_like(carry_ref)

    x = x_ref[0]
    t = x.shape[1]
    row = lax.broadcasted_iota(I32, (t, t), 0)
    col = lax.broadcasted_iota(I32, (t, t), 1)
    upper = jnp.where(row <= col, 1.0, 0.0).astype(BF16)
    hi, mid, lo = _split3(x)
    cs = _dot(hi, upper) + (_dot(mid, upper) + _dot(lo, upper)) + carry_ref[...]
    o_ref[0] = cs
    carry_ref[...] = cs[:, t - 1:t]


def _cumsum_lanes(x):
    n_seq, h, t = x.shape
    tc = min(t, 256)
    return pl.pallas_call(
        _cumsum_lanes_body,
        grid=(n_seq, t // tc),
        in_specs=[pl.BlockSpec((1, h, tc), lambda b, j: (b, 0, j))],
        out_specs=pl.BlockSpec((1, h, tc), lambda b, j: (b, 0, j)),
        out_shape=jax.ShapeDtypeStruct((n_seq, h, t), F32),
        scratch_shapes=[pltpu.VMEM((h, 1), F32)],
        compiler_params=_params(("arbitrary", "arbitrary")),
        name="cumsum_lanes",
    )(x)


ATT_TILE = 512


def _attn_prompt_body(q_ref, k_ref, v_ref, o_ref):
    i = pl.program_id(2)
    t = ATT_TILE
    row = lax.broadcasted_iota(I32, (t, t), 0)
    col = lax.broadcasted_iota(I32, (t, t), 1)
    causal = row >= col
    pair = range(2)
    q = [q_ref[:, hh * SLOT:(hh + 1) * SLOT] for hh in pair]

    def tile(j, carry, masked):
        m, l, acc = carry
        j0 = pl.multiple_of(j * t, t)
        s = [_dot(q[hh], k_ref[pl.ds(j0, t), hh * SLOT:(hh + 1) * SLOT], _NT) for hh in pair]
        if masked:
            s = [jnp.where(causal, sh, -jnp.inf) for sh in s]
        m_new = [jnp.maximum(m[hh], jnp.max(s[hh], axis=-1, keepdims=True)) for hh in pair]
        alpha = [jnp.exp(m[hh] - m_new[hh]) for hh in pair]
        p = [jnp.exp(s[hh] - m_new[hh]) for hh in pair]
        l = [alpha[hh] * l[hh] + jnp.sum(p[hh], axis=-1, keepdims=True) for hh in pair]
        pv = [_dot(p[hh].astype(BF16), v_ref[pl.ds(j0, t), hh * HEAD_DIM:(hh + 1) * HEAD_DIM]) for hh in pair]
        acc = [alpha[hh] * acc[hh] + pv[hh] for hh in pair]
        return m_new, l, acc

    init = ([jnp.full((t, 1), -jnp.inf, F32)] * 2, [jnp.zeros((t, 1), F32)] * 2,
            [jnp.zeros((t, HEAD_DIM), F32)] * 2)
    carry = lax.fori_loop(0, i, lambda j, c: tile(j, c, False), init)
    _, l, acc = tile(i, carry, True)
    o_ref[...] = jnp.concatenate([acc[hh] / l[hh] for hh in pair], axis=-1)


def _attn_prompt(qa, ka, vb, n_seq, seq_len):
    t = ATT_TILE
    nq = seq_len // t
    return pl.pallas_call(
        _attn_prompt_body,
        grid=(n_seq, N_HEADS // 2, nq),
        in_specs=[pl.BlockSpec((t, 2 * SLOT), lambda b, p, i: (b * nq + i, p)),
                  pl.BlockSpec((seq_len, 2 * SLOT), lambda b, p, i: (b, p)),
                  pl.BlockSpec((seq_len, 2 * HEAD_DIM), lambda b, p, i: (b, p))],
        out_specs=pl.BlockSpec((t, 2 * HEAD_DIM), lambda b, p, i: (b * nq + i, p)),
        out_shape=jax.ShapeDtypeStruct((n_seq * seq_len, D_F), F32),
        compiler_params=_params(("parallel", "parallel", "arbitrary")),
        name="fox_attn_prompt",
    )(qa, ka, vb)


def _attn_sample_body(q_ref, kn_ref, vn_ref, kc_ref, vc_ref, cpr_ref, o_ref):
    n = q_ref.shape[0]
    past = kc_ref.shape[2]
    row = lax.broadcasted_iota(I32, (n, n), 0)
    col = lax.broadcasted_iota(I32, (n, n), 1)
    outs = []
    for hh in range(2):
        sl = slice(hh * HEAD_DIM, (hh + 1) * HEAD_DIM)
        q_slot = q_ref[:, hh * SLOT:(hh + 1) * SLOT]
        q = q_slot[:, :HEAD_DIM]
        c_col = (q_slot[:, C_LANE:C_LANE + 1].astype(F32) + q_slot[:, C_LANE + 1:C_LANE + 2].astype(F32)
                 + q_slot[:, C_LANE + 2:C_LANE + 3].astype(F32))
        c_past = cpr_ref[0, 0, hh:hh + 1, :]
        tail = c_past[:, past - 1:past] - c_past
        s_past = _dot(q, kc_ref[0, 0, :, sl].astype(BF16), _NT) + c_col + tail
        s_new = _dot(q_slot, kn_ref[:, hh * SLOT:(hh + 1) * SLOT], _NT)
        s_new = jnp.where(row >= col, s_new, -jnp.inf)
        m = jnp.maximum(jnp.max(s_past, axis=-1, keepdims=True), jnp.max(s_new, axis=-1, keepdims=True))
        p_past = jnp.exp(s_past - m)
        p_new = jnp.exp(s_new - m)
        l = jnp.sum(p_past, axis=-1, keepdims=True) + jnp.sum(p_new, axis=-1, keepdims=True)
        acc = (_dot(p_past.astype(BF16), vc_ref[0, 0, :, sl].astype(BF16))
               + _dot(p_new.astype(BF16), vn_ref[:, sl]))
        outs.append(acc / l)
    o_ref[...] = jnp.concatenate(outs, axis=-1)


def _attn_sample(qa, ka, vb, cache_k, cache_v, layer, c_row_past, n_seq, n_new, row0):
    past = cache_k.shape[2]
    blk0 = row0 // n_new
    pair = lambda wd: pl.BlockSpec((n_new, 2 * wd), lambda b, p: (blk0 + b, p))
    cache = lambda: pl.BlockSpec((1, 1, past, 2 * HEAD_DIM), lambda b, p: (layer, b, 0, p))
    return pl.pallas_call(
        _attn_sample_body,
        grid=(n_seq, N_HEADS // 2),
        in_specs=[pair(SLOT), pair(SLOT), pair(HEAD_DIM), cache(), cache(),
                  pl.BlockSpec((1, 1, 2, past), lambda b, p: (b, p, 0, 0))],
        out_specs=pl.BlockSpec((n_new, 2 * HEAD_DIM), lambda b, p: (b, p)),
        out_shape=jax.ShapeDtypeStruct((n_seq * n_new, D_F), F32),
        compiler_params=_params(("parallel", "parallel")),
        name="fox_attn_sample",
    )(qa, ka, vb, cache_k, cache_v, c_row_past)


def _out_proj_body(tiles_p, ryp_ref, rys_ref, fop_ref, fos_ref, og_ref, x_ref, w_ref, g_ref, b_ref, o_ref):
    from_prompt = pl.program_id(0) < tiles_p
    ry = jnp.where(from_prompt, ryp_ref[...], rys_ref[...]).astype(BF16)
    fo = jnp.where(from_prompt, fop_ref[...], fos_ref[...])
    fy = (fo * _sigmoid(og_ref[...])).astype(BF16)
    m = _dot(ry, w_ref[0:D_R, :]) + _dot(fy, w_ref[D_R:D_R + D_F, :])
    o_ref[...] = _layer_norm(ALPHA * x_ref[...] + m, g_ref[...], b_ref[...])


def _out_proj_ln(ry_p, ry_s, fo_p, fo_s, og, x, w, g, b):
    n = x.shape[0]
    tm = 256
    tiles_p = ry_p.shape[0] // tm
    row = lambda wd: pl.BlockSpec((tm, wd), lambda i: (i, 0))
    row_p = lambda wd: pl.BlockSpec((tm, wd), lambda i: (jnp.minimum(i, tiles_p - 1), 0))
    row_s = lambda wd: pl.BlockSpec((tm, wd), lambda i: (jnp.maximum(i - tiles_p, 0), 0))
    vec = pl.BlockSpec((1, D_MODEL), lambda i: (0, 0))
    return pl.pallas_call(
        functools.partial(_out_proj_body, tiles_p),
        grid=(n // tm,),
        in_specs=[row_p(D_R), row_s(D_R), row_p(D_F), row_s(D_F), row(D_F), row(D_MODEL),
                  pl.BlockSpec((D_R + D_F, D_MODEL), lambda i: (0, 0)), vec, vec],
        out_specs=row(D_MODEL),
        out_shape=jax.ShapeDtypeStruct((n, D_MODEL), F32),
        compiler_params=_params(("parallel",)),
        name="out_proj_ln",
    )(ry_p, ry_s, fo_p, fo_s, og, x, w, g, b)


def _router_body(x_ref, rw_ref, rb_ref, e_ref, g_ref):
    tn = x_ref.shape[0]
    scores = _sigmoid(_dot3(rw_ref[...], x_ref[...], _NT))
    sel = scores + rb_ref[...]
    sel4 = sel.reshape(N_GROUPS, EXPERTS_PER_GROUP, tn)
    sc4 = scores.reshape(N_GROUPS, EXPERTS_PER_GROUP, tn)
    lane_e = lax.broadcasted_iota(I32, (N_GROUPS, EXPERTS_PER_GROUP, tn), 1)

    def top2(vals, idx_iota, axis):
        m1 = jnp.max(vals, axis=axis, keepdims=True)
        i1 = jnp.min(jnp.where(vals == m1, idx_iota, EXPERTS_PER_GROUP), axis=axis, keepdims=True)
        rest = jnp.where(idx_iota == i1, -jnp.inf, vals)
        m2 = jnp.max(rest, axis=axis, keepdims=True)
        i2 = jnp.min(jnp.where(rest == m2, idx_iota, EXPERTS_PER_GROUP), axis=axis, keepdims=True)
        return m1, i1, m2, i2

    m1, _, m2, _ = top2(sel4, lane_e, 1)
    gsum = m1 + m2
    g_iota = lax.broadcasted_iota(I32, (N_GROUPS, 1, tn), 0)
    gmax = jnp.max(gsum, axis=0, keepdims=True)
    g_idx = jnp.min(jnp.where(gsum == gmax, g_iota, N_GROUPS), axis=0, keepdims=True)
    pick = g_iota == g_idx
    sel_g = jnp.max(jnp.where(pick, sel4, -jnp.inf), axis=0)
    sc_g = jnp.max(jnp.where(pick, sc4, -jnp.inf), axis=0)
    e_iota = lax.broadcasted_iota(I32, (EXPERTS_PER_GROUP, tn), 0)
    _, i1, _, i2 = top2(sel_g, e_iota, 0)
    gate1 = jnp.sum(jnp.where(e_iota == i1, sc_g, 0.0), axis=0, keepdims=True)
    gate2 = jnp.sum(jnp.where(e_iota == i2, sc_g, 0.0), axis=0, keepdims=True)
    tot = gate1 + gate2
    base = g_idx[0] * EXPERTS_PER_GROUP
    e_ref[...] = jnp.concatenate([base + i1, base + i2], axis=0)
    g_ref[...] = jnp.concatenate([gate1 / tot, gate2 / tot], axis=0)


def _router(x, rw_t, rb_col):
    n = x.shape[0]
    tn = 512
    return pl.pallas_call(
        _router_body,
        grid=(n // tn,),
        in_specs=[pl.BlockSpec((tn, D_MODEL), lambda i: (i, 0)),
                  pl.BlockSpec((N_EXPERTS, D_MODEL), lambda i: (0, 0)),
                  pl.BlockSpec((N_EXPERTS, 1), lambda i: (0, 0))],
        out_specs=[pl.BlockSpec((TOP_K, tn), lambda i: (0, i)), pl.BlockSpec((TOP_K, tn), lambda i: (0, i))],
        out_shape=[jax.ShapeDtypeStruct((TOP_K, n), I32), jax.ShapeDtypeStruct((TOP_K, n), F32)],
        compiler_params=_params(("parallel",)),
        name="router",
    )(x, rw_t, rb_col)


def _row_copy(src_hbm, src_row, dst, dst_row, sem):
    return pltpu.make_async_copy(src_hbm.at[pl.ds(src_row, 1)], dst.at[pl.ds(dst_row, 1)], sem)


def _expert_body(be_ref, tok_ref, tok_next_ref, x_hbm, w1_ref, w3_ref, w2_ref, y_ref,
                 buf_ref, w1b_ref, w3b_ref, w2b_ref, sem):
    i = pl.program_id(0)
    nb = pl.num_programs(0)
    slot = lax.rem(i, 2)

    def start_gather(ids_ref, s):
        def body(r, carry):
            _row_copy(x_hbm, ids_ref[0, 0, r], buf_ref.at[s], r, sem.at[s]).start()
            return carry
        lax.fori_loop(0, MOE_BLOCK, body, 0, unroll=8)

    @pl.when(i == 0)
    def _():
        start_gather(tok_ref, 0)

    @pl.when(i + 1 < nb)
    def _():
        start_gather(tok_next_ref, 1 - slot)

    @pl.when(jnp.logical_or(i == 0, be_ref[i] != be_ref[jnp.maximum(i - 1, 0)]))
    def _():
        w1b_ref[...] = w1_ref[0, 0].astype(BF16)
        w3b_ref[...] = w3_ref[0, 0].astype(BF16)
        w2b_ref[...] = w2_ref[0, 0].astype(BF16)

    def wait(r, carry):
        _row_copy(x_hbm, 0, buf_ref.at[slot], r, sem.at[slot]).wait()
        return carry
    lax.fori_loop(0, MOE_BLOCK, wait, 0, unroll=8)

    h = buf_ref[slot].astype(BF16)
    a = _dot(h, w1b_ref[...])
    b = _dot(h, w3b_ref[...])
    act = (a * _sigmoid(a) * b).astype(BF16)
    y_ref[...] = _dot(act, w2b_ref[...])


def _experts(block_expert, tok_blocks, x, w1, w3, w2, layer):
    nb = block_expert.shape[0]
    ids = lambda f: pl.BlockSpec((1, 1, MOE_BLOCK), f, memory_space=pltpu.SMEM)
    grid_spec = pltpu.PrefetchScalarGridSpec(
        num_scalar_prefetch=1,
        grid=(nb,),
        in_specs=[ids(lambda i, be: (i, 0, 0)),
                  ids(lambda i, be: (jnp.minimum(i + 1, nb - 1), 0, 0)),
                  pl.BlockSpec(memory_space=pl.ANY),
                  pl.BlockSpec((1, 1, D_MODEL, D_EXPERT), lambda i, be: (layer, be[i], 0, 0)),
                  pl.BlockSpec((1, 1, D_MODEL, D_EXPERT), lambda i, be: (layer, be[i], 0, 0)),
                  pl.BlockSpec((1, 1, D_EXPERT, D_MODEL), lambda i, be: (layer, be[i], 0, 0))],
        out_specs=pl.BlockSpec((MOE_BLOCK, D_MODEL), lambda i, be: (i, 0)),
        scratch_shapes=[pltpu.VMEM((2, MOE_BLOCK, D_MODEL), F32),
                        pltpu.VMEM((D_MODEL, D_EXPERT), BF16), pltpu.VMEM((D_MODEL, D_EXPERT), BF16),
                        pltpu.VMEM((D_EXPERT, D_MODEL), BF16), pltpu.SemaphoreType.DMA((2,))],
    )
    return pl.pallas_call(
        _expert_body,
        grid_spec=grid_spec,
        out_shape=jax.ShapeDtypeStruct((nb * MOE_BLOCK, D_MODEL), F32),
        compiler_params=_params(("arbitrary",)),
        name="moe_experts",
    )(block_expert, tok_blocks, tok_blocks, x, w1, w3, w2)


COMBINE_TILE = 128


def _combine_body(pos_ref, y_hbm, x_ref, gate_ref, g_ref, b_ref, o_ref, buf_ref, sem):
    def start(t, carry):
        for k in range(TOP_K):
            _row_copy(y_hbm, pos_ref[0, 0, TOP_K * t + k], buf_ref.at[k], t, sem).start()
        return carry

    def wait(t, carry):
        for k in range(TOP_K):
            _row_copy(y_hbm, 0, buf_ref.at[k], t, sem).wait()
        return carry

    lax.fori_loop(0, COMBINE_TILE, start, 0)
    lax.fori_loop(0, COMBINE_TILE, wait, 0)
    gate = gate_ref[...]
    y = gate[:, 0:1] * buf_ref[0] + gate[:, 1:2] * buf_ref[1]
    o_ref[...] = _layer_norm(ALPHA * x_ref[...] + y, g_ref[...], b_ref[...])


def _combine_ln(pos_blocks, y_pad, x, gate, g, b):
    n = x.shape[0]
    tm = COMBINE_TILE
    vec = pl.BlockSpec((1, D_MODEL), lambda i: (0, 0))
    return pl.pallas_call(
        _combine_body,
        grid=(n // tm,),
        in_specs=[pl.BlockSpec((1, 1, TOP_K * tm), lambda i: (i, 0, 0), memory_space=pltpu.SMEM),
                  pl.BlockSpec(memory_space=pl.ANY),
                  pl.BlockSpec((tm, D_MODEL), lambda i: (i, 0)),
                  pl.BlockSpec((tm, TOP_K), lambda i: (i, 0)), vec, vec],
        out_specs=pl.BlockSpec((tm, D_MODEL), lambda i: (i, 0)),
        out_shape=jax.ShapeDtypeStruct((n, D_MODEL), F32),
        scratch_shapes=[pltpu.VMEM((TOP_K, tm, D_MODEL), F32), pltpu.SemaphoreType.DMA(())],
        compiler_params=_params(("arbitrary",)),
        name="moe_combine_ln",
    )(pos_blocks, y_pad, x, gate, g, b)


def _dispatch_plan(eidx_t):
    n = eidx_t.shape[1]
    a = n * TOP_K
    flat_e = eidx_t.T.reshape(a)
    flat_tok = jnp.repeat(jnp.arange(n, dtype=I32), TOP_K)
    order = jnp.argsort(flat_e)
    se = flat_e[order]
    counts = jnp.bincount(flat_e, length=N_EXPERTS)
    starts = jnp.cumsum(counts) - counts
    padded = (counts + MOE_BLOCK - 1) // MOE_BLOCK * MOE_BLOCK
    pends = jnp.cumsum(padded)
    dest = ((pends - padded)[se] + jnp.arange(a) - starts[se]).astype(I32)
    nb = -(-a // MOE_BLOCK) + N_EXPERTS
    rows = nb * MOE_BLOCK
    tok_pad = jnp.zeros((rows,), I32).at[dest].set(flat_tok[order])
    pos = jnp.zeros((a,), I32).at[order].set(dest)
    block_expert = jnp.clip(jnp.searchsorted(pends, jnp.arange(nb) * MOE_BLOCK, side='right'),
                            0, N_EXPERTS - 1).astype(I32)
    return tok_pad.reshape(nb, 1, MOE_BLOCK), pos, block_expert


def _grouped_moe_ln(x, rw_t, rb_col, w1, w3, w2, layer, g, b):
    n = x.shape[0]
    eidx_t, gate_t = _router(x, rw_t, rb_col)
    tok_blocks, pos, block_expert = _dispatch_plan(eidx_t)
    y_pad = _experts(block_expert, tok_blocks, x, w1, w3, w2, layer)
    pos_blocks = pos.reshape(n // COMBINE_TILE, 1, TOP_K * COMBINE_TILE)
    return _combine_ln(pos_blocks, y_pad, x, gate_t.T, g, b)


def kernel(x_prompt, x_sample, cache_fox_k, cache_fox_v, cache_fox_logf, state_rwkv, state_rwkv_shift, w_in, rwkv_mu, rwkv_w0, rwkv_w2, rwkv_a0, rwkv_a2, rwkv_g2, rwkv_k_k, rwkv_k_a, rwkv_r_k, rwkv_lnx_w, rwkv_lnx_b, fox_b_f, fox_q_g, fox_k_g, w_out, ln1_g, ln1_b, ln2_g, ln2_b, router_w, router_b, moe_w1, moe_w3, moe_w2):
    nb_p, seq, _ = x_prompt.shape
    nb_s, dec, _ = x_sample.shape
    depth = w_in.shape[0]
    past = cache_fox_k.shape[2]
    n_p, n_s = nb_p * seq, nb_s * dec
    n = n_p + n_s

    x = jnp.concatenate([x_prompt.reshape(n_p, D_MODEL), x_sample.reshape(n_s, D_MODEL)], axis=0)
    cache_k = cache_fox_k.reshape(depth, nb_s, past, D_F)
    cache_v = cache_fox_v.reshape(depth, nb_s, past, D_F)
    fox0 = RWKV_COLS
    fl0 = fox0 + 3 * D_F
    w_in_b = jnp.concatenate(
        [w_in[:, :, :fl0], w_in[:, :, fl0 + N_HEADS:], w_in[:, :, fl0:fl0 + N_HEADS],
         jnp.zeros((depth, D_MODEL, FL_PAD - N_HEADS), F32)], axis=-1).astype(BF16)
    w_out_b = w_out.astype(BF16)
    rw_t = router_w.T
    rb_col = router_b.reshape(N_EXPERTS, 1)
    head_of = jnp.arange(D_F, dtype=I32) // HEAD_DIM
    ones_bd = (head_of[:, None] == head_of[None, :]).astype(BF16)
    slot_consts = _slot_constants()
    zero_shift = jnp.zeros((nb_p, 1, RWKV_COLS), F32)
    zero_state = jnp.zeros((nb_p, N_HEADS, HEAD_DIM, HEAD_DIM), F32)
    row = lambda v: v.reshape(1, -1)

    outs = {k: [] for k in ('pk', 'pv', 'pl', 'pr', 'ps', 'sk', 'sv', 'sl', 'sr', 'ss')}
    for l in range(depth):
        lp = dict(mu=row(rwkv_mu[l]), w0=row(rwkv_w0[l]), w2=rwkv_w2[l], a0=row(rwkv_a0[l]), a2=rwkv_a2[l],
                  g2=rwkv_g2[l], k_k=row(rwkv_k_k[l]), k_a=row(rwkv_k_a[l]), r_k=row(rwkv_r_k[l]),
                  lnx_w=row(rwkv_lnx_w[l]), lnx_b=row(rwkv_lnx_b[l]), ones_bd=ones_bd)
        pr, q, k, v, og, fl = _in_proj(x, w_in_b[l])

        ry_p, st_p = _rwkv(pr, zero_shift, zero_state, lp, nb_p, seq, 0)
        ry_s, st_s = _rwkv(pr, state_rwkv_shift[l], state_rwkv[l], lp, nb_s, dec, n_p)

        b_f = jnp.concatenate([fox_b_f[l], jnp.zeros((FL_PAD - N_HEADS,), F32)]).reshape(1, FL_PAD)
        lf_p, c_p = _logf_cumsum(fl, b_f, nb_p, seq, 0)
        lf_s, c_s = _logf_cumsum(fl, b_f, nb_s, dec, n_p)
        qa, kn, ka, vb = _fox_prep(q, k, v, jnp.concatenate([c_p, c_s], axis=0), row(jnp.tile(fox_q_g[l], N_HEADS)),
                                   row(jnp.tile(fox_k_g[l], N_HEADS)), ones_bd, slot_consts)
        c_past = _cumsum_lanes(cache_fox_logf[l].transpose(0, 2, 1)).reshape(nb_s, N_HEADS // 2, 2, past)
        fo_p = _attn_prompt(qa, ka, vb, nb_p, seq)
        fo_s = _attn_sample(qa, ka, vb, cache_k, cache_v, l, c_past, nb_s, dec, n_p)

        x1 = _out_proj_ln(ry_p, ry_s, fo_p, fo_s, og, x, w_out_b[l], row(ln1_g[l]), row(ln1_b[l]))
        x = _grouped_moe_ln(x1, rw_t, rb_col, moe_w1, moe_w3, moe_w2, l, row(ln2_g[l]), row(ln2_b[l]))

        outs['pk'].append(kn[:n_p].reshape(nb_p, seq, N_HEADS, HEAD_DIM))
        outs['pv'].append(v[:n_p].reshape(nb_p, seq, N_HEADS, HEAD_DIM))
        outs['pl'].append(lf_p[:, :N_HEADS].reshape(nb_p, seq, N_HEADS))
        outs['pr'].append(st_p)
        outs['ps'].append(pr[:n_p].reshape(nb_p, seq, RWKV_COLS)[:, seq - 1:])
        outs['sk'].append(kn[n_p:].reshape(nb_s, dec, N_HEADS, HEAD_DIM))
        outs['sv'].append(v[n_p:].reshape(nb_s, dec, N_HEADS, HEAD_DIM))
        outs['sl'].append(lf_s[:, :N_HEADS].reshape(nb_s, dec, N_HEADS))
        outs['sr'].append(st_s)
        outs['ss'].append(pr[n_p:].reshape(nb_s, dec, RWKV_COLS)[:, dec - 1:])

    stk = lambda key: jnp.stack(outs[key], axis=0)
    return (x[:n_p].reshape(nb_p, seq, D_MODEL), x[n_p:].reshape(nb_s, dec, D_MODEL),
            stk('pk'), stk('pv'), stk('pl'), stk('pr'), stk('ps'),
            stk('sk'), stk('sv'), stk('sl'), stk('sr'), stk('ss'))
```

```python
import functools

import jax
import jax.numpy as jnp
from jax import lax
from jax.experimental import pallas as pl
from jax.experimental.pallas import tpu as pltpu

F32 = jnp.float32
BF16 = jnp.bfloat16
I32 = jnp.int32

D_MODEL = 1024
HEAD_DIM = 64
N_HEADS = 8
D_R = N_HEADS * HEAD_DIM
D_F = N_HEADS * HEAD_DIM
DECAY_LORA = 64
A_LORA = 64
G_LORA = 128
RWKV_COLS = 3 * D_R + DECAY_LORA + A_LORA + G_LORA
FL_PAD = 128
IN_COLS_PAD = RWKV_COLS + 4 * D_F + FL_PAD
DEPTH = 2
N_EXPERTS = 32
N_GROUPS = 4
EXPERTS_PER_GROUP = N_EXPERTS // N_GROUPS
TOP_K = 2
D_EXPERT = D_MODEL // 2
MOE_BLOCK = 128
ALPHA = (2 * DEPTH) ** 0.25
LN_EPS = 1e-5
GN_EPS = 64e-5
QK_EPS = 1e-6
SCALE = HEAD_DIM ** -0.5
RWKV_CHUNK = 64
INV_BASE = 16
VMEM_LIMIT = 48 * 1024 * 1024

_NN = (((1,), (0,)), ((), ()))
_NT = (((1,), (1,)), ((), ()))
_TN = (((0,), (0,)), ((), ()))


def _dot(a, b, dims=_NN):
    return lax.dot_general(a, b, dims, preferred_element_type=F32)


def _split2(x):
    hi = x.astype(BF16)
    lo = (x - hi.astype(F32)).astype(BF16)
    return hi, lo


def _split3(x):
    hi = x.astype(BF16)
    r = x - hi.astype(F32)
    mid = r.astype(BF16)
    lo = (r - mid.astype(F32)).astype(BF16)
    return hi, mid, lo


def _dot3(a, b, dims=_NN):
    ah, al = _split2(a)
    bh, bl = _split2(b)
    return _dot(ah, bh, dims) + (_dot(ah, bl, dims) + _dot(al, bh, dims))


def _dot_exact_lhs(a_bf16, x, dims=_NN):
    hi, mid, lo = _split3(x)
    return _dot(a_bf16, hi, dims) + (_dot(a_bf16, mid, dims) + _dot(a_bf16, lo, dims))


def _sigmoid(x):
    return 1.0 / (1.0 + jnp.exp(-x))


def _softplus(x):
    return jnp.maximum(x, 0.0) + jnp.log(1.0 + jnp.exp(-jnp.abs(x)))


def _layer_norm(z, g, b):
    mu = jnp.mean(z, axis=-1, keepdims=True)
    zc = z - mu
    var = jnp.mean(zc * zc, axis=-1, keepdims=True)
    return zc * lax.rsqrt(var + LN_EPS) * g + b


def _params(sem):
    return pltpu.CompilerParams(dimension_semantics=sem, vmem_limit_bytes=VMEM_LIMIT)


_IN_SPLITS = (RWKV_COLS, D_F, D_F, D_F, D_F, FL_PAD)


def _in_proj_body(x_ref, w_ref, *out_refs):
    x = x_ref[...].astype(BF16)
    col = 0
    for ref, width in zip(out_refs, _IN_SPLITS):
        for c0 in range(0, width, 512):
            c1 = min(c0 + 512, width)
            ref[:, c0:c1] = _dot(x, w_ref[:, col + c0:col + c1])
        col += width


def _in_proj(x, w):
    n = x.shape[0]
    tm = 256
    return pl.pallas_call(
        _in_proj_body,
        grid=(n // tm,),
        in_specs=[pl.BlockSpec((tm, D_MODEL), lambda i: (i, 0)),
                  pl.BlockSpec((D_MODEL, IN_COLS_PAD), lambda i: (0, 0))],
        out_specs=[pl.BlockSpec((tm, wd), lambda i: (i, 0)) for wd in _IN_SPLITS],
        out_shape=[jax.ShapeDtypeStruct((n, wd), F32) for wd in _IN_SPLITS],
        compiler_params=_params(("parallel",)),
        name="in_proj",
    )(x, w)


def _unit_lower_inverses(ls, row, col):
    c = ls[0].shape[0]
    shift = INV_BASE.bit_length() - 1
    same = (row >> shift) == (col >> shift)
    eye = jnp.where(row == col, 1.0, 0.0)
    p = [jnp.where(same, -l, 0.0) for l in ls]
    x = [eye + n for n in p]
    for _ in range(shift - 1):
        p = [_dot3(pi, pi) for pi in p]
        x = [xi + _dot3(xi, pi) for xi, pi in zip(x, p)]
    size = 2 * INV_BASE
    while size <= c:
        s_hi = size.bit_length() - 1
        off = ((row >> s_hi) == (col >> s_hi)) & ((row >> (s_hi - 1)) != (col >> (s_hi - 1)))
        xq = [_dot3(xi, jnp.where(off, l, 0.0)) for xi, l in zip(x, ls)]
        x = [xi - _dot3(xqi, xi) for xi, xqi in zip(x, xq)]
        size *= 2
    return x


def _rwkv_body(c_len, n_chunks, pr_ref, sp_ref, s0_ref, mu_ref, w0_ref, w2_ref, a0_ref, a2_ref, g2_ref,
               kk_ref, ka_ref, rk_ref, lnw_ref, lnb_ref, ones_ref, out_ref, sout_ref, carry_ref, s_ref):
    ci = pl.program_id(1)

    @pl.when(ci == 0)
    def _():
        carry_ref[...] = sp_ref[0]
        s_ref[...] = s0_ref[0]

    pr = pr_ref[...]
    trow = lax.broadcasted_iota(I32, (c_len, 1), 0)
    prev = jnp.where(trow == 0, carry_ref[...], pltpu.roll(pr, 1, 0))
    carry_ref[...] = pr[c_len - 1:c_len, :]
    xs = pr + (prev - pr) * mu_ref[...]
    xr = xs[:, 0:D_R]
    xk = xs[:, D_R:2 * D_R]
    xv = xs[:, 2 * D_R:3 * D_R]
    o = 3 * D_R
    xw = xs[:, o:o + DECAY_LORA]
    xa = xs[:, o + DECAY_LORA:o + DECAY_LORA + A_LORA]
    xg = xs[:, o + DECAY_LORA + A_LORA:RWKV_COLS]

    z = w0_ref[...] + _dot3(jnp.tanh(xw), w2_ref[...])
    lw = -jnp.exp(-_softplus(-z) - 0.5)
    a = _sigmoid(a0_ref[...] + _dot3(xa, a2_ref[...]))
    g = _dot3(_sigmoid(xg), g2_ref[...])
    kk_raw = xk * kk_ref[...]
    k_mod = xk * (1.0 + (a - 1.0) * ka_ref[...])

    row = lax.broadcasted_iota(I32, (c_len, c_len), 0)
    col = lax.broadcasted_iota(I32, (c_len, c_len), 1)
    strict = row > col
    incl = row >= col
    cl = _dot_exact_lhs(jnp.where(incl, 1.0, 0.0).astype(BF16), lw)

    ones = ones_ref[...]

    def head_sum(x):
        hi, lo = _split2(x)
        return _dot(hi, ones) + _dot(lo, ones)

    kk = kk_raw / jnp.maximum(jnp.sqrt(head_sum(kk_raw * kk_raw)), 1e-12)
    eg = jnp.exp(cl)
    e_inv = jnp.exp(-cl)
    r_dec = xr * eg
    kk_dec = kk * jnp.exp(cl - lw)
    b_und = kk * a * e_inv
    k_und = k_mod * e_inv

    hs = [slice(h * HEAD_DIM, (h + 1) * HEAD_DIM) for h in range(N_HEADS)]
    lhs = [jnp.concatenate([kk_dec[:, s], r_dec[:, s]], axis=0) for s in hs]
    bu = [b_und[:, s] for s in hs]
    ku = [k_und[:, s] for s in hs]
    vh = [xv[:, s] for s in hs]
    gb = [_dot3(lhs[h], bu[h], _NT) for h in range(N_HEADS)]
    gk = [_dot3(lhs[h], ku[h], _NT) for h in range(N_HEADS)]
    l_b = [jnp.where(strict, m[:c_len], 0.0) for m in gb]
    m_rb = [jnp.where(incl, m[c_len:], 0.0) for m in gb]
    l_k = [jnp.where(strict, m[:c_len], 0.0) for m in gk]
    m_rk = [jnp.where(incl, m[c_len:], 0.0) for m in gk]
    t_inv = _unit_lower_inverses(l_b, row, col)
    lkv = [_dot3(l_k[h], vh[h]) for h in range(N_HEADS)]
    mkv = [_dot3(m_rk[h], vh[h]) for h in range(N_HEADS)]
    vk = [_dot3(vh[h], ku[h], _TN) for h in range(N_HEADS)]
    s0 = [s_ref[h] for h in range(N_HEADS)]
    ps = [_dot3(lhs[h], s0[h], _NT) for h in range(N_HEADS)]
    u = [-_dot3(t_inv[h], ps[h][:c_len] + lkv[h]) for h in range(N_HEADS)]
    y = [ps[h][c_len:] + mkv[h] + _dot3(m_rb[h], u[h]) for h in range(N_HEADS)]
    for h in range(N_HEADS):
        s_ref[h] = (s0[h] + vk[h] + _dot3(u[h], bu[h], _TN)) * eg[c_len - 1:c_len, hs[h]]

    y = jnp.concatenate(y, axis=-1)
    yc = y - head_sum(y) * (1.0 / HEAD_DIM)
    var = head_sum(yc * yc) * (1.0 / HEAD_DIM)
    yn = yc * lax.rsqrt(var + GN_EPS) * lnw_ref[...] + lnb_ref[...]
    bonus = head_sum(xr * k_mod * rk_ref[...]) * xv
    out_ref[...] = (yn + bonus) * g

    @pl.when(ci == n_chunks - 1)
    def _():
        sout_ref[0] = s_ref[...]


def _rwkv(pr, shift_prev, s0, lp, n_seq, seq_len, row0):
    c_len = min(RWKV_CHUNK, seq_len)
    n_chunks = seq_len // c_len
    blk0 = row0 // c_len
    vec = lambda wd: pl.BlockSpec((1, wd), lambda b, c: (0, 0))
    mat = lambda r, wd: pl.BlockSpec((r, wd), lambda b, c: (0, 0))
    in_specs = [
        pl.BlockSpec((c_len, RWKV_COLS), lambda b, c: (blk0 + b * n_chunks + c, 0)),
        pl.BlockSpec((1, 1, RWKV_COLS), lambda b, c: (b, 0, 0)),
        pl.BlockSpec((1, N_HEADS, HEAD_DIM, HEAD_DIM), lambda b, c: (b, 0, 0, 0)),
        vec(RWKV_COLS), vec(D_R), mat(DECAY_LORA, D_R), vec(D_R), mat(A_LORA, D_R), mat(G_LORA, D_R),
        vec(D_R), vec(D_R), vec(D_R), vec(D_R), vec(D_R), mat(D_R, D_R),
    ]
    args = [pr, shift_prev, s0, lp['mu'], lp['w0'], lp['w2'], lp['a0'], lp['a2'], lp['g2'],
            lp['k_k'], lp['k_a'], lp['r_k'], lp['lnx_w'], lp['lnx_b'], lp['ones_bd']]
    return pl.pallas_call(
        functools.partial(_rwkv_body, c_len, n_chunks),
        grid=(n_seq, n_chunks),
        in_specs=in_specs,
        out_specs=[pl.BlockSpec((c_len, D_R), lambda b, c: (b * n_chunks + c, 0)),
                   pl.BlockSpec((1, N_HEADS, HEAD_DIM, HEAD_DIM), lambda b, c: (b, 0, 0, 0))],
        out_shape=[jax.ShapeDtypeStruct((n_seq * seq_len, D_R), F32),
                   jax.ShapeDtypeStruct((n_seq, N_HEADS, HEAD_DIM, HEAD_DIM), F32)],
        scratch_shapes=[pltpu.VMEM((1, RWKV_COLS), F32), pltpu.VMEM((N_HEADS, HEAD_DIM, HEAD_DIM), F32)],
        compiler_params=_params(("arbitrary", "arbitrary")),
        name="rwkv_mixer",
    )(*args)


SLOT = 2 * HEAD_DIM
C_LANE = HEAD_DIM


def _fox_prep_body(q_ref, k_ref, v_ref, c_ref, qg_ref, kg_ref, ones_ref, place_ref, pcq_ref, pck_ref, oneq_ref,
                   onek_ref, qa_ref, kn_ref, ka_ref, vb_ref):
    ones = ones_ref[...]

    def rms(x, gain):
        hi, lo = _split2(x * x)
        ss = _dot(hi, ones) + _dot(lo, ones)
        return x * lax.rsqrt(ss * (1.0 / HEAD_DIM) + QK_EPS) * gain

    qn = rms(q_ref[...], qg_ref[...]) * SCALE
    kn = rms(k_ref[...], kg_ref[...])
    kn_ref[...] = kn
    vb_ref[...] = v_ref[...].astype(BF16)
    c_parts = _split3(c_ref[...])

    def slots(xb, pc_ref, one_ref):
        acc = _dot(xb, place_ref[...]) + one_ref[...]
        for j in range(3):
            acc = acc + _dot(c_parts[j], pc_ref[j])
        return acc.astype(BF16)

    qa_ref[...] = slots(qn.astype(BF16), pcq_ref, oneq_ref)
    ka_ref[...] = slots(kn.astype(BF16), pck_ref, onek_ref)


def _slot_constants():
    d = jnp.arange(D_F, dtype=I32)
    lane = jnp.arange(N_HEADS * SLOT, dtype=I32)
    place = (lane[None, :] == (d // HEAD_DIM * SLOT + d % HEAD_DIM)[:, None]).astype(BF16)
    h = jnp.arange(FL_PAD, dtype=I32)[None, :, None]
    j = jnp.arange(3, dtype=I32)[:, None, None]
    is_head = h < N_HEADS
    pcq = ((lane[None, None, :] == h * SLOT + C_LANE + j) & is_head).astype(BF16)
    pck = -((lane[None, None, :] == h * SLOT + C_LANE + 3 + j) & is_head).astype(BF16)
    in_slot = lane % SLOT
    oneq = ((in_slot >= C_LANE + 3) & (in_slot < C_LANE + 6)).astype(F32).reshape(1, -1)
    onek = ((in_slot >= C_LANE) & (in_slot < C_LANE + 3)).astype(F32).reshape(1, -1)
    return place, pcq, pck, oneq, onek


def _fox_prep(q, k, v, c, q_gain, k_gain, ones_bd, slot_consts):
    n = q.shape[0]
    tm = 256
    wide = N_HEADS * SLOT
    row = lambda wd: pl.BlockSpec((tm, wd), lambda i: (i, 0))
    vec = lambda wd: pl.BlockSpec((1, wd), lambda i: (0, 0))
    full = lambda *shape: pl.BlockSpec(shape, lambda i: (0,) * len(shape))
    return pl.pallas_call(
        _fox_prep_body,
        grid=(n // tm,),
        in_specs=[row(D_F), row(D_F), row(D_F), row(FL_PAD), vec(D_F), vec(D_F), full(D_F, D_F),
                  full(D_F, wide), full(3, FL_PAD, wide), full(3, FL_PAD, wide), vec(wide), vec(wide)],
        out_specs=[row(wide), row(D_F), row(wide), row(D_F)],
        out_shape=[jax.ShapeDtypeStruct((n, wide), BF16), jax.ShapeDtypeStruct((n, D_F), F32),
                   jax.ShapeDtypeStruct((n, wide), BF16), jax.ShapeDtypeStruct((n, D_F), BF16)],
        compiler_params=_params(("parallel",)),
        name="fox_prep",
    )(q, k, v, c, q_gain, k_gain, ones_bd, *slot_consts)


def _logf_cumsum_body(fl_ref, bf_ref, lf_ref, c_ref, carry_ref):
    @pl.when(pl.program_id(1) == 0)
    def _():
        carry_ref[...] = jnp.zeros_like(carry_ref)

    lf = -_softplus(-(fl_ref[...] + bf_ref[...]))
    lf_ref[...] = lf
    t = lf.shape[0]
    row = lax.broadcasted_iota(I32, (t, t), 0)
    col = lax.broadcasted_iota(I32, (t, t), 1)
    cs = _dot_exact_lhs(jnp.where(row >= col, 1.0, 0.0).astype(BF16), lf) + carry_ref[...]
    c_ref[...] = cs
    carry_ref[...] = cs[t - 1:t, :]


def _logf_cumsum(fl, b_f, n_seq, seq_len, row0):
    tc = min(seq_len, 256)
    nt = seq_len // tc
    blk0 = row0 // tc
    out = pl.BlockSpec((tc, FL_PAD), lambda b, j: (b * nt + j, 0))
    return pl.pallas_call(
        _logf_cumsum_body,
        grid=(n_seq, nt),
        in_specs=[pl.BlockSpec((tc, FL_PAD), lambda b, j: (blk0 + b * nt + j, 0)),
                  pl.BlockSpec((1, FL_PAD), lambda b, j: (0, 0))],
        out_specs=[out, out],
        out_shape=[jax.ShapeDtypeStruct((n_seq * seq_len, FL_PAD), F32)] * 2,
        scratch_shapes=[pltpu.VMEM((1, FL_PAD), F32)],
        compiler_params=_params(("arbitrary", "arbitrary")),
        name="logf_cumsum",
    )(fl, b_f)


ATT_TILE = 512


def _attn_prompt_body(q_ref, k_ref, v_ref, o_ref):
    i = pl.program_id(2)
    t = ATT_TILE
    row = lax.broadcasted_iota(I32, (t, t), 0)
    col = lax.broadcasted_iota(I32, (t, t), 1)
    causal = row >= col
    pair = range(2)
    q = [q_ref[:, hh * SLOT:(hh + 1) * SLOT] for hh in pair]

    def tile(j, carry, masked):
        m, l, acc = carry
        j0 = pl.multiple_of(j * t, t)
        s = [_dot(q[hh], k_ref[pl.ds(j0, t), hh * SLOT:(hh + 1) * SLOT], _NT) for hh in pair]
        if masked:
            s = [jnp.where(causal, sh, -jnp.inf) for sh in s]
        m_new = [jnp.maximum(m[hh], jnp.max(s[hh], axis=-1, keepdims=True)) for hh in pair]
        alpha = [jnp.exp(m[hh] - m_new[hh]) for hh in pair]
        p = [jnp.exp(s[hh] - m_new[hh]) for hh in pair]
        l = [alpha[hh] * l[hh] + jnp.sum(p[hh], axis=-1, keepdims=True) for hh in pair]
        pv = [_dot(p[hh].astype(BF16), v_ref[pl.ds(j0, t), hh * HEAD_DIM:(hh + 1) * HEAD_DIM]) for hh in pair]
        acc = [alpha[hh] * acc[hh] + pv[hh] for hh in pair]
        return m_new, l, acc

    init = ([jnp.full((t, 1), -jnp.inf, F32)] * 2, [jnp.zeros((t, 1), F32)] * 2,
            [jnp.zeros((t, HEAD_DIM), F32)] * 2)
    carry = lax.fori_loop(0, i, lambda j, c: tile(j, c, False), init)
    _, l, acc = tile(i, carry, True)
    o_ref[...] = jnp.concatenate([acc[hh] / l[hh] for hh in pair], axis=-1)


def _attn_prompt(qa, ka, vb, n_seq, seq_len):
    t = ATT_TILE
    nq = seq_len // t
    return pl.pallas_call(
        _attn_prompt_body,
        grid=(n_seq, N_HEADS // 2, nq),
        in_specs=[pl.BlockSpec((t, 2 * SLOT), lambda b, p, i: (b * nq + i, p)),
                  pl.BlockSpec((seq_len, 2 * SLOT), lambda b, p, i: (b, p)),
                  pl.BlockSpec((seq_len, 2 * HEAD_DIM), lambda b, p, i: (b, p))],
        out_specs=pl.BlockSpec((t, 2 * HEAD_DIM), lambda b, p, i: (b * nq + i, p)),
        out_shape=jax.ShapeDtypeStruct((n_seq * seq_len, D_F), F32),
        compiler_params=_params(("parallel", "parallel", "arbitrary")),
        name="fox_attn_prompt",
    )(qa, ka, vb)


SAMPLE_CHUNK = 1024


def _attn_sample_body(n_chunks, q_ref, kn_ref, vn_ref, kc_ref, vc_ref, lp_ref, after_ref, o_ref,
                      m_ref, l_ref, acc_ref, suffix_ref):
    j = pl.program_id(1)
    n = q_ref.shape[0]
    heads = range(N_HEADS)

    @pl.when(j == 0)
    def _():
        m_ref[...] = jnp.full(m_ref.shape, -jnp.inf, F32)
        l_ref[...] = jnp.zeros_like(l_ref)
        acc_ref[...] = jnp.zeros_like(acc_ref)
        suffix_ref[...] = jnp.zeros_like(suffix_ref)

    lp = lp_ref[0, 0]
    hi, mid, lo = _split3(lp)
    after = after_ref[...]
    tail = _dot(hi, after) + (_dot(mid, after) + _dot(lo, after)) + suffix_ref[...]
    suffix_ref[...] = suffix_ref[...] + jnp.sum(lp, axis=-1, keepdims=True)

    q_slot = [q_ref[:, h * SLOT:(h + 1) * SLOT] for h in heads]
    c_col = [(qs[:, C_LANE:C_LANE + 1].astype(F32) + qs[:, C_LANE + 1:C_LANE + 2].astype(F32)
              + qs[:, C_LANE + 2:C_LANE + 3].astype(F32)) for qs in q_slot]
    tp = lp.shape[1]
    head_rows = lambda ref, h: ref[0, 0, pl.ds(h, tp, stride=N_HEADS), :].astype(BF16)
    s = [_dot(q_slot[h][:, :HEAD_DIM], head_rows(kc_ref, h), _NT) + c_col[h] + tail[h:h + 1, :] for h in heads]
    m_old = [m_ref[h] for h in heads]
    m_new = [jnp.maximum(m_old[h], jnp.max(s[h], axis=-1, keepdims=True)) for h in heads]
    alpha = [jnp.exp(m_old[h] - m_new[h]) for h in heads]
    p = [jnp.exp(s[h] - m_new[h]) for h in heads]
    pv = [_dot(p[h].astype(BF16), head_rows(vc_ref, h)) for h in heads]
    for h in heads:
        m_ref[h] = m_new[h]
        l_ref[h] = alpha[h] * l_ref[h] + jnp.sum(p[h], axis=-1, keepdims=True)
        acc_ref[h] = alpha[h] * acc_ref[h] + pv[h]

    @pl.when(j == n_chunks - 1)
    def _():
        row = lax.broadcasted_iota(I32, (n, n), 0)
        col = lax.broadcasted_iota(I32, (n, n), 1)
        s_new = [jnp.where(row >= col, _dot(q_slot[h], kn_ref[:, h * SLOT:(h + 1) * SLOT], _NT), -jnp.inf)
                 for h in heads]
        m_fin = [jnp.maximum(m_ref[h], jnp.max(s_new[h], axis=-1, keepdims=True)) for h in heads]
        a_fin = [jnp.exp(m_ref[h] - m_fin[h]) for h in heads]
        p_new = [jnp.exp(s_new[h] - m_fin[h]) for h in heads]
        l_fin = [a_fin[h] * l_ref[h] + jnp.sum(p_new[h], axis=-1, keepdims=True) for h in heads]
        acc = [a_fin[h] * acc_ref[h] + _dot(p_new[h].astype(BF16), vn_ref[:, h * HEAD_DIM:(h + 1) * HEAD_DIM])
               for h in heads]
        o_ref[...] = jnp.concatenate([acc[h] / l_fin[h] for h in heads], axis=-1)


def _attn_sample(qa, ka, vb, cache_k, cache_v, logf_rows, layer, n_seq, n_new, row0):
    past = cache_k.shape[2] // N_HEADS
    tp = min(SAMPLE_CHUNK, past)
    n_chunks = past // tp
    blk0 = row0 // n_new
    frame = jnp.arange(tp, dtype=I32)
    after = (frame[:, None] > frame[None, :]).astype(BF16)
    rows = lambda wd: pl.BlockSpec((n_new, wd), lambda b, j: (blk0 + b, 0))
    cache = lambda: pl.BlockSpec((1, 1, tp * N_HEADS, HEAD_DIM), lambda b, j: (layer, b, n_chunks - 1 - j, 0))
    return pl.pallas_call(
        functools.partial(_attn_sample_body, n_chunks),
        grid=(n_seq, n_chunks),
        in_specs=[rows(N_HEADS * SLOT), rows(N_HEADS * SLOT), rows(D_F), cache(), cache(),
                  pl.BlockSpec((1, 1, N_HEADS, tp), lambda b, j: (layer, b, 0, n_chunks - 1 - j)),
                  pl.BlockSpec((tp, tp), lambda b, j: (0, 0))],
        out_specs=pl.BlockSpec((n_new, D_F), lambda b, j: (b, 0)),
        out_shape=jax.ShapeDtypeStruct((n_seq * n_new, D_F), F32),
        scratch_shapes=[pltpu.VMEM((N_HEADS, n_new, 1), F32), pltpu.VMEM((N_HEADS, n_new, 1), F32),
                        pltpu.VMEM((N_HEADS, n_new, HEAD_DIM), F32), pltpu.VMEM((N_HEADS, 1), F32)],
        compiler_params=_params(("parallel", "arbitrary")),
        name="fox_attn_sample",
    )(qa, ka, vb, cache_k, cache_v, logf_rows, after)


def _out_proj_body(tiles_p, ryp_ref, rys_ref, fop_ref, fos_ref, og_ref, x_ref, w_ref, g_ref, b_ref, o_ref):
    from_prompt = pl.program_id(0) < tiles_p
    ry = jnp.where(from_prompt, ryp_ref[...], rys_ref[...]).astype(BF16)
    fo = jnp.where(from_prompt, fop_ref[...], fos_ref[...])
    fy = (fo * _sigmoid(og_ref[...])).astype(BF16)
    m = _dot(ry, w_ref[0:D_R, :]) + _dot(fy, w_ref[D_R:D_R + D_F, :])
    o_ref[...] = _layer_norm(ALPHA * x_ref[...] + m, g_ref[...], b_ref[...])


def _out_proj_ln(ry_p, ry_s, fo_p, fo_s, og, x, w, g, b):
    n = x.shape[0]
    tm = 256
    tiles_p = ry_p.shape[0] // tm
    row = lambda wd: pl.BlockSpec((tm, wd), lambda i: (i, 0))
    row_p = lambda wd: pl.BlockSpec((tm, wd), lambda i: (jnp.minimum(i, tiles_p - 1), 0))
    row_s = lambda wd: pl.BlockSpec((tm, wd), lambda i: (jnp.maximum(i - tiles_p, 0), 0))
    vec = pl.BlockSpec((1, D_MODEL), lambda i: (0, 0))
    return pl.pallas_call(
        functools.partial(_out_proj_body, tiles_p),
        grid=(n // tm,),
        in_specs=[row_p(D_R), row_s(D_R), row_p(D_F), row_s(D_F), row(D_F), row(D_MODEL),
                  pl.BlockSpec((D_R + D_F, D_MODEL), lambda i: (0, 0)), vec, vec],
        out_specs=row(D_MODEL),
        out_shape=jax.ShapeDtypeStruct((n, D_MODEL), F32),
        compiler_params=_params(("parallel",)),
        name="out_proj_ln",
    )(ry_p, ry_s, fo_p, fo_s, og, x, w, g, b)


def _router_body(x_ref, rw_ref, rb_ref, earlier_ref, e_ref, g_ref, r_ref, count_ref, seen_ref):
    tn = x_ref.shape[0]
    scores = _sigmoid(_dot3(rw_ref[...], x_ref[...], _NT))
    sel = scores + rb_ref[...]
    sel4 = sel.reshape(N_GROUPS, EXPERTS_PER_GROUP, tn)
    sc4 = scores.reshape(N_GROUPS, EXPERTS_PER_GROUP, tn)
    lane_e = lax.broadcasted_iota(I32, (N_GROUPS, EXPERTS_PER_GROUP, tn), 1)

    def top2(vals, idx_iota, axis):
        m1 = jnp.max(vals, axis=axis, keepdims=True)
        i1 = jnp.min(jnp.where(vals == m1, idx_iota, EXPERTS_PER_GROUP), axis=axis, keepdims=True)
        rest = jnp.where(idx_iota == i1, -jnp.inf, vals)
        m2 = jnp.max(rest, axis=axis, keepdims=True)
        i2 = jnp.min(jnp.where(rest == m2, idx_iota, EXPERTS_PER_GROUP), axis=axis, keepdims=True)
        return m1, i1, m2, i2

    m1, _, m2, _ = top2(sel4, lane_e, 1)
    gsum = m1 + m2
    g_iota = lax.broadcasted_iota(I32, (N_GROUPS, 1, tn), 0)
    gmax = jnp.max(gsum, axis=0, keepdims=True)
    g_idx = jnp.min(jnp.where(gsum == gmax, g_iota, N_GROUPS), axis=0, keepdims=True)
    pick = g_iota == g_idx
    sel_g = jnp.max(jnp.where(pick, sel4, -jnp.inf), axis=0)
    sc_g = jnp.max(jnp.where(pick, sc4, -jnp.inf), axis=0)
    e_iota = lax.broadcasted_iota(I32, (EXPERTS_PER_GROUP, tn), 0)
    _, i1, _, i2 = top2(sel_g, e_iota, 0)
    gate1 = jnp.sum(jnp.where(e_iota == i1, sc_g, 0.0), axis=0, keepdims=True)
    gate2 = jnp.sum(jnp.where(e_iota == i2, sc_g, 0.0), axis=0, keepdims=True)
    tot = gate1 + gate2
    base = g_idx[0] * EXPERTS_PER_GROUP
    e1 = base + i1
    e2 = base + i2
    e_ref[...] = jnp.concatenate([e1, e2], axis=0)
    g_ref[...] = jnp.concatenate([gate1 / tot, gate2 / tot], axis=0)

    @pl.when(pl.program_id(0) == 0)
    def _():
        seen_ref[...] = jnp.zeros_like(seen_ref)

    all_e = lax.broadcasted_iota(I32, (N_EXPERTS, tn), 0)
    hit1 = all_e == e1
    hit2 = all_e == e2
    hits = jnp.where(hit1 | hit2, 1.0, 0.0)
    before = _dot(hits.astype(BF16), earlier_ref[...]) + seen_ref[...]
    r1 = jnp.sum(jnp.where(hit1, before, 0.0), axis=0, keepdims=True)
    r2 = jnp.sum(jnp.where(hit2, before, 0.0), axis=0, keepdims=True)
    r_ref[...] = jnp.concatenate([r1, r2], axis=0).astype(I32)
    seen_ref[...] = seen_ref[...] + jnp.sum(hits, axis=-1, keepdims=True)
    count_ref[...] = seen_ref[...].astype(I32)


def _router(x, rw_t, rb_col):
    n = x.shape[0]
    tn = 512
    tok = jnp.arange(tn, dtype=I32)
    earlier = (tok[:, None] < tok[None, :]).astype(BF16)
    pair = lambda: pl.BlockSpec((TOP_K, tn), lambda i: (0, i))
    return pl.pallas_call(
        _router_body,
        grid=(n // tn,),
        in_specs=[pl.BlockSpec((tn, D_MODEL), lambda i: (i, 0)),
                  pl.BlockSpec((N_EXPERTS, D_MODEL), lambda i: (0, 0)),
                  pl.BlockSpec((N_EXPERTS, 1), lambda i: (0, 0)),
                  pl.BlockSpec((tn, tn), lambda i: (0, 0))],
        out_specs=[pair(), pair(), pair(), pl.BlockSpec((N_EXPERTS, 1), lambda i: (0, 0))],
        out_shape=[jax.ShapeDtypeStruct((TOP_K, n), I32), jax.ShapeDtypeStruct((TOP_K, n), F32),
                   jax.ShapeDtypeStruct((TOP_K, n), I32), jax.ShapeDtypeStruct((N_EXPERTS, 1), I32)],
        scratch_shapes=[pltpu.VMEM((N_EXPERTS, 1), F32)],
        compiler_params=_params(("arbitrary",)),
        name="router",
    )(x, rw_t, rb_col, earlier)


def _slot_rows_body(e_ref, r_ref, start_ref, pos_ref):
    tn = e_ref.shape[1]
    all_e = lax.broadcasted_iota(I32, (N_EXPERTS, tn), 0)
    rows = [jnp.sum(jnp.where(all_e == e_ref[k:k + 1, :], start_ref[...], 0), axis=0, keepdims=True)
            for k in range(TOP_K)]
    pos_ref[...] = r_ref[...] + jnp.concatenate(rows, axis=0)


def _slot_rows(eidx_t, rank_t, expert_start):
    n = eidx_t.shape[1]
    tn = 512
    pair = lambda: pl.BlockSpec((TOP_K, tn), lambda i: (0, i))
    return pl.pallas_call(
        _slot_rows_body,
        grid=(n // tn,),
        in_specs=[pair(), pair(), pl.BlockSpec((N_EXPERTS, 1), lambda i: (0, 0))],
        out_specs=pair(),
        out_shape=jax.ShapeDtypeStruct((TOP_K, n), I32),
        compiler_params=_params(("parallel",)),
        name="moe_slot_rows",
    )(eidx_t, rank_t, expert_start)


def _row_copy(src_hbm, src_row, dst, dst_row, sem):
    return pltpu.make_async_copy(src_hbm.at[pl.ds(src_row, 1)], dst.at[pl.ds(dst_row, 1)], sem)


def _expert_body(be_ref, tok_ref, tok_next_ref, x_hbm, w1_ref, w3_ref, w2_ref, y_ref,
                 buf_ref, w1b_ref, w3b_ref, w2b_ref, sem):
    i = pl.program_id(0)
    nb = pl.num_programs(0)
    slot = lax.rem(i, 2)

    def start_gather(ids_ref, s):
        def body(r, carry):
            _row_copy(x_hbm, ids_ref[0, 0, r], buf_ref.at[s], r, sem.at[s]).start()
            return carry
        lax.fori_loop(0, MOE_BLOCK, body, 0, unroll=8)

    @pl.when(i == 0)
    def _():
        start_gather(tok_ref, 0)

    @pl.when(i + 1 < nb)
    def _():
        start_gather(tok_next_ref, 1 - slot)

    @pl.when(jnp.logical_or(i == 0, be_ref[i] != be_ref[jnp.maximum(i - 1, 0)]))
    def _():
        w1b_ref[...] = w1_ref[0, 0].astype(BF16)
        w3b_ref[...] = w3_ref[0, 0].astype(BF16)
        w2b_ref[...] = w2_ref[0, 0].astype(BF16)

    def wait(r, carry):
        _row_copy(x_hbm, 0, buf_ref.at[slot], r, sem.at[slot]).wait()
        return carry
    lax.fori_loop(0, MOE_BLOCK, wait, 0, unroll=8)

    h = buf_ref[slot].astype(BF16)
    a = _dot(h, w1b_ref[...])
    b = _dot(h, w3b_ref[...])
    act = (a * _sigmoid(a) * b).astype(BF16)
    y_ref[...] = _dot(act, w2b_ref[...])


def _experts(block_expert, tok_blocks, x, w1, w3, w2, layer):
    nb = block_expert.shape[0]
    ids = lambda f: pl.BlockSpec((1, 1, MOE_BLOCK), f, memory_space=pltpu.SMEM)
    grid_spec = pltpu.PrefetchScalarGridSpec(
        num_scalar_prefetch=1,
        grid=(nb,),
        in_specs=[ids(lambda i, be: (i, 0, 0)),
                  ids(lambda i, be: (jnp.minimum(i + 1, nb - 1), 0, 0)),
                  pl.BlockSpec(memory_space=pl.ANY),
                  pl.BlockSpec((1, 1, D_MODEL, D_EXPERT), lambda i, be: (layer, be[i], 0, 0)),
                  pl.BlockSpec((1, 1, D_MODEL, D_EXPERT), lambda i, be: (layer, be[i], 0, 0)),
                  pl.BlockSpec((1, 1, D_EXPERT, D_MODEL), lambda i, be: (layer, be[i], 0, 0))],
        out_specs=pl.BlockSpec((MOE_BLOCK, D_MODEL), lambda i, be: (i, 0)),
        scratch_shapes=[pltpu.VMEM((2, MOE_BLOCK, D_MODEL), F32),
                        pltpu.VMEM((D_MODEL, D_EXPERT), BF16), pltpu.VMEM((D_MODEL, D_EXPERT), BF16),
                        pltpu.VMEM((D_EXPERT, D_MODEL), BF16), pltpu.SemaphoreType.DMA((2,))],
    )
    return pl.pallas_call(
        _expert_body,
        grid_spec=grid_spec,
        out_shape=jax.ShapeDtypeStruct((nb * MOE_BLOCK, D_MODEL), F32),
        compiler_params=_params(("arbitrary",)),
        name="moe_experts",
    )(block_expert, tok_blocks, tok_blocks, x, w1, w3, w2)


COMBINE_TILE = 128


def _combine_body(pos_ref, pos_next_ref, y_hbm, x_ref, gate_ref, g_ref, b_ref, o_ref, buf_ref, sem):
    i = pl.program_id(0)
    slot = lax.rem(i, 2)

    def start_gather(ids_ref, s):
        def body(t, carry):
            for k in range(TOP_K):
                _row_copy(y_hbm, ids_ref[0, 0, TOP_K * t + k], buf_ref.at[s, k], t, sem.at[s]).start()
            return carry
        lax.fori_loop(0, COMBINE_TILE, body, 0, unroll=4)

    @pl.when(i == 0)
    def _():
        start_gather(pos_ref, 0)

    @pl.when(i + 1 < pl.num_programs(0))
    def _():
        start_gather(pos_next_ref, 1 - slot)

    def wait(t, carry):
        for k in range(TOP_K):
            _row_copy(y_hbm, 0, buf_ref.at[slot, k], t, sem.at[slot]).wait()
        return carry
    lax.fori_loop(0, COMBINE_TILE, wait, 0, unroll=4)

    gate = gate_ref[...]
    y = gate[:, 0:1] * buf_ref[slot, 0] + gate[:, 1:2] * buf_ref[slot, 1]
    o_ref[...] = _layer_norm(ALPHA * x_ref[...] + y, g_ref[...], b_ref[...])


def _combine_ln(pos_blocks, y_pad, x, gate, g, b):
    n = x.shape[0]
    tm = COMBINE_TILE
    nt = n // tm
    vec = pl.BlockSpec((1, D_MODEL), lambda i: (0, 0))
    ids = lambda f: pl.BlockSpec((1, 1, TOP_K * tm), f, memory_space=pltpu.SMEM)
    return pl.pallas_call(
        _combine_body,
        grid=(nt,),
        in_specs=[ids(lambda i: (i, 0, 0)), ids(lambda i: (jnp.minimum(i + 1, nt - 1), 0, 0)),
                  pl.BlockSpec(memory_space=pl.ANY),
                  pl.BlockSpec((tm, D_MODEL), lambda i: (i, 0)),
                  pl.BlockSpec((tm, TOP_K), lambda i: (i, 0)), vec, vec],
        out_specs=pl.BlockSpec((tm, D_MODEL), lambda i: (i, 0)),
        out_shape=jax.ShapeDtypeStruct((n, D_MODEL), F32),
        scratch_shapes=[pltpu.VMEM((2, TOP_K, tm, D_MODEL), F32), pltpu.SemaphoreType.DMA((2,))],
        compiler_params=_params(("arbitrary",)),
        name="moe_combine_ln",
    )(pos_blocks, pos_blocks, y_pad, x, gate, g, b)


def _grouped_moe_ln(x, rw_t, rb_col, w1, w3, w2, layer, g, b):
    n = x.shape[0]
    eidx_t, gate_t, rank_t, counts = _router(x, rw_t, rb_col)
    padded = (counts[:, 0] + MOE_BLOCK - 1) // MOE_BLOCK * MOE_BLOCK
    ends = jnp.cumsum(padded)
    nb = -(-n * TOP_K // MOE_BLOCK) + N_EXPERTS
    block_expert = jnp.clip(jnp.searchsorted(ends, jnp.arange(nb, dtype=I32) * MOE_BLOCK, side='right'),
                            0, N_EXPERTS - 1).astype(I32)
    pos_t = _slot_rows(eidx_t, rank_t, (ends - padded).astype(I32).reshape(N_EXPERTS, 1))
    pos = pos_t.T
    tok = jnp.broadcast_to(jnp.arange(n, dtype=I32)[:, None], (n, TOP_K))
    tok_pad = jnp.zeros((nb * MOE_BLOCK,), I32).at[pos.reshape(-1)].set(tok.reshape(-1), unique_indices=True)
    y_pad = _experts(block_expert, tok_pad.reshape(nb, 1, MOE_BLOCK), x, w1, w3, w2, layer)
    pos_blocks = pos.reshape(n // COMBINE_TILE, 1, TOP_K * COMBINE_TILE)
    return _combine_ln(pos_blocks, y_pad, x, gate_t.T, g, b)


def kernel(x_prompt, x_sample, cache_fox_k, cache_fox_v, cache_fox_logf, state_rwkv, state_rwkv_shift, w_in, rwkv_mu, rwkv_w0, rwkv_w2, rwkv_a0, rwkv_a2, rwkv_g2, rwkv_k_k, rwkv_k_a, rwkv_r_k, rwkv_lnx_w, rwkv_lnx_b, fox_b_f, fox_q_g, fox_k_g, w_out, ln1_g, ln1_b, ln2_g, ln2_b, router_w, router_b, moe_w1, moe_w3, moe_w2):
    nb_p, seq, _ = x_prompt.shape
    nb_s, dec, _ = x_sample.shape
    depth = w_in.shape[0]
    past = cache_fox_k.shape[2]
    n_p, n_s = nb_p * seq, nb_s * dec
    n = n_p + n_s

    x = jnp.concatenate([x_prompt.reshape(n_p, D_MODEL), x_sample.reshape(n_s, D_MODEL)], axis=0)
    logf_rows = cache_fox_logf.transpose(0, 1, 3, 2)
    cache_k = cache_fox_k.reshape(depth, nb_s, past * N_HEADS, HEAD_DIM)
    cache_v = cache_fox_v.reshape(depth, nb_s, past * N_HEADS, HEAD_DIM)
    fox0 = RWKV_COLS
    fl0 = fox0 + 3 * D_F
    w_in_b = jnp.concatenate(
        [w_in[:, :, :fl0], w_in[:, :, fl0 + N_HEADS:], w_in[:, :, fl0:fl0 + N_HEADS],
         jnp.zeros((depth, D_MODEL, FL_PAD - N_HEADS), F32)], axis=-1).astype(BF16)
    w_out_b = w_out.astype(BF16)
    rw_t = router_w.T
    rb_col = router_b.reshape(N_EXPERTS, 1)
    head_of = jnp.arange(D_F, dtype=I32) // HEAD_DIM
    ones_bd = (head_of[:, None] == head_of[None, :]).astype(BF16)
    slot_consts = _slot_constants()
    zero_shift = jnp.zeros((nb_p, 1, RWKV_COLS), F32)
    zero_state = jnp.zeros((nb_p, N_HEADS, HEAD_DIM, HEAD_DIM), F32)
    row = lambda v: v.reshape(1, -1)

    outs = {k: [] for k in ('pk', 'pv', 'pl', 'pr', 'ps', 'sk', 'sv', 'sl', 'sr', 'ss')}
    for l in range(depth):
        lp = dict(mu=row(rwkv_mu[l]), w0=row(rwkv_w0[l]), w2=rwkv_w2[l], a0=row(rwkv_a0[l]), a2=rwkv_a2[l],
                  g2=rwkv_g2[l], k_k=row(rwkv_k_k[l]), k_a=row(rwkv_k_a[l]), r_k=row(rwkv_r_k[l]),
                  lnx_w=row(rwkv_lnx_w[l]), lnx_b=row(rwkv_lnx_b[l]), ones_bd=ones_bd)
        pr, q, k, v, og, fl = _in_proj(x, w_in_b[l])

        ry_p, st_p = _rwkv(pr, zero_shift, zero_state, lp, nb_p, seq, 0)
        ry_s, st_s = _rwkv(pr, state_rwkv_shift[l], state_rwkv[l], lp, nb_s, dec, n_p)

        b_f = jnp.concatenate([fox_b_f[l], jnp.zeros((FL_PAD - N_HEADS,), F32)]).reshape(1, FL_PAD)
        lf_p, c_p = _logf_cumsum(fl, b_f, nb_p, seq, 0)
        lf_s, c_s = _logf_cumsum(fl, b_f, nb_s, dec, n_p)
        qa, kn, ka, vb = _fox_prep(q, k, v, jnp.concatenate([c_p, c_s], axis=0), row(jnp.tile(fox_q_g[l], N_HEADS)),
                                   row(jnp.tile(fox_k_g[l], N_HEADS)), ones_bd, slot_consts)
        fo_p = _attn_prompt(qa, ka, vb, nb_p, seq)
        fo_s = _attn_sample(qa, ka, vb, cache_k, cache_v, logf_rows, l, nb_s, dec, n_p)

        x1 = _out_proj_ln(ry_p, ry_s, fo_p, fo_s, og, x, w_out_b[l], row(ln1_g[l]), row(ln1_b[l]))
        x = _grouped_moe_ln(x1, rw_t, rb_col, moe_w1, moe_w3, moe_w2, l, row(ln2_g[l]), row(ln2_b[l]))

        outs['pk'].append(kn[:n_p].reshape(nb_p, seq, N_HEADS, HEAD_DIM))
        outs['pv'].append(v[:n_p].reshape(nb_p, seq, N_HEADS, HEAD_DIM))
        outs['pl'].append(lf_p[:, :N_HEADS].reshape(nb_p, seq, N_HEADS))
        outs['pr'].append(st_p)
        outs['ps'].append(pr[:n_p].reshape(nb_p, seq, RWKV_COLS)[:, seq - 1:])
        outs['sk'].append(kn[n_p:].reshape(nb_s, dec, N_HEADS, HEAD_DIM))
        outs['sv'].append(v[n_p:].reshape(nb_s, dec, N_HEADS, HEAD_DIM))
        outs['sl'].append(lf_s[:, :N_HEADS].reshape(nb_s, dec, N_HEADS))
        outs['sr'].append(st_s)
        outs['ss'].append(pr[n_p:].reshape(nb_s, dec, RWKV_COLS)[:, dec - 1:])

    stk = lambda key: jnp.stack(outs[key], axis=0)
    return (x[:n_p].reshape(nb_p, seq, D_MODEL), x[n_p:].reshape(nb_s, dec, D_MODEL),
            stk('pk'), stk('pv'), stk('pl'), stk('pr'), stk('ps'),
            stk('sk'), stk('sv'), stk('sl'), stk('sr'), stk('ss'))
```

```python
import functools

import jax
import jax.numpy as jnp
from jax import lax
from jax.experimental import pallas as pl
from jax.experimental.pallas import tpu as pltpu

F32 = jnp.float32
BF16 = jnp.bfloat16
I32 = jnp.int32

D_MODEL = 1024
HEAD_DIM = 64
N_HEADS = 8
D_R = N_HEADS * HEAD_DIM
D_F = N_HEADS * HEAD_DIM
DECAY_LORA = 64
A_LORA = 64
G_LORA = 128
RWKV_COLS = 3 * D_R + DECAY_LORA + A_LORA + G_LORA
FL_PAD = 128
IN_COLS_PAD = RWKV_COLS + 4 * D_F + FL_PAD
DEPTH = 2
N_EXPERTS = 32
N_GROUPS = 4
EXPERTS_PER_GROUP = N_EXPERTS // N_GROUPS
TOP_K = 2
D_EXPERT = D_MODEL // 2
MOE_BLOCK = 256
ALPHA = (2 * DEPTH) ** 0.25
LN_EPS = 1e-5
GN_EPS = 64e-5
QK_EPS = 1e-6
SCALE = HEAD_DIM ** -0.5
RWKV_CHUNK = 64
INV_BASE = 16
VMEM_LIMIT = 48 * 1024 * 1024

_NN = (((1,), (0,)), ((), ()))
_NT = (((1,), (1,)), ((), ()))
_TN = (((0,), (0,)), ((), ()))


def _dot(a, b, dims=_NN):
    return lax.dot_general(a, b, dims, preferred_element_type=F32)


def _split2(x):
    hi = x.astype(BF16)
    lo = (x - hi.astype(F32)).astype(BF16)
    return hi, lo


def _split3(x):
    hi = x.astype(BF16)
    r = x - hi.astype(F32)
    mid = r.astype(BF16)
    lo = (r - mid.astype(F32)).astype(BF16)
    return hi, mid, lo


def _dot3(a, b, dims=_NN):
    ah, al = _split2(a)
    bh, bl = _split2(b)
    return _dot(ah, bh, dims) + (_dot(ah, bl, dims) + _dot(al, bh, dims))


def _dot_exact_lhs(a_bf16, x, dims=_NN):
    hi, mid, lo = _split3(x)
    return _dot(a_bf16, hi, dims) + (_dot(a_bf16, mid, dims) + _dot(a_bf16, lo, dims))


def _sigmoid(x):
    return 1.0 / (1.0 + jnp.exp(-x))


def _softplus(x):
    return jnp.maximum(x, 0.0) + jnp.log(1.0 + jnp.exp(-jnp.abs(x)))


def _layer_norm(z, g, b):
    mu = jnp.mean(z, axis=-1, keepdims=True)
    zc = z - mu
    var = jnp.mean(zc * zc, axis=-1, keepdims=True)
    return zc * lax.rsqrt(var + LN_EPS) * g + b


def _params(sem):
    return pltpu.CompilerParams(dimension_semantics=sem, vmem_limit_bytes=VMEM_LIMIT)


_IN_SPLITS = (RWKV_COLS, D_F, D_F, D_F, D_F, FL_PAD)


def _in_proj_body(x_ref, w_ref, *out_refs):
    x = x_ref[...].astype(BF16)
    col = 0
    for ref, width in zip(out_refs, _IN_SPLITS):
        for c0 in range(0, width, 512):
            c1 = min(c0 + 512, width)
            ref[:, c0:c1] = _dot(x, w_ref[:, col + c0:col + c1])
        col += width


def _in_proj(x, w):
    n = x.shape[0]
    tm = 256
    return pl.pallas_call(
        _in_proj_body,
        grid=(n // tm,),
        in_specs=[pl.BlockSpec((tm, D_MODEL), lambda i: (i, 0)),
                  pl.BlockSpec((D_MODEL, IN_COLS_PAD), lambda i: (0, 0))],
        out_specs=[pl.BlockSpec((tm, wd), lambda i: (i, 0)) for wd in _IN_SPLITS],
        out_shape=[jax.ShapeDtypeStruct((n, wd), F32) for wd in _IN_SPLITS],
        compiler_params=_params(("parallel",)),
        name="in_proj",
    )(x, w)


def _unit_lower_inverses(ls, row, col):
    c = ls[0].shape[0]
    shift = INV_BASE.bit_length() - 1
    same = (row >> shift) == (col >> shift)
    eye = jnp.where(row == col, 1.0, 0.0)
    p = [jnp.where(same, -l, 0.0) for l in ls]
    x = [eye + n for n in p]
    for _ in range(shift - 1):
        p = [_dot3(pi, pi) for pi in p]
        x = [xi + _dot3(xi, pi) for xi, pi in zip(x, p)]
    size = 2 * INV_BASE
    while size <= c:
        s_hi = size.bit_length() - 1
        off = ((row >> s_hi) == (col >> s_hi)) & ((row >> (s_hi - 1)) != (col >> (s_hi - 1)))
        xq = [_dot3(xi, jnp.where(off, l, 0.0)) for xi, l in zip(x, ls)]
        x = [xi - _dot3(xqi, xi) for xi, xqi in zip(x, xq)]
        size *= 2
    return x


def _rwkv_body(c_len, n_chunks, pr_ref, sp_ref, s0_ref, mu_ref, w0_ref, w2_ref, a0_ref, a2_ref, g2_ref,
               kk_ref, ka_ref, rk_ref, lnw_ref, lnb_ref, ones_ref, out_ref, sout_ref, carry_ref, s_ref):
    ci = pl.program_id(1)

    @pl.when(ci == 0)
    def _():
        carry_ref[...] = sp_ref[0]
        s_ref[...] = s0_ref[0]

    pr = pr_ref[...]
    trow = lax.broadcasted_iota(I32, (c_len, 1), 0)
    prev = jnp.where(trow == 0, carry_ref[...], pltpu.roll(pr, 1, 0))
    carry_ref[...] = pr[c_len - 1:c_len, :]
    xs = pr + (prev - pr) * mu_ref[...]
    xr = xs[:, 0:D_R]
    xk = xs[:, D_R:2 * D_R]
    xv = xs[:, 2 * D_R:3 * D_R]
    o = 3 * D_R
    xw = xs[:, o:o + DECAY_LORA]
    xa = xs[:, o + DECAY_LORA:o + DECAY_LORA + A_LORA]
    xg = xs[:, o + DECAY_LORA + A_LORA:RWKV_COLS]

    z = w0_ref[...] + _dot3(jnp.tanh(xw), w2_ref[...])
    lw = -jnp.exp(-_softplus(-z) - 0.5)
    a = _sigmoid(a0_ref[...] + _dot3(xa, a2_ref[...]))
    g = _dot3(_sigmoid(xg), g2_ref[...])
    kk_raw = xk * kk_ref[...]
    k_mod = xk * (1.0 + (a - 1.0) * ka_ref[...])

    row = lax.broadcasted_iota(I32, (c_len, c_len), 0)
    col = lax.broadcasted_iota(I32, (c_len, c_len), 1)
    strict = row > col
    incl = row >= col
    cl = _dot_exact_lhs(jnp.where(incl, 1.0, 0.0).astype(BF16), lw)

    ones = ones_ref[...]

    def head_sum(x):
        hi, lo = _split2(x)
        return _dot(hi, ones) + _dot(lo, ones)

    kk = kk_raw / jnp.maximum(jnp.sqrt(head_sum(kk_raw * kk_raw)), 1e-12)
    eg = jnp.exp(cl)
    e_inv = jnp.exp(-cl)
    r_dec = xr * eg
    kk_dec = kk * jnp.exp(cl - lw)
    b_und = kk * a * e_inv
    k_und = k_mod * e_inv

    hs = [slice(h * HEAD_DIM, (h + 1) * HEAD_DIM) for h in range(N_HEADS)]
    lhs = [jnp.concatenate([kk_dec[:, s], r_dec[:, s]], axis=0) for s in hs]
    bu = [b_und[:, s] for s in hs]
    ku = [k_und[:, s] for s in hs]
    vh = [xv[:, s] for s in hs]
    gb = [_dot3(lhs[h], bu[h], _NT) for h in range(N_HEADS)]
    gk = [_dot3(lhs[h], ku[h], _NT) for h in range(N_HEADS)]
    l_b = [jnp.where(strict, m[:c_len], 0.0) for m in gb]
    m_rb = [jnp.where(incl, m[c_len:], 0.0) for m in gb]
    l_k = [jnp.where(strict, m[:c_len], 0.0) for m in gk]
    m_rk = [jnp.where(incl, m[c_len:], 0.0) for m in gk]
    t_inv = _unit_lower_inverses(l_b, row, col)
    lkv = [_dot3(l_k[h], vh[h]) for h in range(N_HEADS)]
    mkv = [_dot3(m_rk[h], vh[h]) for h in range(N_HEADS)]
    vk = [_dot3(vh[h], ku[h], _TN) for h in range(N_HEADS)]
    s0 = [s_ref[h] for h in range(N_HEADS)]
    ps = [_dot3(lhs[h], s0[h], _NT) for h in range(N_HEADS)]
    u = [-_dot3(t_inv[h], ps[h][:c_len] + lkv[h]) for h in range(N_HEADS)]
    y = [ps[h][c_len:] + mkv[h] + _dot3(m_rb[h], u[h]) for h in range(N_HEADS)]
    for h in range(N_HEADS):
        s_ref[h] = (s0[h] + vk[h] + _dot3(u[h], bu[h], _TN)) * eg[c_len - 1:c_len, hs[h]]

    y = jnp.concatenate(y, axis=-1)
    yc = y - head_sum(y) * (1.0 / HEAD_DIM)
    var = head_sum(yc * yc) * (1.0 / HEAD_DIM)
    yn = yc * lax.rsqrt(var + GN_EPS) * lnw_ref[...] + lnb_ref[...]
    bonus = head_sum(xr * k_mod * rk_ref[...]) * xv
    out_ref[...] = (yn + bonus) * g

    @pl.when(ci == n_chunks - 1)
    def _():
        sout_ref[0] = s_ref[...]


def _rwkv(pr, shift_prev, s0, lp, n_seq, seq_len, row0):
    c_len = min(RWKV_CHUNK, seq_len)
    n_chunks = seq_len // c_len
    blk0 = row0 // c_len
    vec = lambda wd: pl.BlockSpec((1, wd), lambda b, c: (0, 0))
    mat = lambda r, wd: pl.BlockSpec((r, wd), lambda b, c: (0, 0))
    in_specs = [
        pl.BlockSpec((c_len, RWKV_COLS), lambda b, c: (blk0 + b * n_chunks + c, 0)),
        pl.BlockSpec((1, 1, RWKV_COLS), lambda b, c: (b, 0, 0)),
        pl.BlockSpec((1, N_HEADS, HEAD_DIM, HEAD_DIM), lambda b, c: (b, 0, 0, 0)),
        vec(RWKV_COLS), vec(D_R), mat(DECAY_LORA, D_R), vec(D_R), mat(A_LORA, D_R), mat(G_LORA, D_R),
        vec(D_R), vec(D_R), vec(D_R), vec(D_R), vec(D_R), mat(D_R, D_R),
    ]
    args = [pr, shift_prev, s0, lp['mu'], lp['w0'], lp['w2'], lp['a0'], lp['a2'], lp['g2'],
            lp['k_k'], lp['k_a'], lp['r_k'], lp['lnx_w'], lp['lnx_b'], lp['ones_bd']]
    return pl.pallas_call(
        functools.partial(_rwkv_body, c_len, n_chunks),
        grid=(n_seq, n_chunks),
        in_specs=in_specs,
        out_specs=[pl.BlockSpec((c_len, D_R), lambda b, c: (b * n_chunks + c, 0)),
                   pl.BlockSpec((1, N_HEADS, HEAD_DIM, HEAD_DIM), lambda b, c: (b, 0, 0, 0))],
        out_shape=[jax.ShapeDtypeStruct((n_seq * seq_len, D_R), F32),
                   jax.ShapeDtypeStruct((n_seq, N_HEADS, HEAD_DIM, HEAD_DIM), F32)],
        scratch_shapes=[pltpu.VMEM((1, RWKV_COLS), F32), pltpu.VMEM((N_HEADS, HEAD_DIM, HEAD_DIM), F32)],
        compiler_params=_params(("arbitrary", "arbitrary")),
        name="rwkv_mixer",
    )(*args)


SLOT = 2 * HEAD_DIM
C_LANE = HEAD_DIM


def _fox_prep_body(q_ref, k_ref, v_ref, c_ref, qg_ref, kg_ref, ones_ref, place_ref, pcq_ref, pck_ref, oneq_ref,
                   onek_ref, qa_ref, kn_ref, ka_ref, vb_ref):
    ones = ones_ref[...]

    def rms(x, gain):
        hi, lo = _split2(x * x)
        ss = _dot(hi, ones) + _dot(lo, ones)
        return x * lax.rsqrt(ss * (1.0 / HEAD_DIM) + QK_EPS) * gain

    qn = rms(q_ref[...], qg_ref[...]) * SCALE
    kn = rms(k_ref[...], kg_ref[...])
    kn_ref[...] = kn
    vb_ref[...] = v_ref[...].astype(BF16)
    c_parts = _split3(c_ref[...])

    def slots(xb, pc_ref, one_ref):
        acc = _dot(xb, place_ref[...]) + one_ref[...]
        for j in range(3):
            acc = acc + _dot(c_parts[j], pc_ref[j])
        return acc.astype(BF16)

    qa_ref[...] = slots(qn.astype(BF16), pcq_ref, oneq_ref)
    ka_ref[...] = slots(kn.astype(BF16), pck_ref, onek_ref)


def _slot_constants():
    d = jnp.arange(D_F, dtype=I32)
    lane = jnp.arange(N_HEADS * SLOT, dtype=I32)
    place = (lane[None, :] == (d // HEAD_DIM * SLOT + d % HEAD_DIM)[:, None]).astype(BF16)
    h = jnp.arange(FL_PAD, dtype=I32)[None, :, None]
    j = jnp.arange(3, dtype=I32)[:, None, None]
    is_head = h < N_HEADS
    pcq = ((lane[None, None, :] == h * SLOT + C_LANE + j) & is_head).astype(BF16)
    pck = -((lane[None, None, :] == h * SLOT + C_LANE + 3 + j) & is_head).astype(BF16)
    in_slot = lane % SLOT
    oneq = ((in_slot >= C_LANE + 3) & (in_slot < C_LANE + 6)).astype(F32).reshape(1, -1)
    onek = ((in_slot >= C_LANE) & (in_slot < C_LANE + 3)).astype(F32).reshape(1, -1)
    return place, pcq, pck, oneq, onek


def _fox_prep(q, k, v, c, q_gain, k_gain, ones_bd, slot_consts):
    n = q.shape[0]
    tm = 256
    wide = N_HEADS * SLOT
    row = lambda wd: pl.BlockSpec((tm, wd), lambda i: (i, 0))
    vec = lambda wd: pl.BlockSpec((1, wd), lambda i: (0, 0))
    full = lambda *shape: pl.BlockSpec(shape, lambda i: (0,) * len(shape))
    return pl.pallas_call(
        _fox_prep_body,
        grid=(n // tm,),
        in_specs=[row(D_F), row(D_F), row(D_F), row(FL_PAD), vec(D_F), vec(D_F), full(D_F, D_F),
                  full(D_F, wide), full(3, FL_PAD, wide), full(3, FL_PAD, wide), vec(wide), vec(wide)],
        out_specs=[row(wide), row(D_F), row(wide), row(D_F)],
        out_shape=[jax.ShapeDtypeStruct((n, wide), BF16), jax.ShapeDtypeStruct((n, D_F), F32),
                   jax.ShapeDtypeStruct((n, wide), BF16), jax.ShapeDtypeStruct((n, D_F), BF16)],
        compiler_params=_params(("parallel",)),
        name="fox_prep",
    )(q, k, v, c, q_gain, k_gain, ones_bd, *slot_consts)


def _logf_cumsum_body(fl_ref, bf_ref, lf_ref, c_ref, carry_ref):
    @pl.when(pl.program_id(1) == 0)
    def _():
        carry_ref[...] = jnp.zeros_like(carry_ref)

    lf = -_softplus(-(fl_ref[...] + bf_ref[...]))
    lf_ref[...] = lf
    t = lf.shape[0]
    row = lax.broadcasted_iota(I32, (t, t), 0)
    col = lax.broadcasted_iota(I32, (t, t), 1)
    cs = _dot_exact_lhs(jnp.where(row >= col, 1.0, 0.0).astype(BF16), lf) + carry_ref[...]
    c_ref[...] = cs
    carry_ref[...] = cs[t - 1:t, :]


def _logf_cumsum(fl, b_f, n_seq, seq_len, row0):
    tc = min(seq_len, 256)
    nt = seq_len // tc
    blk0 = row0 // tc
    out = pl.BlockSpec((tc, FL_PAD), lambda b, j: (b * nt + j, 0))
    return pl.pallas_call(
        _logf_cumsum_body,
        grid=(n_seq, nt),
        in_specs=[pl.BlockSpec((tc, FL_PAD), lambda b, j: (blk0 + b * nt + j, 0)),
                  pl.BlockSpec((1, FL_PAD), lambda b, j: (0, 0))],
        out_specs=[out, out],
        out_shape=[jax.ShapeDtypeStruct((n_seq * seq_len, FL_PAD), F32)] * 2,
        scratch_shapes=[pltpu.VMEM((1, FL_PAD), F32)],
        compiler_params=_params(("arbitrary", "arbitrary")),
        name="logf_cumsum",
    )(fl, b_f)


ATT_TILE = 512


def _attn_prompt_body(q_ref, k_ref, v_ref, o_ref):
    i = pl.program_id(2)
    t = ATT_TILE
    row = lax.broadcasted_iota(I32, (t, t), 0)
    col = lax.broadcasted_iota(I32, (t, t), 1)
    causal = row >= col
    pair = range(2)
    q = [q_ref[:, hh * SLOT:(hh + 1) * SLOT] for hh in pair]

    def tile(j, carry, masked):
        m, l, acc = carry
        j0 = pl.multiple_of(j * t, t)
        s = [_dot(q[hh], k_ref[pl.ds(j0, t), hh * SLOT:(hh + 1) * SLOT], _NT) for hh in pair]
        if masked:
            s = [jnp.where(causal, sh, -jnp.inf) for sh in s]
        m_new = [jnp.maximum(m[hh], jnp.max(s[hh], axis=-1, keepdims=True)) for hh in pair]
        alpha = [jnp.exp(m[hh] - m_new[hh]) for hh in pair]
        p = [jnp.exp(s[hh] - m_new[hh]) for hh in pair]
        l = [alpha[hh] * l[hh] + jnp.sum(p[hh], axis=-1, keepdims=True) for hh in pair]
        pv = [_dot(p[hh].astype(BF16), v_ref[pl.ds(j0, t), hh * HEAD_DIM:(hh + 1) * HEAD_DIM]) for hh in pair]
        acc = [alpha[hh] * acc[hh] + pv[hh] for hh in pair]
        return m_new, l, acc

    init = ([jnp.full((t, 1), -jnp.inf, F32)] * 2, [jnp.zeros((t, 1), F32)] * 2,
            [jnp.zeros((t, HEAD_DIM), F32)] * 2)
    carry = lax.fori_loop(0, i, lambda j, c: tile(j, c, False), init)
    _, l, acc = tile(i, carry, True)
    o_ref[...] = jnp.concatenate([acc[hh] / l[hh] for hh in pair], axis=-1)


def _attn_prompt(qa, ka, vb, n_seq, seq_len):
    t = ATT_TILE
    nq = seq_len // t
    return pl.pallas_call(
        _attn_prompt_body,
        grid=(n_seq, N_HEADS // 2, nq),
        in_specs=[pl.BlockSpec((t, 2 * SLOT), lambda b, p, i: (b * nq + i, p)),
                  pl.BlockSpec((seq_len, 2 * SLOT), lambda b, p, i: (b, p)),
                  pl.BlockSpec((seq_len, 2 * HEAD_DIM), lambda b, p, i: (b, p))],
        out_specs=pl.BlockSpec((t, 2 * HEAD_DIM), lambda b, p, i: (b * nq + i, p)),
        out_shape=jax.ShapeDtypeStruct((n_seq * seq_len, D_F), F32),
        compiler_params=_params(("parallel", "parallel", "arbitrary")),
        name="fox_attn_prompt",
    )(qa, ka, vb)


SAMPLE_CHUNK = 2048
TAIL_BLOCK = 1024


def _attn_sample_body(n_chunks, q_ref, kn_ref, vn_ref, kc_ref, vc_ref, lp_ref, after_ref, o_ref,
                      m_ref, l_ref, acc_ref, suffix_ref):
    j = pl.program_id(1)
    n = q_ref.shape[0]
    heads = range(N_HEADS)

    @pl.when(j == 0)
    def _():
        m_ref[...] = jnp.full(m_ref.shape, -jnp.inf, F32)
        l_ref[...] = jnp.zeros_like(l_ref)
        acc_ref[...] = jnp.zeros_like(acc_ref)
        suffix_ref[...] = jnp.zeros_like(suffix_ref)

    after = after_ref[...]
    tp = lp_ref.shape[3]
    tails = []
    suffix = suffix_ref[...]
    for b0 in range(tp - TAIL_BLOCK, -1, -TAIL_BLOCK):
        lp = lp_ref[0, 0, :, b0:b0 + TAIL_BLOCK]
        hi, mid, lo = _split3(lp)
        tails.insert(0, _dot(hi, after) + (_dot(mid, after) + _dot(lo, after)) + suffix)
        suffix = suffix + jnp.sum(lp, axis=-1, keepdims=True)
    suffix_ref[...] = suffix
    tail = jnp.concatenate(tails, axis=-1)

    q_slot = [q_ref[:, h * SLOT:(h + 1) * SLOT] for h in heads]
    c_col = [(qs[:, C_LANE:C_LANE + 1].astype(F32) + qs[:, C_LANE + 1:C_LANE + 2].astype(F32)
              + qs[:, C_LANE + 2:C_LANE + 3].astype(F32)) for qs in q_slot]
    s = [_dot(q_slot[h][:, :HEAD_DIM], kc_ref[0, 0, h].astype(BF16)) + c_col[h] + tail[h:h + 1, :] for h in heads]
    m_old = [m_ref[h] for h in heads]
    m_new = [jnp.maximum(m_old[h], jnp.max(s[h], axis=-1, keepdims=True)) for h in heads]
    alpha = [jnp.exp(m_old[h] - m_new[h]) for h in heads]
    p = [jnp.exp(s[h] - m_new[h]) for h in heads]
    pv = [_dot(p[h].astype(BF16), vc_ref[0, 0, h].astype(BF16), _NT) for h in heads]
    for h in heads:
        m_ref[h] = m_new[h]
        l_ref[h] = alpha[h] * l_ref[h] + jnp.sum(p[h], axis=-1, keepdims=True)
        acc_ref[h] = alpha[h] * acc_ref[h] + pv[h]

    @pl.when(j == n_chunks - 1)
    def _():
        row = lax.broadcasted_iota(I32, (n, n), 0)
        col = lax.broadcasted_iota(I32, (n, n), 1)
        s_new = [jnp.where(row >= col, _dot(q_slot[h], kn_ref[:, h * SLOT:(h + 1) * SLOT], _NT), -jnp.inf)
                 for h in heads]
        m_fin = [jnp.maximum(m_ref[h], jnp.max(s_new[h], axis=-1, keepdims=True)) for h in heads]
        a_fin = [jnp.exp(m_ref[h] - m_fin[h]) for h in heads]
        p_new = [jnp.exp(s_new[h] - m_fin[h]) for h in heads]
        l_fin = [a_fin[h] * l_ref[h] + jnp.sum(p_new[h], axis=-1, keepdims=True) for h in heads]
        acc = [a_fin[h] * acc_ref[h] + _dot(p_new[h].astype(BF16), vn_ref[:, h * HEAD_DIM:(h + 1) * HEAD_DIM])
               for h in heads]
        o_ref[...] = jnp.concatenate([acc[h] / l_fin[h] for h in heads], axis=-1)


def _attn_sample(qa, ka, vb, cache_k, cache_v, logf_rows, layer, n_seq, n_new, row0):
    past = cache_k.shape[4]
    tp = min(SAMPLE_CHUNK, past)
    n_chunks = past // tp
    blk0 = row0 // n_new
    frame = jnp.arange(TAIL_BLOCK, dtype=I32)
    after = (frame[:, None] > frame[None, :]).astype(BF16)
    rows = lambda wd: pl.BlockSpec((n_new, wd), lambda b, j: (blk0 + b, 0))
    cache = lambda: pl.BlockSpec((1, 1, N_HEADS, HEAD_DIM, tp), lambda b, j: (layer, b, 0, 0, n_chunks - 1 - j))
    return pl.pallas_call(
        functools.partial(_attn_sample_body, n_chunks),
        grid=(n_seq, n_chunks),
        in_specs=[rows(N_HEADS * SLOT), rows(N_HEADS * SLOT), rows(D_F), cache(), cache(),
                  pl.BlockSpec((1, 1, N_HEADS, tp), lambda b, j: (layer, b, 0, n_chunks - 1 - j)),
                  pl.BlockSpec((TAIL_BLOCK, TAIL_BLOCK), lambda b, j: (0, 0))],
        out_specs=pl.BlockSpec((n_new, D_F), lambda b, j: (b, 0)),
        out_shape=jax.ShapeDtypeStruct((n_seq * n_new, D_F), F32),
        scratch_shapes=[pltpu.VMEM((N_HEADS, n_new, 1), F32), pltpu.VMEM((N_HEADS, n_new, 1), F32),
                        pltpu.VMEM((N_HEADS, n_new, HEAD_DIM), F32), pltpu.VMEM((N_HEADS, 1), F32)],
        compiler_params=_params(("parallel", "arbitrary")),
        name="fox_attn_sample",
    )(qa, ka, vb, cache_k, cache_v, logf_rows, after)


def _out_proj_body(tiles_p, ryp_ref, rys_ref, fop_ref, fos_ref, og_ref, x_ref, w_ref, g_ref, b_ref, o_ref):
    from_prompt = pl.program_id(0) < tiles_p
    ry = jnp.where(from_prompt, ryp_ref[...], rys_ref[...]).astype(BF16)
    fo = jnp.where(from_prompt, fop_ref[...], fos_ref[...])
    fy = (fo * _sigmoid(og_ref[...])).astype(BF16)
    m = _dot(ry, w_ref[0:D_R, :]) + _dot(fy, w_ref[D_R:D_R + D_F, :])
    o_ref[...] = _layer_norm(ALPHA * x_ref[...] + m, g_ref[...], b_ref[...])


def _out_proj_ln(ry_p, ry_s, fo_p, fo_s, og, x, w, g, b):
    n = x.shape[0]
    tm = 256
    tiles_p = ry_p.shape[0] // tm
    row = lambda wd: pl.BlockSpec((tm, wd), lambda i: (i, 0))
    row_p = lambda wd: pl.BlockSpec((tm, wd), lambda i: (jnp.minimum(i, tiles_p - 1), 0))
    row_s = lambda wd: pl.BlockSpec((tm, wd), lambda i: (jnp.maximum(i - tiles_p, 0), 0))
    vec = pl.BlockSpec((1, D_MODEL), lambda i: (0, 0))
    return pl.pallas_call(
        functools.partial(_out_proj_body, tiles_p),
        grid=(n // tm,),
        in_specs=[row_p(D_R), row_s(D_R), row_p(D_F), row_s(D_F), row(D_F), row(D_MODEL),
                  pl.BlockSpec((D_R + D_F, D_MODEL), lambda i: (0, 0)), vec, vec],
        out_specs=row(D_MODEL),
        out_shape=jax.ShapeDtypeStruct((n, D_MODEL), F32),
        compiler_params=_params(("parallel",)),
        name="out_proj_ln",
    )(ry_p, ry_s, fo_p, fo_s, og, x, w, g, b)


def _router_body(x_ref, rw_ref, rb_ref, earlier_ref, e_ref, g_ref, r_ref, count_ref, seen_ref):
    tn = x_ref.shape[0]
    scores = _sigmoid(_dot3(rw_ref[...], x_ref[...], _NT))
    sel = scores + rb_ref[...]
    sel4 = sel.reshape(N_GROUPS, EXPERTS_PER_GROUP, tn)
    sc4 = scores.reshape(N_GROUPS, EXPERTS_PER_GROUP, tn)
    lane_e = lax.broadcasted_iota(I32, (N_GROUPS, EXPERTS_PER_GROUP, tn), 1)

    def top2(vals, idx_iota, axis):
        m1 = jnp.max(vals, axis=axis, keepdims=True)
        i1 = jnp.min(jnp.where(vals == m1, idx_iota, EXPERTS_PER_GROUP), axis=axis, keepdims=True)
        rest = jnp.where(idx_iota == i1, -jnp.inf, vals)
        m2 = jnp.max(rest, axis=axis, keepdims=True)
        i2 = jnp.min(jnp.where(rest == m2, idx_iota, EXPERTS_PER_GROUP), axis=axis, keepdims=True)
        return m1, i1, m2, i2

    m1, _, m2, _ = top2(sel4, lane_e, 1)
    gsum = m1 + m2
    g_iota = lax.broadcasted_iota(I32, (N_GROUPS, 1, tn), 0)
    gmax = jnp.max(gsum, axis=0, keepdims=True)
    g_idx = jnp.min(jnp.where(gsum == gmax, g_iota, N_GROUPS), axis=0, keepdims=True)
    pick = g_iota == g_idx
    sel_g = jnp.max(jnp.where(pick, sel4, -jnp.inf), axis=0)
    sc_g = jnp.max(jnp.where(pick, sc4, -jnp.inf), axis=0)
    e_iota = lax.broadcasted_iota(I32, (EXPERTS_PER_GROUP, tn), 0)
    _, i1, _, i2 = top2(sel_g, e_iota, 0)
    gate1 = jnp.sum(jnp.where(e_iota == i1, sc_g, 0.0), axis=0, keepdims=True)
    gate2 = jnp.sum(jnp.where(e_iota == i2, sc_g, 0.0), axis=0, keepdims=True)
    tot = gate1 + gate2
    base = g_idx[0] * EXPERTS_PER_GROUP
    e1 = base + i1
    e2 = base + i2
    e_ref[...] = jnp.concatenate([e1, e2], axis=0)
    g_ref[...] = jnp.concatenate([gate1 / tot, gate2 / tot], axis=0)

    @pl.when(pl.program_id(0) == 0)
    def _():
        seen_ref[...] = jnp.zeros_like(seen_ref)

    all_e = lax.broadcasted_iota(I32, (N_EXPERTS, tn), 0)
    hit1 = all_e == e1
    hit2 = all_e == e2
    hits = jnp.where(hit1 | hit2, 1.0, 0.0)
    before = _dot(hits.astype(BF16), earlier_ref[...]) + seen_ref[...]
    r1 = jnp.sum(jnp.where(hit1, before, 0.0), axis=0, keepdims=True)
    r2 = jnp.sum(jnp.where(hit2, before, 0.0), axis=0, keepdims=True)
    r_ref[...] = jnp.concatenate([r1, r2], axis=0).astype(I32)
    seen_ref[...] = seen_ref[...] + jnp.sum(hits, axis=-1, keepdims=True)
    count_ref[...] = seen_ref[...].astype(I32)


def _router(x, rw_t, rb_col):
    n = x.shape[0]
    tn = 512
    tok = jnp.arange(tn, dtype=I32)
    earlier = (tok[:, None] < tok[None, :]).astype(BF16)
    pair = lambda: pl.BlockSpec((TOP_K, tn), lambda i: (0, i))
    return pl.pallas_call(
        _router_body,
        grid=(n // tn,),
        in_specs=[pl.BlockSpec((tn, D_MODEL), lambda i: (i, 0)),
                  pl.BlockSpec((N_EXPERTS, D_MODEL), lambda i: (0, 0)),
                  pl.BlockSpec((N_EXPERTS, 1), lambda i: (0, 0)),
                  pl.BlockSpec((tn, tn), lambda i: (0, 0))],
        out_specs=[pair(), pair(), pair(), pl.BlockSpec((N_EXPERTS, 1), lambda i: (0, 0))],
        out_shape=[jax.ShapeDtypeStruct((TOP_K, n), I32), jax.ShapeDtypeStruct((TOP_K, n), F32),
                   jax.ShapeDtypeStruct((TOP_K, n), I32), jax.ShapeDtypeStruct((N_EXPERTS, 1), I32)],
        scratch_shapes=[pltpu.VMEM((N_EXPERTS, 1), F32)],
        compiler_params=_params(("arbitrary",)),
        name="router",
    )(x, rw_t, rb_col, earlier)


def _slot_rows_body(e_ref, r_ref, start_ref, pos_ref):
    tn = e_ref.shape[1]
    all_e = lax.broadcasted_iota(I32, (N_EXPERTS, tn), 0)
    rows = [jnp.sum(jnp.where(all_e == e_ref[k:k + 1, :], start_ref[...], 0), axis=0, keepdims=True)
            for k in range(TOP_K)]
    pos_ref[...] = r_ref[...] + jnp.concatenate(rows, axis=0)


def _slot_rows(eidx_t, rank_t, expert_start):
    n = eidx_t.shape[1]
    tn = 512
    pair = lambda: pl.BlockSpec((TOP_K, tn), lambda i: (0, i))
    return pl.pallas_call(
        _slot_rows_body,
        grid=(n // tn,),
        in_specs=[pair(), pair(), pl.BlockSpec((N_EXPERTS, 1), lambda i: (0, 0))],
        out_specs=pair(),
        out_shape=jax.ShapeDtypeStruct((TOP_K, n), I32),
        compiler_params=_params(("parallel",)),
        name="moe_slot_rows",
    )(eidx_t, rank_t, expert_start)


def _row_copy(src_hbm, src_row, dst, dst_row, sem):
    return pltpu.make_async_copy(src_hbm.at[pl.ds(src_row, 1)], dst.at[pl.ds(dst_row, 1)], sem)


def _expert_body(be_ref, used_ref, tok_ref, tok_next_ref, x_hbm, w1_ref, w3_ref, w2_ref, y_ref,
                 buf_ref, w1b_ref, w3b_ref, w2b_ref, sem):
    i = pl.program_id(0)
    n_used = used_ref[0]
    slot = lax.rem(i, 2)

    def start_gather(ids_ref, s):
        def body(r, carry):
            _row_copy(x_hbm, ids_ref[0, 0, r], buf_ref.at[s], r, sem.at[s]).start()
            return carry
        lax.fori_loop(0, MOE_BLOCK, body, 0, unroll=8)

    @pl.when(jnp.logical_and(i == 0, n_used > 0))
    def _():
        start_gather(tok_ref, 0)

    @pl.when(i + 1 < n_used)
    def _():
        start_gather(tok_next_ref, 1 - slot)

    @pl.when(i < n_used)
    def _():
        @pl.when(jnp.logical_or(i == 0, be_ref[i] != be_ref[jnp.maximum(i - 1, 0)]))
        def _():
            w1b_ref[...] = w1_ref[0, 0].astype(BF16)
            w3b_ref[...] = w3_ref[0, 0].astype(BF16)
            w2b_ref[...] = w2_ref[0, 0].astype(BF16)

        def wait(r, carry):
            _row_copy(x_hbm, 0, buf_ref.at[slot], r, sem.at[slot]).wait()
            return carry
        lax.fori_loop(0, MOE_BLOCK, wait, 0, unroll=8)

        h = buf_ref[slot].astype(BF16)
        a = _dot(h, w1b_ref[...])
        b = _dot(h, w3b_ref[...])
        act = (a * _sigmoid(a) * b).astype(BF16)
        y_ref[...] = _dot(act, w2b_ref[...])

    @pl.when(i >= n_used)
    def _():
        y_ref[...] = jnp.zeros_like(y_ref)


def _experts(block_expert, n_used, tok_blocks, x, w1, w3, w2, layer):
    nb = block_expert.shape[0]
    ids = lambda f: pl.BlockSpec((1, 1, MOE_BLOCK), f, memory_space=pltpu.SMEM)
    grid_spec = pltpu.PrefetchScalarGridSpec(
        num_scalar_prefetch=2,
        grid=(nb,),
        in_specs=[ids(lambda i, be, nu: (i, 0, 0)),
                  ids(lambda i, be, nu: (jnp.minimum(i + 1, nb - 1), 0, 0)),
                  pl.BlockSpec(memory_space=pl.ANY),
                  pl.BlockSpec((1, 1, D_MODEL, D_EXPERT), lambda i, be, nu: (layer, be[i], 0, 0)),
                  pl.BlockSpec((1, 1, D_MODEL, D_EXPERT), lambda i, be, nu: (layer, be[i], 0, 0)),
                  pl.BlockSpec((1, 1, D_EXPERT, D_MODEL), lambda i, be, nu: (layer, be[i], 0, 0))],
        out_specs=pl.BlockSpec((MOE_BLOCK, D_MODEL), lambda i, be, nu: (i, 0)),
        scratch_shapes=[pltpu.VMEM((2, MOE_BLOCK, D_MODEL), F32),
                        pltpu.VMEM((D_MODEL, D_EXPERT), BF16), pltpu.VMEM((D_MODEL, D_EXPERT), BF16),
                        pltpu.VMEM((D_EXPERT, D_MODEL), BF16), pltpu.SemaphoreType.DMA((2,))],
    )
    return pl.pallas_call(
        _expert_body,
        grid_spec=grid_spec,
        out_shape=jax.ShapeDtypeStruct((nb * MOE_BLOCK, D_MODEL), F32),
        compiler_params=_params(("arbitrary",)),
        name="moe_experts",
    )(block_expert, n_used, tok_blocks, tok_blocks, x, w1, w3, w2)


COMBINE_TILE = 128


def _combine_body(pos_ref, pos_next_ref, y_hbm, x_ref, gate_ref, g_ref, b_ref, o_ref, buf_ref, sem):
    i = pl.program_id(0)
    slot = lax.rem(i, 2)

    def start_gather(ids_ref, s):
        def body(t, carry):
            for k in range(TOP_K):
                _row_copy(y_hbm, ids_ref[0, 0, TOP_K * t + k], buf_ref.at[s, k], t, sem.at[s]).start()
            return carry
        lax.fori_loop(0, COMBINE_TILE, body, 0, unroll=4)

    @pl.when(i == 0)
    def _():
        start_gather(pos_ref, 0)

    @pl.when(i + 1 < pl.num_programs(0))
    def _():
        start_gather(pos_next_ref, 1 - slot)

    def wait(t, carry):
        for k in range(TOP_K):
            _row_copy(y_hbm, 0, buf_ref.at[slot, k], t, sem.at[slot]).wait()
        return carry
    lax.fori_loop(0, COMBINE_TILE, wait, 0, unroll=4)

    gate = gate_ref[...]
    y = gate[:, 0:1] * buf_ref[slot, 0] + gate[:, 1:2] * buf_ref[slot, 1]
    o_ref[...] = _layer_norm(ALPHA * x_ref[...] + y, g_ref[...], b_ref[...])


def _combine_ln(pos_blocks, y_pad, x, gate, g, b):
    n = x.shape[0]
    tm = COMBINE_TILE
    nt = n // tm
    vec = pl.BlockSpec((1, D_MODEL), lambda i: (0, 0))
    ids = lambda f: pl.BlockSpec((1, 1, TOP_K * tm), f, memory_space=pltpu.SMEM)
    return pl.pallas_call(
        _combine_body,
        grid=(nt,),
        in_specs=[ids(lambda i: (i, 0, 0)), ids(lambda i: (jnp.minimum(i + 1, nt - 1), 0, 0)),
                  pl.BlockSpec(memory_space=pl.ANY),
                  pl.BlockSpec((tm, D_MODEL), lambda i: (i, 0)),
                  pl.BlockSpec((tm, TOP_K), lambda i: (i, 0)), vec, vec],
        out_specs=pl.BlockSpec((tm, D_MODEL), lambda i: (i, 0)),
        out_shape=jax.ShapeDtypeStruct((n, D_MODEL), F32),
        scratch_shapes=[pltpu.VMEM((2, TOP_K, tm, D_MODEL), F32), pltpu.SemaphoreType.DMA((2,))],
        compiler_params=_params(("arbitrary",)),
        name="moe_combine_ln",
    )(pos_blocks, pos_blocks, y_pad, x, gate, g, b)


def _grouped_moe_ln(x, rw_t, rb_col, w1, w3, w2, layer, g, b):
    n = x.shape[0]
    eidx_t, gate_t, rank_t, counts = _router(x, rw_t, rb_col)
    padded = (counts[:, 0] + MOE_BLOCK - 1) // MOE_BLOCK * MOE_BLOCK
    ends = jnp.cumsum(padded)
    nb = -(-n * TOP_K // MOE_BLOCK) + N_EXPERTS
    block_start = jnp.arange(nb, dtype=I32) * MOE_BLOCK
    block_expert = jnp.minimum(jnp.sum(ends[None, :] <= block_start[:, None], axis=1), N_EXPERTS - 1).astype(I32)
    n_used = (ends[N_EXPERTS - 1:] // MOE_BLOCK).astype(I32)
    pos_t = _slot_rows(eidx_t, rank_t, (ends - padded).astype(I32).reshape(N_EXPERTS, 1))
    pos = pos_t.T
    tok = jnp.broadcast_to(jnp.arange(n, dtype=I32)[:, None], (n, TOP_K))
    tok_pad = jnp.zeros((nb * MOE_BLOCK,), I32).at[pos.reshape(-1)].set(tok.reshape(-1), unique_indices=True)
    y_pad = _experts(block_expert, n_used, tok_pad.reshape(nb, 1, MOE_BLOCK), x, w1, w3, w2, layer)
    pos_blocks = pos.reshape(n // COMBINE_TILE, 1, TOP_K * COMBINE_TILE)
    return _combine_ln(pos_blocks, y_pad, x, gate_t.T, g, b)


def kernel(x_prompt, x_sample, cache_fox_k, cache_fox_v, cache_fox_logf, state_rwkv, state_rwkv_shift, w_in, rwkv_mu, rwkv_w0, rwkv_w2, rwkv_a0, rwkv_a2, rwkv_g2, rwkv_k_k, rwkv_k_a, rwkv_r_k, rwkv_lnx_w, rwkv_lnx_b, fox_b_f, fox_q_g, fox_k_g, w_out, ln1_g, ln1_b, ln2_g, ln2_b, router_w, router_b, moe_w1, moe_w3, moe_w2):
    nb_p, seq, _ = x_prompt.shape
    nb_s, dec, _ = x_sample.shape
    depth = w_in.shape[0]
    past = cache_fox_k.shape[2]
    n_p, n_s = nb_p * seq, nb_s * dec
    n = n_p + n_s

    x = jnp.concatenate([x_prompt.reshape(n_p, D_MODEL), x_sample.reshape(n_s, D_MODEL)], axis=0)
    logf_rows = cache_fox_logf.transpose(0, 1, 3, 2)
    cache_k = cache_fox_k.transpose(0, 1, 3, 4, 2)
    cache_v = cache_fox_v.transpose(0, 1, 3, 4, 2)
    fox0 = RWKV_COLS
    fl0 = fox0 + 3 * D_F
    w_in_b = jnp.concatenate(
        [w_in[:, :, :fl0], w_in[:, :, fl0 + N_HEADS:], w_in[:, :, fl0:fl0 + N_HEADS],
         jnp.zeros((depth, D_MODEL, FL_PAD - N_HEADS), F32)], axis=-1).astype(BF16)
    w_out_b = w_out.astype(BF16)
    rw_t = router_w.T
    rb_col = router_b.reshape(N_EXPERTS, 1)
    head_of = jnp.arange(D_F, dtype=I32) // HEAD_DIM
    ones_bd = (head_of[:, None] == head_of[None, :]).astype(BF16)
    slot_consts = _slot_constants()
    zero_shift = jnp.zeros((nb_p, 1, RWKV_COLS), F32)
    zero_state = jnp.zeros((nb_p, N_HEADS, HEAD_DIM, HEAD_DIM), F32)
    row = lambda v: v.reshape(1, -1)

    outs = {k: [] for k in ('pk', 'pv', 'pl', 'pr', 'ps', 'sk', 'sv', 'sl', 'sr', 'ss')}
    for l in range(depth):
        lp = dict(mu=row(rwkv_mu[l]), w0=row(rwkv_w0[l]), w2=rwkv_w2[l], a0=row(rwkv_a0[l]), a2=rwkv_a2[l],
                  g2=rwkv_g2[l], k_k=row(rwkv_k_k[l]), k_a=row(rwkv_k_a[l]), r_k=row(rwkv_r_k[l]),
                  lnx_w=row(rwkv_lnx_w[l]), lnx_b=row(rwkv_lnx_b[l]), ones_bd=ones_bd)
        pr, q, k, v, og, fl = _in_proj(x, w_in_b[l])

        ry_p, st_p = _rwkv(pr, zero_shift, zero_state, lp, nb_p, seq, 0)
        ry_s, st_s = _rwkv(pr, state_rwkv_shift[l], state_rwkv[l], lp, nb_s, dec, n_p)

        b_f = jnp.concatenate([fox_b_f[l], jnp.zeros((FL_PAD - N_HEADS,), F32)]).reshape(1, FL_PAD)
        lf_p, c_p = _logf_cumsum(fl, b_f, nb_p, seq, 0)
        lf_s, c_s = _logf_cumsum(fl, b_f, nb_s, dec, n_p)
        qa, kn, ka, vb = _fox_prep(q, k, v, jnp.concatenate([c_p, c_s], axis=0), row(jnp.tile(fox_q_g[l], N_HEADS)),
                                   row(jnp.tile(fox_k_g[l], N_HEADS)), ones_bd, slot_consts)
        fo_p = _attn_prompt(qa, ka, vb, nb_p, seq)
        fo_s = _attn_sample(qa, ka, vb, cache_k, cache_v, logf_rows, l, nb_s, dec, n_p)

        x1 = _out_proj_ln(ry_p, ry_s, fo_p, fo_s, og, x, w_out_b[l], row(ln1_g[l]), row(ln1_b[l]))
        x = _grouped_moe_ln(x1, rw_t, rb_col, moe_w1, moe_w3, moe_w2, l, row(ln2_g[l]), row(ln2_b[l]))

        outs['pk'].append(kn[:n_p].reshape(nb_p, seq, N_HEADS, HEAD_DIM))
        outs['pv'].append(v[:n_p].reshape(nb_p, seq, N_HEADS, HEAD_DIM))
        outs['pl'].append(lf_p[:, :N_HEADS].reshape(nb_p, seq, N_HEADS))
        outs['pr'].append(st_p)
        outs['ps'].append(pr[:n_p].reshape(nb_p, seq, RWKV_COLS)[:, seq - 1:])
        outs['sk'].append(kn[n_p:].reshape(nb_s, dec, N_HEADS, HEAD_DIM))
        outs['sv'].append(v[n_p:].reshape(nb_s, dec, N_HEADS, HEAD_DIM))
        outs['sl'].append(lf_s[:, :N_HEADS].reshape(nb_s, dec, N_HEADS))
        outs['sr'].append(st_s)
        outs['ss'].append(pr[n_p:].reshape(nb_s, dec, RWKV_COLS)[:, dec - 1:])

    stk = lambda key: jnp.stack(outs[key], axis=0)
    return (x[:n_p].reshape(nb_p, seq, D_MODEL), x[n_p:].reshape(nb_s, dec, D_MODEL),
            stk('pk'), stk('pv'), stk('pl'), stk('pr'), stk('ps'),
            stk('sk'), stk('sv'), stk('sl'), stk('sr'), stk('ss'))
```

```python
import functools

import jax
import jax.numpy as jnp
from jax import lax
from jax.experimental import pallas as pl
from jax.experimental.pallas import tpu as pltpu

F32 = jnp.float32
BF16 = jnp.bfloat16
I32 = jnp.int32

D_MODEL = 1024
HEAD_DIM = 64
N_HEADS = 8
D_R = N_HEADS * HEAD_DIM
D_F = N_HEADS * HEAD_DIM
DECAY_LORA = 64
A_LORA = 64
G_LORA = 128
RWKV_COLS = 3 * D_R + DECAY_LORA + A_LORA + G_LORA
FL_PAD = 128
IN_COLS_PAD = RWKV_COLS + 4 * D_F + FL_PAD
DEPTH = 2
N_EXPERTS = 32
N_GROUPS = 4
EXPERTS_PER_GROUP = N_EXPERTS // N_GROUPS
TOP_K = 2
D_EXPERT = D_MODEL // 2
MOE_BLOCK = 256
ALPHA = (2 * DEPTH) ** 0.25
LN_EPS = 1e-5
GN_EPS = 64e-5
QK_EPS = 1e-6
SCALE = HEAD_DIM ** -0.5
RWKV_CHUNK = 64
INV_BASE = 16
VMEM_LIMIT = 48 * 1024 * 1024

_NN = (((1,), (0,)), ((), ()))
_NT = (((1,), (1,)), ((), ()))
_TN = (((0,), (0,)), ((), ()))


def _dot(a, b, dims=_NN):
    return lax.dot_general(a, b, dims, preferred_element_type=F32)


def _split2(x):
    hi = x.astype(BF16)
    lo = (x - hi.astype(F32)).astype(BF16)
    return hi, lo


def _split3(x):
    hi = x.astype(BF16)
    r = x - hi.astype(F32)
    mid = r.astype(BF16)
    lo = (r - mid.astype(F32)).astype(BF16)
    return hi, mid, lo


def _dot3(a, b, dims=_NN):
    ah, al = _split2(a)
    bh, bl = _split2(b)
    return _dot(ah, bh, dims) + (_dot(ah, bl, dims) + _dot(al, bh, dims))


def _dot_exact_lhs(a_bf16, x, dims=_NN):
    hi, mid, lo = _split3(x)
    return _dot(a_bf16, hi, dims) + (_dot(a_bf16, mid, dims) + _dot(a_bf16, lo, dims))


def _sigmoid(x):
    return 1.0 / (1.0 + jnp.exp(-x))


def _softplus(x):
    return jnp.maximum(x, 0.0) + jnp.log(1.0 + jnp.exp(-jnp.abs(x)))


def _layer_norm(z, g, b):
    mu = jnp.mean(z, axis=-1, keepdims=True)
    zc = z - mu
    var = jnp.mean(zc * zc, axis=-1, keepdims=True)
    return zc * lax.rsqrt(var + LN_EPS) * g + b


def _params(sem):
    return pltpu.CompilerParams(dimension_semantics=sem, vmem_limit_bytes=VMEM_LIMIT)


_IN_SPLITS = (RWKV_COLS, D_F, D_F, D_F, D_F, FL_PAD)


def _in_proj_body(x_ref, w_ref, *out_refs):
    x = x_ref[...].astype(BF16)
    col = 0
    for ref, width in zip(out_refs, _IN_SPLITS):
        for c0 in range(0, width, 512):
            c1 = min(c0 + 512, width)
            ref[:, c0:c1] = _dot(x, w_ref[:, col + c0:col + c1])
        col += width


def _in_proj(x, w):
    n = x.shape[0]
    tm = 256
    return pl.pallas_call(
        _in_proj_body,
        grid=(n // tm,),
        in_specs=[pl.BlockSpec((tm, D_MODEL), lambda i: (i, 0)),
                  pl.BlockSpec((D_MODEL, IN_COLS_PAD), lambda i: (0, 0))],
        out_specs=[pl.BlockSpec((tm, wd), lambda i: (i, 0)) for wd in _IN_SPLITS],
        out_shape=[jax.ShapeDtypeStruct((n, wd), F32) for wd in _IN_SPLITS],
        compiler_params=_params(("parallel",)),
        name="in_proj",
    )(x, w)


GROUP = 4
GROUP_W = GROUP * HEAD_DIM
RWKV_SUB = 4


def _block_diag(x, keep):
    return jnp.where(keep, jnp.concatenate([x] * GROUP, axis=0), jnp.zeros((), x.dtype))


def _heads_mm(a, b, keep, dims=_NN, exact=True):
    if not exact:
        return _dot(a.astype(BF16), _block_diag(b.astype(BF16), keep), dims)
    ah, al = _split2(a)
    bh, bl = _split2(b)
    dh = _block_diag(bh, keep)
    dl = _block_diag(bl, keep)
    return _dot(ah, dh, dims) + (_dot(ah, dl, dims) + _dot(al, dh, dims))


def _fold_heads(f, keep):
    f = jnp.where(keep, f, 0.0)
    return (f[0:HEAD_DIM] + f[HEAD_DIM:2 * HEAD_DIM]) + (f[2 * HEAD_DIM:3 * HEAD_DIM] + f[3 * HEAD_DIM:4 * HEAD_DIM])


def _unit_lower_inverses(ls, c_len, t_row, t_col, keep_tt):
    shift = INV_BASE.bit_length() - 1
    same = (t_row >> shift) == (t_col >> shift)
    eye = jnp.where(t_row == t_col, 1.0, 0.0)
    p = [jnp.where(same, -l, 0.0) for l in ls]
    x = [eye + n for n in p]
    for _ in range(shift - 1):
        p = [_heads_mm(pi, pi, keep_tt) for pi in p]
        x = [xi + _heads_mm(xi, pi, keep_tt) for xi, pi in zip(x, p)]
    size = 2 * INV_BASE
    while size <= c_len:
        s_hi = size.bit_length() - 1
        off = ((t_row >> s_hi) == (t_col >> s_hi)) & ((t_row >> (s_hi - 1)) != (t_col >> (s_hi - 1)))
        xq = [_heads_mm(xi, jnp.where(off, l, 0.0), keep_tt) for xi, l in zip(x, ls)]
        x = [xi - _heads_mm(xqi, xi, keep_tt) for xi, xqi in zip(x, xq)]
        size *= 2
    return x


def _rwkv_body(c_len, n_sub, n_steps, pr_ref, sp_ref, s0_ref, mu_ref, w0_ref, w2_ref, a0_ref, a2_ref, g2_ref,
               kk_ref, ka_ref, rk_ref, lnw_ref, lnb_ref, ones_ref, out_ref, sout_ref, carry_ref, s_ref):
    ci = pl.program_id(1)
    rows = c_len * n_sub
    n_groups = N_HEADS // GROUP

    @pl.when(ci == 0)
    def _():
        carry_ref[...] = sp_ref[0]
        for gi in range(n_groups):
            s_ref[gi] = jnp.concatenate([s0_ref[0, GROUP * gi + j] for j in range(GROUP)], axis=-1)

    pr = pr_ref[...]
    trow = lax.broadcasted_iota(I32, (rows, 1), 0)
    prev = jnp.where(trow == 0, carry_ref[...], pltpu.roll(pr, 1, 0))
    carry_ref[...] = pr[rows - 1:rows, :]
    xs = pr + (prev - pr) * mu_ref[...]
    xr = xs[:, 0:D_R]
    xk = xs[:, D_R:2 * D_R]
    xv = xs[:, 2 * D_R:3 * D_R]
    o = 3 * D_R
    xw = xs[:, o:o + DECAY_LORA]
    xa = xs[:, o + DECAY_LORA:o + DECAY_LORA + A_LORA]
    xg = xs[:, o + DECAY_LORA + A_LORA:RWKV_COLS]

    z = w0_ref[...] + _dot3(jnp.tanh(xw), w2_ref[...])
    lw = -jnp.exp(-_softplus(-z) - 0.5)
    a = _sigmoid(a0_ref[...] + _dot3(xa, a2_ref[...]))
    g = _dot3(_sigmoid(xg), g2_ref[...])
    kk_raw = xk * kk_ref[...]
    k_mod = xk * (1.0 + (a - 1.0) * ka_ref[...])

    shift = c_len.bit_length() - 1
    r2 = lax.broadcasted_iota(I32, (rows, rows), 0)
    c2 = lax.broadcasted_iota(I32, (rows, rows), 1)
    within = ((r2 >> shift) == (c2 >> shift)) & (r2 >= c2)
    cl = _dot_exact_lhs(jnp.where(within, 1.0, 0.0).astype(BF16), lw)

    ones = ones_ref[...]

    def head_sum(x):
        hi, lo = _split2(x)
        return _dot(hi, ones) + _dot(lo, ones)

    kk = kk_raw / jnp.maximum(jnp.sqrt(head_sum(kk_raw * kk_raw)), 1e-12)
    eg = jnp.exp(cl)
    e_inv = jnp.exp(-cl)
    r_dec = xr * eg
    kk_dec = kk * jnp.exp(cl - lw)
    b_und = kk * a * e_inv
    k_und = k_mod * e_inv

    wt = GROUP * c_len
    t_row = lax.broadcasted_iota(I32, (c_len, wt), 0)
    t_col = lax.broadcasted_iota(I32, (c_len, wt), 1) & (c_len - 1)
    strict = t_row > t_col
    incl = t_row >= t_col
    hd_shift = HEAD_DIM.bit_length() - 1
    keep_tt = (lax.broadcasted_iota(I32, (wt, wt), 0) >> shift) == (lax.broadcasted_iota(I32, (wt, wt), 1) >> shift)
    keep_tf = (lax.broadcasted_iota(I32, (wt, GROUP_W), 0) >> shift) == (
        lax.broadcasted_iota(I32, (wt, GROUP_W), 1) >> hd_shift)
    keep_ff = (lax.broadcasted_iota(I32, (GROUP_W, GROUP_W), 0) >> hd_shift) == (
        lax.broadcasted_iota(I32, (GROUP_W, GROUP_W), 1) >> hd_shift)

    chains = [(c, gi) for c in range(n_sub) for gi in range(n_groups)]
    cut = lambda x, c, gi: x[c * c_len:(c + 1) * c_len, gi * GROUP_W:(gi + 1) * GROUP_W]
    lhs = [jnp.concatenate([cut(kk_dec, c, gi), cut(r_dec, c, gi)], axis=0) for c, gi in chains]
    bu = [cut(b_und, c, gi) for c, gi in chains]
    ku = [cut(k_und, c, gi) for c, gi in chains]
    vh = [cut(xv, c, gi) for c, gi in chains]
    n_ch = range(len(chains))
    gb = [_heads_mm(lhs[i], bu[i], keep_tf, _NT) for i in n_ch]
    gk = [_heads_mm(lhs[i], ku[i], keep_tf, _NT, exact=False) for i in n_ch]
    l_b = [jnp.where(strict, m[:c_len], 0.0) for m in gb]
    m_rb = [jnp.where(incl, m[c_len:], 0.0) for m in gb]
    l_k = [jnp.where(strict, m[:c_len], 0.0) for m in gk]
    m_rk = [jnp.where(incl, m[c_len:], 0.0) for m in gk]
    t_inv = _unit_lower_inverses(l_b, c_len, t_row, t_col, keep_tt)
    lkv = [_heads_mm(l_k[i], vh[i], keep_tf, exact=False) for i in n_ch]
    mkv = [_heads_mm(m_rk[i], vh[i], keep_tf, exact=False) for i in n_ch]
    fv = [_fold_heads(_dot(vh[i].astype(BF16), ku[i].astype(BF16), _TN), keep_ff) for i in n_ch]

    state = [s_ref[gi] for gi in range(n_groups)]
    y_rows = []
    for c in range(n_sub):
        ids = [c * n_groups + gi for gi in range(n_groups)]
        ps = [_heads_mm(lhs[i], state[gi], keep_ff, _NT, exact=False) for gi, i in enumerate(ids)]
        u = [-_heads_mm(t_inv[i], ps[gi][:c_len] + lkv[i], keep_tf, exact=False) for gi, i in enumerate(ids)]
        y = [ps[gi][c_len:] + mkv[i] + _heads_mm(m_rb[i], u[gi], keep_tf, exact=False) for gi, i in enumerate(ids)]
        fu = [_fold_heads(_dot(u[gi].astype(BF16), bu[i].astype(BF16), _TN), keep_ff) for gi, i in enumerate(ids)]
        last = c * c_len + c_len - 1
        state = [(state[gi] + fv[i] + fu[gi]) * eg[last:last + 1, gi * GROUP_W:(gi + 1) * GROUP_W]
                 for gi, i in enumerate(ids)]
        y_rows.append(jnp.concatenate(y, axis=-1))
    for gi in range(n_groups):
        s_ref[gi] = state[gi]

    y = jnp.concatenate(y_rows, axis=0)
    yc = y - head_sum(y) * (1.0 / HEAD_DIM)
    var = head_sum(yc * yc) * (1.0 / HEAD_DIM)
    yn = yc * lax.rsqrt(var + GN_EPS) * lnw_ref[...] + lnb_ref[...]
    bonus = head_sum(xr * k_mod * rk_ref[...]) * xv
    out_ref[...] = (yn + bonus) * g

    @pl.when(ci == n_steps - 1)
    def _():
        for gi in range(n_groups):
            for j in range(GROUP):
                sout_ref[0, GROUP * gi + j] = state[gi][:, j * HEAD_DIM:(j + 1) * HEAD_DIM]


def _rwkv(pr, shift_prev, s0, lp, n_seq, seq_len, row0):
    c_len = min(RWKV_CHUNK, seq_len)
    n_sub = min(RWKV_SUB, seq_len // c_len)
    rows = c_len * n_sub
    n_steps = seq_len // rows
    blk0 = row0 // rows
    vec = lambda wd: pl.BlockSpec((1, wd), lambda b, c: (0, 0))
    mat = lambda r, wd: pl.BlockSpec((r, wd), lambda b, c: (0, 0))
    in_specs = [
        pl.BlockSpec((rows, RWKV_COLS), lambda b, c: (blk0 + b * n_steps + c, 0)),
        pl.BlockSpec((1, 1, RWKV_COLS), lambda b, c: (b, 0, 0)),
        pl.BlockSpec((1, N_HEADS, HEAD_DIM, HEAD_DIM), lambda b, c: (b, 0, 0, 0)),
        vec(RWKV_COLS), vec(D_R), mat(DECAY_LORA, D_R), vec(D_R), mat(A_LORA, D_R), mat(G_LORA, D_R),
        vec(D_R), vec(D_R), vec(D_R), vec(D_R), vec(D_R), mat(D_R, D_R),
    ]
    args = [pr, shift_prev, s0, lp['mu'], lp['w0'], lp['w2'], lp['a0'], lp['a2'], lp['g2'],
            lp['k_k'], lp['k_a'], lp['r_k'], lp['lnx_w'], lp['lnx_b'], lp['ones_bd']]
    return pl.pallas_call(
        functools.partial(_rwkv_body, c_len, n_sub, n_steps),
        grid=(n_seq, n_steps),
        in_specs=in_specs,
        out_specs=[pl.BlockSpec((rows, D_R), lambda b, c: (b * n_steps + c, 0)),
                   pl.BlockSpec((1, N_HEADS, HEAD_DIM, HEAD_DIM), lambda b, c: (b, 0, 0, 0))],
        out_shape=[jax.ShapeDtypeStruct((n_seq * seq_len, D_R), F32),
                   jax.ShapeDtypeStruct((n_seq, N_HEADS, HEAD_DIM, HEAD_DIM), F32)],
        scratch_shapes=[pltpu.VMEM((1, RWKV_COLS), F32), pltpu.VMEM((N_HEADS // GROUP, HEAD_DIM, GROUP_W), F32)],
        compiler_params=_params(("arbitrary", "arbitrary")),
        name="rwkv_mixer",
    )(*args)


SLOT = 2 * HEAD_DIM
C_LANE = HEAD_DIM


def _fox_prep_body(q_ref, k_ref, v_ref, c_ref, qg_ref, kg_ref, ones_ref, place_ref, pcq_ref, pck_ref, oneq_ref,
                   onek_ref, qa_ref, kn_ref, ka_ref, vb_ref):
    ones = ones_ref[...]

    def rms(x, gain):
        hi, lo = _split2(x * x)
        ss = _dot(hi, ones) + _dot(lo, ones)
        return x * lax.rsqrt(ss * (1.0 / HEAD_DIM) + QK_EPS) * gain

    qn = rms(q_ref[...], qg_ref[...]) * SCALE
    kn = rms(k_ref[...], kg_ref[...])
    kn_ref[...] = kn
    vb_ref[...] = v_ref[...].astype(BF16)
    c_parts = _split3(c_ref[...])

    def slots(xb, pc_ref, one_ref):
        acc = _dot(xb, place_ref[...]) + one_ref[...]
        for j in range(3):
            acc = acc + _dot(c_parts[j], pc_ref[j])
        return acc.astype(BF16)

    qa_ref[...] = slots(qn.astype(BF16), pcq_ref, oneq_ref)
    ka_ref[...] = slots(kn.astype(BF16), pck_ref, onek_ref)


def _slot_constants():
    d = jnp.arange(D_F, dtype=I32)
    lane = jnp.arange(N_HEADS * SLOT, dtype=I32)
    place = (lane[None, :] == (d // HEAD_DIM * SLOT + d % HEAD_DIM)[:, None]).astype(BF16)
    h = jnp.arange(FL_PAD, dtype=I32)[None, :, None]
    j = jnp.arange(3, dtype=I32)[:, None, None]
    is_head = h < N_HEADS
    pcq = ((lane[None, None, :] == h * SLOT + C_LANE + j) & is_head).astype(BF16)
    pck = -((lane[None, None, :] == h * SLOT + C_LANE + 3 + j) & is_head).astype(BF16)
    in_slot = lane % SLOT
    oneq = ((in_slot >= C_LANE + 3) & (in_slot < C_LANE + 6)).astype(F32).reshape(1, -1)
    onek = ((in_slot >= C_LANE) & (in_slot < C_LANE + 3)).astype(F32).reshape(1, -1)
    return place, pcq, pck, oneq, onek


def _fox_prep(q, k, v, c, q_gain, k_gain, ones_bd, slot_consts):
    n = q.shape[0]
    tm = 256
    wide = N_HEADS * SLOT
    row = lambda wd: pl.BlockSpec((tm, wd), lambda i: (i, 0))
    vec = lambda wd: pl.BlockSpec((1, wd), lambda i: (0, 0))
    full = lambda *shape: pl.BlockSpec(shape, lambda i: (0,) * len(shape))
    return pl.pallas_call(
        _fox_prep_body,
        grid=(n // tm,),
        in_specs=[row(D_F), row(D_F), row(D_F), row(FL_PAD), vec(D_F), vec(D_F), full(D_F, D_F),
                  full(D_F, wide), full(3, FL_PAD, wide), full(3, FL_PAD, wide), vec(wide), vec(wide)],
        out_specs=[row(wide), row(D_F), row(wide), row(D_F)],
        out_shape=[jax.ShapeDtypeStruct((n, wide), BF16), jax.ShapeDtypeStruct((n, D_F), F32),
                   jax.ShapeDtypeStruct((n, wide), BF16), jax.ShapeDtypeStruct((n, D_F), BF16)],
        compiler_params=_params(("parallel",)),
        name="fox_prep",
    )(q, k, v, c, q_gain, k_gain, ones_bd, *slot_consts)


def _logf_cumsum_body(fl_ref, bf_ref, lf_ref, c_ref, carry_ref):
    @pl.when(pl.program_id(1) == 0)
    def _():
        carry_ref[...] = jnp.zeros_like(carry_ref)

    lf = -_softplus(-(fl_ref[...] + bf_ref[...]))
    lf_ref[...] = lf
    t = lf.shape[0]
    row = lax.broadcasted_iota(I32, (t, t), 0)
    col = lax.broadcasted_iota(I32, (t, t), 1)
    cs = _dot_exact_lhs(jnp.where(row >= col, 1.0, 0.0).astype(BF16), lf) + carry_ref[...]
    c_ref[...] = cs
    carry_ref[...] = cs[t - 1:t, :]


def _logf_cumsum(fl, b_f, n_seq, seq_len, row0):
    tc = min(seq_len, 256)
    nt = seq_len // tc
    blk0 = row0 // tc
    out = pl.BlockSpec((tc, FL_PAD), lambda b, j: (b * nt + j, 0))
    return pl.pallas_call(
        _logf_cumsum_body,
        grid=(n_seq, nt),
        in_specs=[pl.BlockSpec((tc, FL_PAD), lambda b, j: (blk0 + b * nt + j, 0)),
                  pl.BlockSpec((1, FL_PAD), lambda b, j: (0, 0))],
        out_specs=[out, out],
        out_shape=[jax.ShapeDtypeStruct((n_seq * seq_len, FL_PAD), F32)] * 2,
        scratch_shapes=[pltpu.VMEM((1, FL_PAD), F32)],
        compiler_params=_params(("arbitrary", "arbitrary")),
        name="logf_cumsum",
    )(fl, b_f)


ATT_TILE = 512


def _attn_prompt_body(q_ref, k_ref, v_ref, o_ref):
    i = pl.program_id(2)
    t = ATT_TILE
    row = lax.broadcasted_iota(I32, (t, t), 0)
    col = lax.broadcasted_iota(I32, (t, t), 1)
    causal = row >= col
    pair = range(2)
    q = [q_ref[:, hh * SLOT:(hh + 1) * SLOT] for hh in pair]

    def tile(j, carry, masked):
        m, l, acc = carry
        j0 = pl.multiple_of(j * t, t)
        s = [_dot(q[hh], k_ref[pl.ds(j0, t), hh * SLOT:(hh + 1) * SLOT], _NT) for hh in pair]
        if masked:
            s = [jnp.where(causal, sh, -jnp.inf) for sh in s]
        m_new = [jnp.maximum(m[hh], jnp.max(s[hh], axis=-1, keepdims=True)) for hh in pair]
        alpha = [jnp.exp(m[hh] - m_new[hh]) for hh in pair]
        p = [jnp.exp(s[hh] - m_new[hh]) for hh in pair]
        l = [alpha[hh] * l[hh] + jnp.sum(p[hh], axis=-1, keepdims=True) for hh in pair]
        pv = [_dot(p[hh].astype(BF16), v_ref[pl.ds(j0, t), hh * HEAD_DIM:(hh + 1) * HEAD_DIM]) for hh in pair]
        acc = [alpha[hh] * acc[hh] + pv[hh] for hh in pair]
        return m_new, l, acc

    init = ([jnp.full((t, 1), -jnp.inf, F32)] * 2, [jnp.zeros((t, 1), F32)] * 2,
            [jnp.zeros((t, HEAD_DIM), F32)] * 2)
    carry = lax.fori_loop(0, i, lambda j, c: tile(j, c, False), init)
    _, l, acc = tile(i, carry, True)
    o_ref[...] = jnp.concatenate([acc[hh] / l[hh] for hh in pair], axis=-1)


def _attn_prompt(qa, ka, vb, n_seq, seq_len):
    t = ATT_TILE
    nq = seq_len // t
    return pl.pallas_call(
        _attn_prompt_body,
        grid=(n_seq, N_HEADS // 2, nq),
        in_specs=[pl.BlockSpec((t, 2 * SLOT), lambda b, p, i: (b * nq + i, p)),
                  pl.BlockSpec((seq_len, 2 * SLOT), lambda b, p, i: (b, p)),
                  pl.BlockSpec((seq_len, 2 * HEAD_DIM), lambda b, p, i: (b, p))],
        out_specs=pl.BlockSpec((t, 2 * HEAD_DIM), lambda b, p, i: (b * nq + i, p)),
        out_shape=jax.ShapeDtypeStruct((n_seq * seq_len, D_F), F32),
        compiler_params=_params(("parallel", "parallel", "arbitrary")),
        name="fox_attn_prompt",
    )(qa, ka, vb)


SAMPLE_CHUNK = 2048
TAIL_BLOCK = 1024


def _attn_sample_body(n_chunks, q_ref, kn_ref, vn_ref, kc_ref, vc_ref, lp_ref, after_ref, o_ref,
                      m_ref, l_ref, acc_ref, suffix_ref):
    j = pl.program_id(1)
    n = q_ref.shape[0]
    heads = range(N_HEADS)

    @pl.when(j == 0)
    def _():
        m_ref[...] = jnp.full(m_ref.shape, -jnp.inf, F32)
        l_ref[...] = jnp.zeros_like(l_ref)
        acc_ref[...] = jnp.zeros_like(acc_ref)
        suffix_ref[...] = jnp.zeros_like(suffix_ref)

    after = after_ref[...]
    tp = lp_ref.shape[3]
    tails = []
    suffix = suffix_ref[...]
    for b0 in range(tp - TAIL_BLOCK, -1, -TAIL_BLOCK):
        lp = lp_ref[0, 0, :, b0:b0 + TAIL_BLOCK]
        hi, mid, lo = _split3(lp)
        tails.insert(0, _dot(hi, after) + (_dot(mid, after) + _dot(lo, after)) + suffix)
        suffix = suffix + jnp.sum(lp, axis=-1, keepdims=True)
    suffix_ref[...] = suffix
    tail = jnp.concatenate(tails, axis=-1)

    q_slot = [q_ref[:, h * SLOT:(h + 1) * SLOT] for h in heads]
    c_col = [(qs[:, C_LANE:C_LANE + 1].astype(F32) + qs[:, C_LANE + 1:C_LANE + 2].astype(F32)
              + qs[:, C_LANE + 2:C_LANE + 3].astype(F32)) for qs in q_slot]
    s = [_dot(q_slot[h][:, :HEAD_DIM], kc_ref[0, 0, h].astype(BF16)) + c_col[h] + tail[h:h + 1, :] for h in heads]
    m_old = [m_ref[h] for h in heads]
    m_new = [jnp.maximum(m_old[h], jnp.max(s[h], axis=-1, keepdims=True)) for h in heads]
    alpha = [jnp.exp(m_old[h] - m_new[h]) for h in heads]
    p = [jnp.exp(s[h] - m_new[h]) for h in heads]
    pv = [_dot(p[h].astype(BF16), vc_ref[0, 0, h].astype(BF16), _NT) for h in heads]
    for h in heads:
        m_ref[h] = m_new[h]
        l_ref[h] = alpha[h] * l_ref[h] + jnp.sum(p[h], axis=-1, keepdims=True)
        acc_ref[h] = alpha[h] * acc_ref[h] + pv[h]

    @pl.when(j == n_chunks - 1)
    def _():
        row = lax.broadcasted_iota(I32, (n, n), 0)
        col = lax.broadcasted_iota(I32, (n, n), 1)
        s_new = [jnp.where(row >= col, _dot(q_slot[h], kn_ref[:, h * SLOT:(h + 1) * SLOT], _NT), -jnp.inf)
                 for h in heads]
        m_fin = [jnp.maximum(m_ref[h], jnp.max(s_new[h], axis=-1, keepdims=True)) for h in heads]
        a_fin = [jnp.exp(m_ref[h] - m_fin[h]) for h in heads]
        p_new = [jnp.exp(s_new[h] - m_fin[h]) for h in heads]
        l_fin = [a_fin[h] * l_ref[h] + jnp.sum(p_new[h], axis=-1, keepdims=True) for h in heads]
        acc = [a_fin[h] * acc_ref[h] + _dot(p_new[h].astype(BF16), vn_ref[:, h * HEAD_DIM:(h + 1) * HEAD_DIM])
               for h in heads]
        o_ref[...] = jnp.concatenate([acc[h] / l_fin[h] for h in heads], axis=-1)


def _attn_sample(qa, ka, vb, cache_k, cache_v, logf_rows, layer, n_seq, n_new, row0):
    past = cache_k.shape[4]
    tp = min(SAMPLE_CHUNK, past)
    n_chunks = past // tp
    blk0 = row0 // n_new
    frame = jnp.arange(TAIL_BLOCK, dtype=I32)
    after = (frame[:, None] > frame[None, :]).astype(BF16)
    rows = lambda wd: pl.BlockSpec((n_new, wd), lambda b, j: (blk0 + b, 0))
    cache = lambda: pl.BlockSpec((1, 1, N_HEADS, HEAD_DIM, tp), lambda b, j: (layer, b, 0, 0, n_chunks - 1 - j))
    return pl.pallas_call(
        functools.partial(_attn_sample_body, n_chunks),
        grid=(n_seq, n_chunks),
        in_specs=[rows(N_HEADS * SLOT), rows(N_HEADS * SLOT), rows(D_F), cache(), cache(),
                  pl.BlockSpec((1, 1, N_HEADS, tp), lambda b, j: (layer, b, 0, n_chunks - 1 - j)),
                  pl.BlockSpec((TAIL_BLOCK, TAIL_BLOCK), lambda b, j: (0, 0))],
        out_specs=pl.BlockSpec((n_new, D_F), lambda b, j: (b, 0)),
        out_shape=jax.ShapeDtypeStruct((n_seq * n_new, D_F), F32),
        scratch_shapes=[pltpu.VMEM((N_HEADS, n_new, 1), F32), pltpu.VMEM((N_HEADS, n_new, 1), F32),
                        pltpu.VMEM((N_HEADS, n_new, HEAD_DIM), F32), pltpu.VMEM((N_HEADS, 1), F32)],
        compiler_params=_params(("parallel", "arbitrary")),
        name="fox_attn_sample",
    )(qa, ka, vb, cache_k, cache_v, logf_rows, after)


def _out_proj_body(tiles_p, ryp_ref, rys_ref, fop_ref, fos_ref, og_ref, x_ref, w_ref, g_ref, b_ref, o_ref):
    from_prompt = pl.program_id(0) < tiles_p
    ry = jnp.where(from_prompt, ryp_ref[...], rys_ref[...]).astype(BF16)
    fo = jnp.where(from_prompt, fop_ref[...], fos_ref[...])
    fy = (fo * _sigmoid(og_ref[...])).astype(BF16)
    m = _dot(ry, w_ref[0:D_R, :]) + _dot(fy, w_ref[D_R:D_R + D_F, :])
    o_ref[...] = _layer_norm(ALPHA * x_ref[...] + m, g_ref[...], b_ref[...])


def _out_proj_ln(ry_p, ry_s, fo_p, fo_s, og, x, w, g, b):
    n = x.shape[0]
    tm = 256
    tiles_p = ry_p.shape[0] // tm
    row = lambda wd: pl.BlockSpec((tm, wd), lambda i: (i, 0))
    row_p = lambda wd: pl.BlockSpec((tm, wd), lambda i: (jnp.minimum(i, tiles_p - 1), 0))
    row_s = lambda wd: pl.BlockSpec((tm, wd), lambda i: (jnp.maximum(i - tiles_p, 0), 0))
    vec = pl.BlockSpec((1, D_MODEL), lambda i: (0, 0))
    return pl.pallas_call(
        functools.partial(_out_proj_body, tiles_p),
        grid=(n // tm,),
        in_specs=[row_p(D_R), row_s(D_R), row_p(D_F), row_s(D_F), row(D_F), row(D_MODEL),
                  pl.BlockSpec((D_R + D_F, D_MODEL), lambda i: (0, 0)), vec, vec],
        out_specs=row(D_MODEL),
        out_shape=jax.ShapeDtypeStruct((n, D_MODEL), F32),
        compiler_params=_params(("parallel",)),
        name="out_proj_ln",
    )(ry_p, ry_s, fo_p, fo_s, og, x, w, g, b)


def _router_body(x_ref, rw_ref, rb_ref, earlier_ref, e_ref, g_ref, r_ref, count_ref, seen_ref):
    tn = x_ref.shape[0]
    scores = _sigmoid(_dot3(rw_ref[...], x_ref[...], _NT))
    sel = scores + rb_ref[...]
    sel4 = sel.reshape(N_GROUPS, EXPERTS_PER_GROUP, tn)
    sc4 = scores.reshape(N_GROUPS, EXPERTS_PER_GROUP, tn)
    lane_e = lax.broadcasted_iota(I32, (N_GROUPS, EXPERTS_PER_GROUP, tn), 1)

    def top2(vals, idx_iota, axis):
        m1 = jnp.max(vals, axis=axis, keepdims=True)
        i1 = jnp.min(jnp.where(vals == m1, idx_iota, EXPERTS_PER_GROUP), axis=axis, keepdims=True)
        rest = jnp.where(idx_iota == i1, -jnp.inf, vals)
        m2 = jnp.max(rest, axis=axis, keepdims=True)
        i2 = jnp.min(jnp.where(rest == m2, idx_iota, EXPERTS_PER_GROUP), axis=axis, keepdims=True)
        return m1, i1, m2, i2

    m1, _, m2, _ = top2(sel4, lane_e, 1)
    gsum = m1 + m2
    g_iota = lax.broadcasted_iota(I32, (N_GROUPS, 1, tn), 0)
    gmax = jnp.max(gsum, axis=0, keepdims=True)
    g_idx = jnp.min(jnp.where(gsum == gmax, g_iota, N_GROUPS), axis=0, keepdims=True)
    pick = g_iota == g_idx
    sel_g = jnp.max(jnp.where(pick, sel4, -jnp.inf), axis=0)
    sc_g = jnp.max(jnp.where(pick, sc4, -jnp.inf), axis=0)
    e_iota = lax.broadcasted_iota(I32, (EXPERTS_PER_GROUP, tn), 0)
    _, i1, _, i2 = top2(sel_g, e_iota, 0)
    gate1 = jnp.sum(jnp.where(e_iota == i1, sc_g, 0.0), axis=0, keepdims=True)
    gate2 = jnp.sum(jnp.where(e_iota == i2, sc_g, 0.0), axis=0, keepdims=True)
    tot = gate1 + gate2
    base = g_idx[0] * EXPERTS_PER_GROUP
    e1 = base + i1
    e2 = base + i2
    e_ref[...] = jnp.concatenate([e1, e2], axis=0)
    g_ref[...] = jnp.concatenate([gate1 / tot, gate2 / tot], axis=0)

    @pl.when(pl.program_id(0) == 0)
    def _():
        seen_ref[...] = jnp.zeros_like(seen_ref)

    all_e = lax.broadcasted_iota(I32, (N_EXPERTS, tn), 0)
    hit1 = all_e == e1
    hit2 = all_e == e2
    hits = jnp.where(hit1 | hit2, 1.0, 0.0)
    before = _dot(hits.astype(BF16), earlier_ref[...]) + seen_ref[...]
    r1 = jnp.sum(jnp.where(hit1, before, 0.0), axis=0, keepdims=True)
    r2 = jnp.sum(jnp.where(hit2, before, 0.0), axis=0, keepdims=True)
    r_ref[...] = jnp.concatenate([r1, r2], axis=0).astype(I32)
    seen_ref[...] = seen_ref[...] + jnp.sum(hits, axis=-1, keepdims=True)
    count_ref[...] = seen_ref[...].astype(I32)


def _router(x, rw_t, rb_col):
    n = x.shape[0]
    tn = 512
    tok = jnp.arange(tn, dtype=I32)
    earlier = (tok[:, None] < tok[None, :]).astype(BF16)
    pair = lambda: pl.BlockSpec((TOP_K, tn), lambda i: (0, i))
    return pl.pallas_call(
        _router_body,
        grid=(n // tn,),
        in_specs=[pl.BlockSpec((tn, D_MODEL), lambda i: (i, 0)),
                  pl.BlockSpec((N_EXPERTS, D_MODEL), lambda i: (0, 0)),
                  pl.BlockSpec((N_EXPERTS, 1), lambda i: (0, 0)),
                  pl.BlockSpec((tn, tn), lambda i: (0, 0))],
        out_specs=[pair(), pair(), pair(), pl.BlockSpec((N_EXPERTS, 1), lambda i: (0, 0))],
        out_shape=[jax.ShapeDtypeStruct((TOP_K, n), I32), jax.ShapeDtypeStruct((TOP_K, n), F32),
                   jax.ShapeDtypeStruct((TOP_K, n), I32), jax.ShapeDtypeStruct((N_EXPERTS, 1), I32)],
        scratch_shapes=[pltpu.VMEM((N_EXPERTS, 1), F32)],
        compiler_params=_params(("arbitrary",)),
        name="router",
    )(x, rw_t, rb_col, earlier)


def _slot_rows_body(e_ref, r_ref, start_ref, pos_ref):
    tn = e_ref.shape[1]
    all_e = lax.broadcasted_iota(I32, (N_EXPERTS, tn), 0)
    rows = [jnp.sum(jnp.where(all_e == e_ref[k:k + 1, :], start_ref[...], 0), axis=0, keepdims=True)
            for k in range(TOP_K)]
    pos_ref[...] = r_ref[...] + jnp.concatenate(rows, axis=0)


def _slot_rows(eidx_t, rank_t, expert_start):
    n = eidx_t.shape[1]
    tn = 512
    pair = lambda: pl.BlockSpec((TOP_K, tn), lambda i: (0, i))
    return pl.pallas_call(
        _slot_rows_body,
        grid=(n // tn,),
        in_specs=[pair(), pair(), pl.BlockSpec((N_EXPERTS, 1), lambda i: (0, 0))],
        out_specs=pair(),
        out_shape=jax.ShapeDtypeStruct((TOP_K, n), I32),
        compiler_params=_params(("parallel",)),
        name="moe_slot_rows",
    )(eidx_t, rank_t, expert_start)


def _row_copy(src_hbm, src_row, dst, dst_row, sem):
    return pltpu.make_async_copy(src_hbm.at[pl.ds(src_row, 1)], dst.at[pl.ds(dst_row, 1)], sem)


def _expert_body(be_ref, used_ref, tok_ref, tok_next_ref, x_hbm, w1_ref, w3_ref, w2_ref, y_ref,
                 buf_ref, w1b_ref, w3b_ref, w2b_ref, sem):
    i = pl.program_id(0)
    n_used = used_ref[0]
    slot = lax.rem(i, 2)

    def start_gather(ids_ref, s):
        def body(r, carry):
            _row_copy(x_hbm, ids_ref[0, 0, r], buf_ref.at[s], r, sem.at[s]).start()
            return carry
        lax.fori_loop(0, MOE_BLOCK, body, 0, unroll=8)

    @pl.when(jnp.logical_and(i == 0, n_used > 0))
    def _():
        start_gather(tok_ref, 0)

    @pl.when(i + 1 < n_used)
    def _():
        start_gather(tok_next_ref, 1 - slot)

    @pl.when(i < n_used)
    def _():
        @pl.when(jnp.logical_or(i == 0, be_ref[i] != be_ref[jnp.maximum(i - 1, 0)]))
        def _():
            w1b_ref[...] = w1_ref[0, 0].astype(BF16)
            w3b_ref[...] = w3_ref[0, 0].astype(BF16)
            w2b_ref[...] = w2_ref[0, 0].astype(BF16)

        def wait(r, carry):
            _row_copy(x_hbm, 0, buf_ref.at[slot], r, sem.at[slot]).wait()
            return carry
        lax.fori_loop(0, MOE_BLOCK, wait, 0, unroll=8)

        h = buf_ref[slot].astype(BF16)
        a = _dot(h, w1b_ref[...])
        b = _dot(h, w3b_ref[...])
        act = (a * _sigmoid(a) * b).astype(BF16)
        y_ref[...] = _dot(act, w2b_ref[...])

    @pl.when(i >= n_used)
    def _():
        y_ref[...] = jnp.zeros_like(y_ref)


def _experts(block_expert, n_used, tok_blocks, x, w1, w3, w2, layer):
    nb = block_expert.shape[0]
    ids = lambda f: pl.BlockSpec((1, 1, MOE_BLOCK), f, memory_space=pltpu.SMEM)
    grid_spec = pltpu.PrefetchScalarGridSpec(
        num_scalar_prefetch=2,
        grid=(nb,),
        in_specs=[ids(lambda i, be, nu: (i, 0, 0)),
                  ids(lambda i, be, nu: (jnp.minimum(i + 1, nb - 1), 0, 0)),
                  pl.BlockSpec(memory_space=pl.ANY),
                  pl.BlockSpec((1, 1, D_MODEL, D_EXPERT), lambda i, be, nu: (layer, be[i], 0, 0)),
                  pl.BlockSpec((1, 1, D_MODEL, D_EXPERT), lambda i, be, nu: (layer, be[i], 0, 0)),
                  pl.BlockSpec((1, 1, D_EXPERT, D_MODEL), lambda i, be, nu: (layer, be[i], 0, 0))],
        out_specs=pl.BlockSpec((MOE_BLOCK, D_MODEL), lambda i, be, nu: (i, 0)),
        scratch_shapes=[pltpu.VMEM((2, MOE_BLOCK, D_MODEL), F32),
                        pltpu.VMEM((D_MODEL, D_EXPERT), BF16), pltpu.VMEM((D_MODEL, D_EXPERT), BF16),
                        pltpu.VMEM((D_EXPERT, D_MODEL), BF16), pltpu.SemaphoreType.DMA((2,))],
    )
    return pl.pallas_call(
        _expert_body,
        grid_spec=grid_spec,
        out_shape=jax.ShapeDtypeStruct((nb * MOE_BLOCK, D_MODEL), F32),
        compiler_params=_params(("arbitrary",)),
        name="moe_experts",
    )(block_expert, n_used, tok_blocks, tok_blocks, x, w1, w3, w2)


COMBINE_TILE = 128


def _combine_body(pos_ref, pos_next_ref, y_hbm, x_ref, gate_ref, g_ref, b_ref, o_ref, buf_ref, sem):
    i = pl.program_id(0)
    slot = lax.rem(i, 2)

    def start_gather(ids_ref, s):
        def body(t, carry):
            for k in range(TOP_K):
                _row_copy(y_hbm, ids_ref[0, 0, TOP_K * t + k], buf_ref.at[s, k], t, sem.at[s]).start()
            return carry
        lax.fori_loop(0, COMBINE_TILE, body, 0, unroll=4)

    @pl.when(i == 0)
    def _():
        start_gather(pos_ref, 0)

    @pl.when(i + 1 < pl.num_programs(0))
    def _():
        start_gather(pos_next_ref, 1 - slot)

    def wait(t, carry):
        for k in range(TOP_K):
            _row_copy(y_hbm, 0, buf_ref.at[slot, k], t, sem.at[slot]).wait()
        return carry
    lax.fori_loop(0, COMBINE_TILE, wait, 0, unroll=4)

    gate = gate_ref[...]
    y = gate[:, 0:1] * buf_ref[slot, 0] + gate[:, 1:2] * buf_ref[slot, 1]
    o_ref[...] = _layer_norm(ALPHA * x_ref[...] + y, g_ref[...], b_ref[...])


def _combine_ln(pos_blocks, y_pad, x, gate, g, b):
    n = x.shape[0]
    tm = COMBINE_TILE
    nt = n // tm
    vec = pl.BlockSpec((1, D_MODEL), lambda i: (0, 0))
    ids = lambda f: pl.BlockSpec((1, 1, TOP_K * tm), f, memory_space=pltpu.SMEM)
    return pl.pallas_call(
        _combine_body,
        grid=(nt,),
        in_specs=[ids(lambda i: (i, 0, 0)), ids(lambda i: (jnp.minimum(i + 1, nt - 1), 0, 0)),
                  pl.BlockSpec(memory_space=pl.ANY),
                  pl.BlockSpec((tm, D_MODEL), lambda i: (i, 0)),
                  pl.BlockSpec((tm, TOP_K), lambda i: (i, 0)), vec, vec],
        out_specs=pl.BlockSpec((tm, D_MODEL), lambda i: (i, 0)),
        out_shape=jax.ShapeDtypeStruct((n, D_MODEL), F32),
        scratch_shapes=[pltpu.VMEM((2, TOP_K, tm, D_MODEL), F32), pltpu.SemaphoreType.DMA((2,))],
        compiler_params=_params(("arbitrary",)),
        name="moe_combine_ln",
    )(pos_blocks, pos_blocks, y_pad, x, gate, g, b)


def _grouped_moe_ln(x, rw_t, rb_col, w1, w3, w2, layer, g, b):
    n = x.shape[0]
    eidx_t, gate_t, rank_t, counts = _router(x, rw_t, rb_col)
    padded = (counts[:, 0] + MOE_BLOCK - 1) // MOE_BLOCK * MOE_BLOCK
    ends = jnp.cumsum(padded)
    nb = -(-n * TOP_K // MOE_BLOCK) + N_EXPERTS
    block_start = jnp.arange(nb, dtype=I32) * MOE_BLOCK
    block_expert = jnp.minimum(jnp.sum(ends[None, :] <= block_start[:, None], axis=1), N_EXPERTS - 1).astype(I32)
    n_used = (ends[N_EXPERTS - 1:] // MOE_BLOCK).astype(I32)
    pos_t = _slot_rows(eidx_t, rank_t, (ends - padded).astype(I32).reshape(N_EXPERTS, 1))
    pos = pos_t.T
    tok = jnp.broadcast_to(jnp.arange(n, dtype=I32)[:, None], (n, TOP_K))
    tok_pad = jnp.zeros((nb * MOE_BLOCK,), I32).at[pos.reshape(-1)].set(tok.reshape(-1), unique_indices=True)
    y_pad = _experts(block_expert, n_used, tok_pad.reshape(nb, 1, MOE_BLOCK), x, w1, w3, w2, layer)
    pos_blocks = pos.reshape(n // COMBINE_TILE, 1, TOP_K * COMBINE_TILE)
    return _combine_ln(pos_blocks, y_pad, x, gate_t.T, g, b)


def kernel(x_prompt, x_sample, cache_fox_k, cache_fox_v, cache_fox_logf, state_rwkv, state_rwkv_shift, w_in, rwkv_mu, rwkv_w0, rwkv_w2, rwkv_a0, rwkv_a2, rwkv_g2, rwkv_k_k, rwkv_k_a, rwkv_r_k, rwkv_lnx_w, rwkv_lnx_b, fox_b_f, fox_q_g, fox_k_g, w_out, ln1_g, ln1_b, ln2_g, ln2_b, router_w, router_b, moe_w1, moe_w3, moe_w2):
    nb_p, seq, _ = x_prompt.shape
    nb_s, dec, _ = x_sample.shape
    depth = w_in.shape[0]
    past = cache_fox_k.shape[2]
    n_p, n_s = nb_p * seq, nb_s * dec
    n = n_p + n_s

    x = jnp.concatenate([x_prompt.reshape(n_p, D_MODEL), x_sample.reshape(n_s, D_MODEL)], axis=0)
    logf_rows = cache_fox_logf.transpose(0, 1, 3, 2)
    cache_k = cache_fox_k.transpose(0, 1, 3, 4, 2)
    cache_v = cache_fox_v.transpose(0, 1, 3, 4, 2)
    fox0 = RWKV_COLS
    fl0 = fox0 + 3 * D_F
    w_in_b = jnp.concatenate(
        [w_in[:, :, :fl0], w_in[:, :, fl0 + N_HEADS:], w_in[:, :, fl0:fl0 + N_HEADS],
         jnp.zeros((depth, D_MODEL, FL_PAD - N_HEADS), F32)], axis=-1).astype(BF16)
    w_out_b = w_out.astype(BF16)
    rw_t = router_w.T
    rb_col = router_b.reshape(N_EXPERTS, 1)
    head_of = jnp.arange(D_F, dtype=I32) // HEAD_DIM
    ones_bd = (head_of[:, None] == head_of[None, :]).astype(BF16)
    slot_consts = _slot_constants()
    zero_shift = jnp.zeros((nb_p, 1, RWKV_COLS), F32)
    zero_state = jnp.zeros((nb_p, N_HEADS, HEAD_DIM, HEAD_DIM), F32)
    row = lambda v: v.reshape(1, -1)

    outs = {k: [] for k in ('pk', 'pv', 'pl', 'pr', 'ps', 'sk', 'sv', 'sl', 'sr', 'ss')}
    for l in range(depth):
        lp = dict(mu=row(rwkv_mu[l]), w0=row(rwkv_w0[l]), w2=rwkv_w2[l], a0=row(rwkv_a0[l]), a2=rwkv_a2[l],
                  g2=rwkv_g2[l], k_k=row(rwkv_k_k[l]), k_a=row(rwkv_k_a[l]), r_k=row(rwkv_r_k[l]),
                  lnx_w=row(rwkv_lnx_w[l]), lnx_b=row(rwkv_lnx_b[l]), ones_bd=ones_bd)
        pr, q, k, v, og, fl = _in_proj(x, w_in_b[l])

        ry_p, st_p = _rwkv(pr, zero_shift, zero_state, lp, nb_p, seq, 0)
        ry_s, st_s = _rwkv(pr, state_rwkv_shift[l], state_rwkv[l], lp, nb_s, dec, n_p)

        b_f = jnp.concatenate([fox_b_f[l], jnp.zeros((FL_PAD - N_HEADS,), F32)]).reshape(1, FL_PAD)
        lf_p, c_p = _logf_cumsum(fl, b_f, nb_p, seq, 0)
        lf_s, c_s = _logf_cumsum(fl, b_f, nb_s, dec, n_p)
        qa, kn, ka, vb = _fox_prep(q, k, v, jnp.concatenate([c_p, c_s], axis=0), row(jnp.tile(fox_q_g[l], N_HEADS)),
                                   row(jnp.tile(fox_k_g[l], N_HEADS)), ones_bd, slot_consts)
        fo_p = _attn_prompt(qa, ka, vb, nb_p, seq)
        fo_s = _attn_sample(qa, ka, vb, cache_k, cache_v, logf_rows, l, nb_s, dec, n_p)

        x1 = _out_proj_ln(ry_p, ry_s, fo_p, fo_s, og, x, w_out_b[l], row(ln1_g[l]), row(ln1_b[l]))
        x = _grouped_moe_ln(x1, rw_t, rb_col, moe_w1, moe_w3, moe_w2, l, row(ln2_g[l]), row(ln2_b[l]))

        outs['pk'].append(kn[:n_p].reshape(nb_p, seq, N_HEADS, HEAD_DIM))
        outs['pv'].append(v[:n_p].reshape(nb_p, seq, N_HEADS, HEAD_DIM))
        outs['pl'].append(lf_p[:, :N_HEADS].reshape(nb_p, seq, N_HEADS))
        outs['pr'].append(st_p)
        outs['ps'].append(pr[:n_p].reshape(nb_p, seq, RWKV_COLS)[:, seq - 1:])
        outs['sk'].append(kn[n_p:].reshape(nb_s, dec, N_HEADS, HEAD_DIM))
        outs['sv'].append(v[n_p:].reshape(nb_s, dec, N_HEADS, HEAD_DIM))
        outs['sl'].append(lf_s[:, :N_HEADS].reshape(nb_s, dec, N_HEADS))
        outs['sr'].append(st_s)
        outs['ss'].append(pr[n_p:].reshape(nb_s, dec, RWKV_COLS)[:, dec - 1:])

    stk = lambda key: jnp.stack(outs[key], axis=0)
    return (x[:n_p].reshape(nb_p, seq, D_MODEL), x[n_p:].reshape(nb_s, dec, D_MODEL),
            stk('pk'), stk('pv'), stk('pl'), stk('pr'), stk('ps'),
            stk('sk'), stk('sv'), stk('sl'), stk('sr'), stk('ss'))
```

```python
import functools

import jax
import jax.numpy as jnp
from jax import lax
from jax.experimental import pallas as pl
from jax.experimental.pallas import tpu as pltpu

F32 = jnp.float32
BF16 = jnp.bfloat16
I32 = jnp.int32

D_MODEL = 1024
HEAD_DIM = 64
N_HEADS = 8
D_R = N_HEADS * HEAD_DIM
D_F = N_HEADS * HEAD_DIM
DECAY_LORA = 64
A_LORA = 64
G_LORA = 128
RWKV_COLS = 3 * D_R + DECAY_LORA + A_LORA + G_LORA
FL_PAD = 128
IN_COLS_PAD = RWKV_COLS + 4 * D_F + FL_PAD
DEPTH = 2
N_EXPERTS = 32
N_GROUPS = 4
EXPERTS_PER_GROUP = N_EXPERTS // N_GROUPS
TOP_K = 2
D_EXPERT = D_MODEL // 2
MOE_BLOCK = 256
ALPHA = (2 * DEPTH) ** 0.25
LN_EPS = 1e-5
GN_EPS = 64e-5
QK_EPS = 1e-6
SCALE = HEAD_DIM ** -0.5
RWKV_CHUNK = 64
INV_BASE = 16
VMEM_LIMIT = 48 * 1024 * 1024

_NN = (((1,), (0,)), ((), ()))
_NT = (((1,), (1,)), ((), ()))
_TN = (((0,), (0,)), ((), ()))


def _dot(a, b, dims=_NN):
    return lax.dot_general(a, b, dims, preferred_element_type=F32)


def _split2(x):
    hi = x.astype(BF16)
    lo = (x - hi.astype(F32)).astype(BF16)
    return hi, lo


def _split3(x):
    hi = x.astype(BF16)
    r = x - hi.astype(F32)
    mid = r.astype(BF16)
    lo = (r - mid.astype(F32)).astype(BF16)
    return hi, mid, lo


def _dot3(a, b, dims=_NN):
    ah, al = _split2(a)
    bh, bl = _split2(b)
    return _dot(ah, bh, dims) + (_dot(ah, bl, dims) + _dot(al, bh, dims))


def _dot_exact_lhs(a_bf16, x, dims=_NN):
    hi, mid, lo = _split3(x)
    return _dot(a_bf16, hi, dims) + (_dot(a_bf16, mid, dims) + _dot(a_bf16, lo, dims))


def _sigmoid(x):
    return 1.0 / (1.0 + jnp.exp(-x))


def _softplus(x):
    return jnp.maximum(x, 0.0) + jnp.log(1.0 + jnp.exp(-jnp.abs(x)))


def _layer_norm(z, g, b):
    mu = jnp.mean(z, axis=-1, keepdims=True)
    zc = z - mu
    var = jnp.mean(zc * zc, axis=-1, keepdims=True)
    return zc * lax.rsqrt(var + LN_EPS) * g + b


def _params(sem):
    return pltpu.CompilerParams(dimension_semantics=sem, vmem_limit_bytes=VMEM_LIMIT)


_IN_SPLITS = (RWKV_COLS, D_F, D_F, D_F, D_F, FL_PAD)


def _in_proj_body(x_ref, w_ref, *out_refs):
    x = x_ref[...].astype(BF16)
    col = 0
    for ref, width in zip(out_refs, _IN_SPLITS):
        for c0 in range(0, width, 512):
            c1 = min(c0 + 512, width)
            ref[:, c0:c1] = _dot(x, w_ref[:, col + c0:col + c1])
        col += width


def _in_proj(x, w):
    n = x.shape[0]
    tm = 256
    return pl.pallas_call(
        _in_proj_body,
        grid=(n // tm,),
        in_specs=[pl.BlockSpec((tm, D_MODEL), lambda i: (i, 0)),
                  pl.BlockSpec((D_MODEL, IN_COLS_PAD), lambda i: (0, 0))],
        out_specs=[pl.BlockSpec((tm, wd), lambda i: (i, 0)) for wd in _IN_SPLITS],
        out_shape=[jax.ShapeDtypeStruct((n, wd), F32) for wd in _IN_SPLITS],
        compiler_params=_params(("parallel",)),
        name="in_proj",
    )(x, w)


GROUP = 4
GROUP_W = GROUP * HEAD_DIM
RWKV_SUB = 4


def _block_diag(x, keep):
    return jnp.where(keep, jnp.concatenate([x] * GROUP, axis=0), jnp.zeros((), x.dtype))


def _heads_mm(a, b, keep, dims=_NN, exact=True):
    if not exact:
        return _dot(a.astype(BF16), _block_diag(b.astype(BF16), keep), dims)
    ah, al = _split2(a)
    bh, bl = _split2(b)
    dh = _block_diag(bh, keep)
    dl = _block_diag(bl, keep)
    return _dot(ah, dh, dims) + (_dot(ah, dl, dims) + _dot(al, dh, dims))


def _fold_heads(f, keep):
    f = jnp.where(keep, f, 0.0)
    return (f[0:HEAD_DIM] + f[HEAD_DIM:2 * HEAD_DIM]) + (f[2 * HEAD_DIM:3 * HEAD_DIM] + f[3 * HEAD_DIM:4 * HEAD_DIM])


def _unit_lower_inverses(ls, c_len, t_row, t_col, keep_tt):
    shift = INV_BASE.bit_length() - 1
    same = (t_row >> shift) == (t_col >> shift)
    eye = jnp.where(t_row == t_col, 1.0, 0.0)
    p = [jnp.where(same, -l, 0.0) for l in ls]
    x = [eye + n for n in p]
    for _ in range(shift - 1):
        p = [_heads_mm(pi, pi, keep_tt) for pi in p]
        x = [xi + _heads_mm(xi, pi, keep_tt) for xi, pi in zip(x, p)]
    size = 2 * INV_BASE
    while size <= c_len:
        s_hi = size.bit_length() - 1
        off = ((t_row >> s_hi) == (t_col >> s_hi)) & ((t_row >> (s_hi - 1)) != (t_col >> (s_hi - 1)))
        xq = [_heads_mm(xi, jnp.where(off, l, 0.0), keep_tt) for xi, l in zip(x, ls)]
        x = [xi - _heads_mm(xqi, xi, keep_tt) for xi, xqi in zip(x, xq)]
        size *= 2
    return x


def _rwkv_body(c_len, n_sub, n_steps, pr_ref, sp_ref, s0_ref, mu_ref, w0_ref, w2_ref, a0_ref, a2_ref, g2_ref,
               kk_ref, ka_ref, rk_ref, lnw_ref, lnb_ref, ones_ref, out_ref, sout_ref, shift_ref, carry_ref, s_ref):
    ci = pl.program_id(1)
    rows = c_len * n_sub
    n_groups = N_HEADS // GROUP

    @pl.when(ci == 0)
    def _():
        carry_ref[...] = sp_ref[0]
        for gi in range(n_groups):
            s_ref[gi] = jnp.concatenate([s0_ref[0, GROUP * gi + j] for j in range(GROUP)], axis=-1)

    pr = pr_ref[...]
    trow = lax.broadcasted_iota(I32, (rows, 1), 0)
    prev = jnp.where(trow == 0, carry_ref[...], pltpu.roll(pr, 1, 0))
    carry_ref[...] = pr[rows - 1:rows, :]
    xs = pr + (prev - pr) * mu_ref[...]
    xr = xs[:, 0:D_R]
    xk = xs[:, D_R:2 * D_R]
    xv = xs[:, 2 * D_R:3 * D_R]
    o = 3 * D_R
    xw = xs[:, o:o + DECAY_LORA]
    xa = xs[:, o + DECAY_LORA:o + DECAY_LORA + A_LORA]
    xg = xs[:, o + DECAY_LORA + A_LORA:RWKV_COLS]

    z = w0_ref[...] + _dot3(jnp.tanh(xw), w2_ref[...])
    lw = -jnp.exp(-_softplus(-z) - 0.5)
    a = _sigmoid(a0_ref[...] + _dot3(xa, a2_ref[...]))
    g = _dot3(_sigmoid(xg), g2_ref[...])
    kk_raw = xk * kk_ref[...]
    k_mod = xk * (1.0 + (a - 1.0) * ka_ref[...])

    shift = c_len.bit_length() - 1
    r2 = lax.broadcasted_iota(I32, (rows, rows), 0)
    c2 = lax.broadcasted_iota(I32, (rows, rows), 1)
    within = ((r2 >> shift) == (c2 >> shift)) & (r2 >= c2)
    cl = _dot_exact_lhs(jnp.where(within, 1.0, 0.0).astype(BF16), lw)

    ones = ones_ref[...]

    def head_sum(x):
        hi, lo = _split2(x)
        return _dot(hi, ones) + _dot(lo, ones)

    kk = kk_raw / jnp.maximum(jnp.sqrt(head_sum(kk_raw * kk_raw)), 1e-12)
    eg = jnp.exp(cl)
    e_inv = jnp.exp(-cl)
    r_dec = xr * eg
    kk_dec = kk * jnp.exp(cl - lw)
    b_und = kk * a * e_inv
    k_und = k_mod * e_inv

    wt = GROUP * c_len
    t_row = lax.broadcasted_iota(I32, (c_len, wt), 0)
    t_col = lax.broadcasted_iota(I32, (c_len, wt), 1) & (c_len - 1)
    strict = t_row > t_col
    incl = t_row >= t_col
    hd_shift = HEAD_DIM.bit_length() - 1
    keep_tt = (lax.broadcasted_iota(I32, (wt, wt), 0) >> shift) == (lax.broadcasted_iota(I32, (wt, wt), 1) >> shift)
    keep_tf = (lax.broadcasted_iota(I32, (wt, GROUP_W), 0) >> shift) == (
        lax.broadcasted_iota(I32, (wt, GROUP_W), 1) >> hd_shift)
    keep_ff = (lax.broadcasted_iota(I32, (GROUP_W, GROUP_W), 0) >> hd_shift) == (
        lax.broadcasted_iota(I32, (GROUP_W, GROUP_W), 1) >> hd_shift)

    chains = [(c, gi) for c in range(n_sub) for gi in range(n_groups)]
    cut = lambda x, c, gi: x[c * c_len:(c + 1) * c_len, gi * GROUP_W:(gi + 1) * GROUP_W]
    lhs = [jnp.concatenate([cut(kk_dec, c, gi), cut(r_dec, c, gi)], axis=0) for c, gi in chains]
    bu = [cut(b_und, c, gi) for c, gi in chains]
    ku = [cut(k_und, c, gi) for c, gi in chains]
    vh = [cut(xv, c, gi) for c, gi in chains]
    n_ch = range(len(chains))
    gb = [_heads_mm(lhs[i], bu[i], keep_tf, _NT) for i in n_ch]
    gk = [_heads_mm(lhs[i], ku[i], keep_tf, _NT, exact=False) for i in n_ch]
    l_b = [jnp.where(strict, m[:c_len], 0.0) for m in gb]
    m_rb = [jnp.where(incl, m[c_len:], 0.0) for m in gb]
    l_k = [jnp.where(strict, m[:c_len], 0.0) for m in gk]
    m_rk = [jnp.where(incl, m[c_len:], 0.0) for m in gk]
    t_inv = _unit_lower_inverses(l_b, c_len, t_row, t_col, keep_tt)
    lkv = [_heads_mm(l_k[i], vh[i], keep_tf, exact=False) for i in n_ch]
    mkv = [_heads_mm(m_rk[i], vh[i], keep_tf, exact=False) for i in n_ch]
    fv = [_fold_heads(_dot(vh[i].astype(BF16), ku[i].astype(BF16), _TN), keep_ff) for i in n_ch]

    state = [s_ref[gi] for gi in range(n_groups)]
    y_rows = []
    for c in range(n_sub):
        ids = [c * n_groups + gi for gi in range(n_groups)]
        ps = [_heads_mm(lhs[i], state[gi], keep_ff, _NT, exact=False) for gi, i in enumerate(ids)]
        u = [-_heads_mm(t_inv[i], ps[gi][:c_len] + lkv[i], keep_tf, exact=False) for gi, i in enumerate(ids)]
        y = [ps[gi][c_len:] + mkv[i] + _heads_mm(m_rb[i], u[gi], keep_tf, exact=False) for gi, i in enumerate(ids)]
        fu = [_fold_heads(_dot(u[gi].astype(BF16), bu[i].astype(BF16), _TN), keep_ff) for gi, i in enumerate(ids)]
        last = c * c_len + c_len - 1
        state = [(state[gi] + fv[i] + fu[gi]) * eg[last:last + 1, gi * GROUP_W:(gi + 1) * GROUP_W]
                 for gi, i in enumerate(ids)]
        y_rows.append(jnp.concatenate(y, axis=-1))
    for gi in range(n_groups):
        s_ref[gi] = state[gi]

    y = jnp.concatenate(y_rows, axis=0)
    yc = y - head_sum(y) * (1.0 / HEAD_DIM)
    var = head_sum(yc * yc) * (1.0 / HEAD_DIM)
    yn = yc * lax.rsqrt(var + GN_EPS) * lnw_ref[...] + lnb_ref[...]
    bonus = head_sum(xr * k_mod * rk_ref[...]) * xv
    out_ref[...] = (yn + bonus) * g

    @pl.when(ci == n_steps - 1)
    def _():
        for gi in range(n_groups):
            for j in range(GROUP):
                sout_ref[0, GROUP * gi + j] = state[gi][:, j * HEAD_DIM:(j + 1) * HEAD_DIM]
        shift_ref[0] = pr[rows - 1:rows, :]


def _rwkv(pr, shift_prev, s0, lp, n_seq, seq_len, row0):
    c_len = min(RWKV_CHUNK, seq_len)
    n_sub = min(RWKV_SUB, seq_len // c_len)
    rows = c_len * n_sub
    n_steps = seq_len // rows
    blk0 = row0 // rows
    vec = lambda wd: pl.BlockSpec((1, wd), lambda b, c: (0, 0))
    mat = lambda r, wd: pl.BlockSpec((r, wd), lambda b, c: (0, 0))
    in_specs = [
        pl.BlockSpec((rows, RWKV_COLS), lambda b, c: (blk0 + b * n_steps + c, 0)),
        pl.BlockSpec((1, 1, RWKV_COLS), lambda b, c: (b, 0, 0)),
        pl.BlockSpec((1, N_HEADS, HEAD_DIM, HEAD_DIM), lambda b, c: (b, 0, 0, 0)),
        vec(RWKV_COLS), vec(D_R), mat(DECAY_LORA, D_R), vec(D_R), mat(A_LORA, D_R), mat(G_LORA, D_R),
        vec(D_R), vec(D_R), vec(D_R), vec(D_R), vec(D_R), mat(D_R, D_R),
    ]
    args = [pr, shift_prev, s0, lp['mu'], lp['w0'], lp['w2'], lp['a0'], lp['a2'], lp['g2'],
            lp['k_k'], lp['k_a'], lp['r_k'], lp['lnx_w'], lp['lnx_b'], lp['ones_bd']]
    return pl.pallas_call(
        functools.partial(_rwkv_body, c_len, n_sub, n_steps),
        grid=(n_seq, n_steps),
        in_specs=in_specs,
        out_specs=[pl.BlockSpec((rows, D_R), lambda b, c: (b * n_steps + c, 0)),
                   pl.BlockSpec((1, N_HEADS, HEAD_DIM, HEAD_DIM), lambda b, c: (b, 0, 0, 0)),
                   pl.BlockSpec((1, 1, RWKV_COLS), lambda b, c: (b, 0, 0))],
        out_shape=[jax.ShapeDtypeStruct((n_seq * seq_len, D_R), F32),
                   jax.ShapeDtypeStruct((n_seq, N_HEADS, HEAD_DIM, HEAD_DIM), F32),
                   jax.ShapeDtypeStruct((n_seq, 1, RWKV_COLS), F32)],
        scratch_shapes=[pltpu.VMEM((1, RWKV_COLS), F32), pltpu.VMEM((N_HEADS // GROUP, HEAD_DIM, GROUP_W), F32)],
        compiler_params=_params(("arbitrary", "arbitrary")),
        name="rwkv_mixer",
    )(*args)


SLOT = 2 * HEAD_DIM
C_LANE = HEAD_DIM


def _fox_prep_body(q_ref, k_ref, v_ref, c_ref, qg_ref, kg_ref, ones_ref, place_ref, pcq_ref, pck_ref, oneq_ref,
                   onek_ref, qa_ref, kn_ref, ka_ref, vb_ref):
    ones = ones_ref[...]

    def rms(x, gain):
        hi, lo = _split2(x * x)
        ss = _dot(hi, ones) + _dot(lo, ones)
        return x * lax.rsqrt(ss * (1.0 / HEAD_DIM) + QK_EPS) * gain

    qn = rms(q_ref[...], qg_ref[...]) * SCALE
    kn = rms(k_ref[...], kg_ref[...])
    kn_ref[...] = kn
    vb_ref[...] = v_ref[...].astype(BF16)
    c_parts = _split3(c_ref[...])

    def slots(xb, pc_ref, one_ref):
        acc = _dot(xb, place_ref[...]) + one_ref[...]
        for j in range(3):
            acc = acc + _dot(c_parts[j], pc_ref[j])
        return acc.astype(BF16)

    qa_ref[...] = slots(qn.astype(BF16), pcq_ref, oneq_ref)
    ka_ref[...] = slots(kn.astype(BF16), pck_ref, onek_ref)


def _slot_constants():
    d = jnp.arange(D_F, dtype=I32)
    lane = jnp.arange(N_HEADS * SLOT, dtype=I32)
    place = (lane[None, :] == (d // HEAD_DIM * SLOT + d % HEAD_DIM)[:, None]).astype(BF16)
    h = jnp.arange(FL_PAD, dtype=I32)[None, :, None]
    j = jnp.arange(3, dtype=I32)[:, None, None]
    is_head = h < N_HEADS
    pcq = ((lane[None, None, :] == h * SLOT + C_LANE + j) & is_head).astype(BF16)
    pck = -((lane[None, None, :] == h * SLOT + C_LANE + 3 + j) & is_head).astype(BF16)
    in_slot = lane % SLOT
    oneq = ((in_slot >= C_LANE + 3) & (in_slot < C_LANE + 6)).astype(F32).reshape(1, -1)
    onek = ((in_slot >= C_LANE) & (in_slot < C_LANE + 3)).astype(F32).reshape(1, -1)
    return place, pcq, pck, oneq, onek


def _fox_prep(q, k, v, c, q_gain, k_gain, ones_bd, slot_consts):
    n = q.shape[0]
    tm = 256
    wide = N_HEADS * SLOT
    row = lambda wd: pl.BlockSpec((tm, wd), lambda i: (i, 0))
    vec = lambda wd: pl.BlockSpec((1, wd), lambda i: (0, 0))
    full = lambda *shape: pl.BlockSpec(shape, lambda i: (0,) * len(shape))
    return pl.pallas_call(
        _fox_prep_body,
        grid=(n // tm,),
        in_specs=[row(D_F), row(D_F), row(D_F), row(FL_PAD), vec(D_F), vec(D_F), full(D_F, D_F),
                  full(D_F, wide), full(3, FL_PAD, wide), full(3, FL_PAD, wide), vec(wide), vec(wide)],
        out_specs=[row(wide), row(D_F), row(wide), row(D_F)],
        out_shape=[jax.ShapeDtypeStruct((n, wide), BF16), jax.ShapeDtypeStruct((n, D_F), F32),
                   jax.ShapeDtypeStruct((n, wide), BF16), jax.ShapeDtypeStruct((n, D_F), BF16)],
        compiler_params=_params(("parallel",)),
        name="fox_prep",
    )(q, k, v, c, q_gain, k_gain, ones_bd, *slot_consts)


def _logf_cumsum_body(fl_ref, bf_ref, lf_ref, c_ref, carry_ref):
    @pl.when(pl.program_id(1) == 0)
    def _():
        carry_ref[...] = jnp.zeros_like(carry_ref)

    lf = -_softplus(-(fl_ref[...] + bf_ref[...]))
    lf_ref[...] = lf
    t = lf.shape[0]
    row = lax.broadcasted_iota(I32, (t, t), 0)
    col = lax.broadcasted_iota(I32, (t, t), 1)
    cs = _dot_exact_lhs(jnp.where(row >= col, 1.0, 0.0).astype(BF16), lf) + carry_ref[...]
    c_ref[...] = cs
    carry_ref[...] = cs[t - 1:t, :]


def _logf_cumsum(fl, b_f, n_seq, seq_len, row0):
    tc = min(seq_len, 256)
    nt = seq_len // tc
    blk0 = row0 // tc
    out = pl.BlockSpec((tc, FL_PAD), lambda b, j: (b * nt + j, 0))
    return pl.pallas_call(
        _logf_cumsum_body,
        grid=(n_seq, nt),
        in_specs=[pl.BlockSpec((tc, FL_PAD), lambda b, j: (blk0 + b * nt + j, 0)),
                  pl.BlockSpec((1, FL_PAD), lambda b, j: (0, 0))],
        out_specs=[out, out],
        out_shape=[jax.ShapeDtypeStruct((n_seq * seq_len, FL_PAD), F32)] * 2,
        scratch_shapes=[pltpu.VMEM((1, FL_PAD), F32)],
        compiler_params=_params(("arbitrary", "arbitrary")),
        name="logf_cumsum",
    )(fl, b_f)


ATT_TILE = 512


def _attn_prompt_body(q_ref, k_ref, v_ref, o_ref):
    i = pl.program_id(2)
    t = ATT_TILE
    row = lax.broadcasted_iota(I32, (t, t), 0)
    col = lax.broadcasted_iota(I32, (t, t), 1)
    causal = row >= col
    pair = range(2)
    q = [q_ref[:, hh * SLOT:(hh + 1) * SLOT] for hh in pair]

    def tile(j, carry, masked):
        m, l, acc = carry
        j0 = pl.multiple_of(j * t, t)
        s = [_dot(q[hh], k_ref[pl.ds(j0, t), hh * SLOT:(hh + 1) * SLOT], _NT) for hh in pair]
        if masked:
            s = [jnp.where(causal, sh, -jnp.inf) for sh in s]
        m_new = [jnp.maximum(m[hh], jnp.max(s[hh], axis=-1, keepdims=True)) for hh in pair]
        alpha = [jnp.exp(m[hh] - m_new[hh]) for hh in pair]
        p = [jnp.exp(s[hh] - m_new[hh]) for hh in pair]
        l = [alpha[hh] * l[hh] + jnp.sum(p[hh], axis=-1, keepdims=True) for hh in pair]
        pv = [_dot(p[hh].astype(BF16), v_ref[pl.ds(j0, t), hh * HEAD_DIM:(hh + 1) * HEAD_DIM]) for hh in pair]
        acc = [alpha[hh] * acc[hh] + pv[hh] for hh in pair]
        return m_new, l, acc

    init = ([jnp.full((t, 1), -jnp.inf, F32)] * 2, [jnp.zeros((t, 1), F32)] * 2,
            [jnp.zeros((t, HEAD_DIM), F32)] * 2)
    carry = lax.fori_loop(0, i, lambda j, c: tile(j, c, False), init)
    _, l, acc = tile(i, carry, True)
    o_ref[...] = jnp.concatenate([acc[hh] / l[hh] for hh in pair], axis=-1)


def _attn_prompt(qa, ka, vb, n_seq, seq_len):
    t = ATT_TILE
    nq = seq_len // t
    return pl.pallas_call(
        _attn_prompt_body,
        grid=(n_seq, N_HEADS // 2, nq),
        in_specs=[pl.BlockSpec((t, 2 * SLOT), lambda b, p, i: (b * nq + i, p)),
                  pl.BlockSpec((seq_len, 2 * SLOT), lambda b, p, i: (b, p)),
                  pl.BlockSpec((seq_len, 2 * HEAD_DIM), lambda b, p, i: (b, p))],
        out_specs=pl.BlockSpec((t, 2 * HEAD_DIM), lambda b, p, i: (b * nq + i, p)),
        out_shape=jax.ShapeDtypeStruct((n_seq * seq_len, D_F), F32),
        compiler_params=_params(("parallel", "parallel", "arbitrary")),
        name="fox_attn_prompt",
    )(qa, ka, vb)


SAMPLE_CHUNK = 2048
TAIL_BLOCK = 1024


def _attn_sample_body(n_chunks, q_ref, kn_ref, vn_ref, kc_ref, vc_ref, lp_ref, after_ref, o_ref,
                      m_ref, l_ref, acc_ref, suffix_ref):
    j = pl.program_id(1)
    n = q_ref.shape[0]
    heads = range(N_HEADS)

    @pl.when(j == 0)
    def _():
        m_ref[...] = jnp.full(m_ref.shape, -jnp.inf, F32)
        l_ref[...] = jnp.zeros_like(l_ref)
        acc_ref[...] = jnp.zeros_like(acc_ref)
        suffix_ref[...] = jnp.zeros_like(suffix_ref)

    after = after_ref[...]
    tp = lp_ref.shape[3]
    tails = []
    suffix = suffix_ref[...]
    for b0 in range(tp - TAIL_BLOCK, -1, -TAIL_BLOCK):
        lp = lp_ref[0, 0, :, b0:b0 + TAIL_BLOCK]
        hi, mid, lo = _split3(lp)
        tails.insert(0, _dot(hi, after) + (_dot(mid, after) + _dot(lo, after)) + suffix)
        suffix = suffix + jnp.sum(lp, axis=-1, keepdims=True)
    suffix_ref[...] = suffix
    tail = jnp.concatenate(tails, axis=-1)

    q_slot = [q_ref[:, h * SLOT:(h + 1) * SLOT] for h in heads]
    c_col = [(qs[:, C_LANE:C_LANE + 1].astype(F32) + qs[:, C_LANE + 1:C_LANE + 2].astype(F32)
              + qs[:, C_LANE + 2:C_LANE + 3].astype(F32)) for qs in q_slot]
    s = [_dot(q_slot[h][:, :HEAD_DIM], kc_ref[0, 0, h].astype(BF16)) + c_col[h] + tail[h:h + 1, :] for h in heads]
    m_old = [m_ref[h] for h in heads]
    m_new = [jnp.maximum(m_old[h], jnp.max(s[h], axis=-1, keepdims=True)) for h in heads]
    alpha = [jnp.exp(m_old[h] - m_new[h]) for h in heads]
    p = [jnp.exp(s[h] - m_new[h]) for h in heads]
    pv = [_dot(p[h].astype(BF16), vc_ref[0, 0, h].astype(BF16), _NT) for h in heads]
    for h in heads:
        m_ref[h] = m_new[h]
        l_ref[h] = alpha[h] * l_ref[h] + jnp.sum(p[h], axis=-1, keepdims=True)
        acc_ref[h] = alpha[h] * acc_ref[h] + pv[h]

    @pl.when(j == n_chunks - 1)
    def _():
        row = lax.broadcasted_iota(I32, (n, n), 0)
        col = lax.broadcasted_iota(I32, (n, n), 1)
        s_new = [jnp.where(row >= col, _dot(q_slot[h], kn_ref[:, h * SLOT:(h + 1) * SLOT], _NT), -jnp.inf)
                 for h in heads]
        m_fin = [jnp.maximum(m_ref[h], jnp.max(s_new[h], axis=-1, keepdims=True)) for h in heads]
        a_fin = [jnp.exp(m_ref[h] - m_fin[h]) for h in heads]
        p_new = [jnp.exp(s_new[h] - m_fin[h]) for h in heads]
        l_fin = [a_fin[h] * l_ref[h] + jnp.sum(p_new[h], axis=-1, keepdims=True) for h in heads]
        acc = [a_fin[h] * acc_ref[h] + _dot(p_new[h].astype(BF16), vn_ref[:, h * HEAD_DIM:(h + 1) * HEAD_DIM])
               for h in heads]
        o_ref[...] = jnp.concatenate([acc[h] / l_fin[h] for h in heads], axis=-1)


def _attn_sample(qa, ka, vb, cache_k, cache_v, logf_rows, layer, n_seq, n_new, row0):
    past = cache_k.shape[4]
    tp = min(SAMPLE_CHUNK, past)
    n_chunks = past // tp
    blk0 = row0 // n_new
    frame = jnp.arange(TAIL_BLOCK, dtype=I32)
    after = (frame[:, None] > frame[None, :]).astype(BF16)
    rows = lambda wd: pl.BlockSpec((n_new, wd), lambda b, j: (blk0 + b, 0))
    cache = lambda: pl.BlockSpec((1, 1, N_HEADS, HEAD_DIM, tp), lambda b, j: (layer, b, 0, 0, n_chunks - 1 - j))
    return pl.pallas_call(
        functools.partial(_attn_sample_body, n_chunks),
        grid=(n_seq, n_chunks),
        in_specs=[rows(N_HEADS * SLOT), rows(N_HEADS * SLOT), rows(D_F), cache(), cache(),
                  pl.BlockSpec((1, 1, N_HEADS, tp), lambda b, j: (layer, b, 0, n_chunks - 1 - j)),
                  pl.BlockSpec((TAIL_BLOCK, TAIL_BLOCK), lambda b, j: (0, 0))],
        out_specs=pl.BlockSpec((n_new, D_F), lambda b, j: (b, 0)),
        out_shape=jax.ShapeDtypeStruct((n_seq * n_new, D_F), F32),
        scratch_shapes=[pltpu.VMEM((N_HEADS, n_new, 1), F32), pltpu.VMEM((N_HEADS, n_new, 1), F32),
                        pltpu.VMEM((N_HEADS, n_new, HEAD_DIM), F32), pltpu.VMEM((N_HEADS, 1), F32)],
        compiler_params=_params(("parallel", "arbitrary")),
        name="fox_attn_sample",
    )(qa, ka, vb, cache_k, cache_v, logf_rows, after)


LANES = 128
TOKEN_ROWS = D_MODEL // LANES


def _store_token_tiles(ref, x):
    m = x.shape[0]
    for c in range(TOKEN_ROWS):
        ref[pl.ds(c, m, stride=TOKEN_ROWS), :] = x[:, c * LANES:(c + 1) * LANES]


def _load_token_tiles(ref, m):
    return jnp.concatenate([ref[pl.ds(c, m, stride=TOKEN_ROWS), :] for c in range(TOKEN_ROWS)], axis=-1)


def _out_proj_body(tiles_p, ryp_ref, rys_ref, fop_ref, fos_ref, og_ref, x_ref, w_ref, g_ref, b_ref, o_ref, ot_ref):
    from_prompt = pl.program_id(0) < tiles_p
    ry = jnp.where(from_prompt, ryp_ref[...], rys_ref[...]).astype(BF16)
    fo = jnp.where(from_prompt, fop_ref[...], fos_ref[...])
    fy = (fo * _sigmoid(og_ref[...])).astype(BF16)
    m = _dot(ry, w_ref[0:D_R, :]) + _dot(fy, w_ref[D_R:D_R + D_F, :])
    out = _layer_norm(ALPHA * x_ref[...] + m, g_ref[...], b_ref[...])
    o_ref[...] = out
    _store_token_tiles(ot_ref, out)


def _out_proj_ln(ry_p, ry_s, fo_p, fo_s, og, x, w, g, b):
    n = x.shape[0]
    tm = 256
    tiles_p = ry_p.shape[0] // tm
    row = lambda wd: pl.BlockSpec((tm, wd), lambda i: (i, 0))
    row_p = lambda wd: pl.BlockSpec((tm, wd), lambda i: (jnp.minimum(i, tiles_p - 1), 0))
    row_s = lambda wd: pl.BlockSpec((tm, wd), lambda i: (jnp.maximum(i - tiles_p, 0), 0))
    vec = pl.BlockSpec((1, D_MODEL), lambda i: (0, 0))
    return pl.pallas_call(
        functools.partial(_out_proj_body, tiles_p),
        grid=(n // tm,),
        in_specs=[row_p(D_R), row_s(D_R), row_p(D_F), row_s(D_F), row(D_F), row(D_MODEL),
                  pl.BlockSpec((D_R + D_F, D_MODEL), lambda i: (0, 0)), vec, vec],
        out_specs=[row(D_MODEL), pl.BlockSpec((tm * TOKEN_ROWS, LANES), lambda i: (i, 0))],
        out_shape=[jax.ShapeDtypeStruct((n, D_MODEL), F32), jax.ShapeDtypeStruct((n * TOKEN_ROWS, LANES), F32)],
        compiler_params=_params(("parallel",)),
        name="out_proj_ln",
    )(ry_p, ry_s, fo_p, fo_s, og, x, w, g, b)


def _router_body(x_ref, rw_ref, rb_ref, earlier_ref, e_ref, g_ref, r_ref, count_ref, seen_ref):
    tn = x_ref.shape[0]
    scores = _sigmoid(_dot3(rw_ref[...], x_ref[...], _NT))
    sel = scores + rb_ref[...]
    sel4 = sel.reshape(N_GROUPS, EXPERTS_PER_GROUP, tn)
    sc4 = scores.reshape(N_GROUPS, EXPERTS_PER_GROUP, tn)
    lane_e = lax.broadcasted_iota(I32, (N_GROUPS, EXPERTS_PER_GROUP, tn), 1)

    def top2(vals, idx_iota, axis):
        m1 = jnp.max(vals, axis=axis, keepdims=True)
        i1 = jnp.min(jnp.where(vals == m1, idx_iota, EXPERTS_PER_GROUP), axis=axis, keepdims=True)
        rest = jnp.where(idx_iota == i1, -jnp.inf, vals)
        m2 = jnp.max(rest, axis=axis, keepdims=True)
        i2 = jnp.min(jnp.where(rest == m2, idx_iota, EXPERTS_PER_GROUP), axis=axis, keepdims=True)
        return m1, i1, m2, i2

    m1, _, m2, _ = top2(sel4, lane_e, 1)
    gsum = m1 + m2
    g_iota = lax.broadcasted_iota(I32, (N_GROUPS, 1, tn), 0)
    gmax = jnp.max(gsum, axis=0, keepdims=True)
    g_idx = jnp.min(jnp.where(gsum == gmax, g_iota, N_GROUPS), axis=0, keepdims=True)
    pick = g_iota == g_idx
    sel_g = jnp.max(jnp.where(pick, sel4, -jnp.inf), axis=0)
    sc_g = jnp.max(jnp.where(pick, sc4, -jnp.inf), axis=0)
    e_iota = lax.broadcasted_iota(I32, (EXPERTS_PER_GROUP, tn), 0)
    _, i1, _, i2 = top2(sel_g, e_iota, 0)
    gate1 = jnp.sum(jnp.where(e_iota == i1, sc_g, 0.0), axis=0, keepdims=True)
    gate2 = jnp.sum(jnp.where(e_iota == i2, sc_g, 0.0), axis=0, keepdims=True)
    tot = gate1 + gate2
    base = g_idx[0] * EXPERTS_PER_GROUP
    e1 = base + i1
    e2 = base + i2
    e_ref[...] = jnp.concatenate([e1, e2], axis=0)
    g_ref[...] = jnp.concatenate([gate1 / tot, gate2 / tot], axis=0)

    @pl.when(pl.program_id(0) == 0)
    def _():
        seen_ref[...] = jnp.zeros_like(seen_ref)

    all_e = lax.broadcasted_iota(I32, (N_EXPERTS, tn), 0)
    hit1 = all_e == e1
    hit2 = all_e == e2
    hits = jnp.where(hit1 | hit2, 1.0, 0.0)
    before = _dot(hits.astype(BF16), earlier_ref[...]) + seen_ref[...]
    r1 = jnp.sum(jnp.where(hit1, before, 0.0), axis=0, keepdims=True)
    r2 = jnp.sum(jnp.where(hit2, before, 0.0), axis=0, keepdims=True)
    r_ref[...] = jnp.concatenate([r1, r2], axis=0).astype(I32)
    seen_ref[...] = seen_ref[...] + jnp.sum(hits, axis=-1, keepdims=True)
    count_ref[...] = seen_ref[...].astype(I32)


def _router(x, rw_t, rb_col):
    n = x.shape[0]
    tn = 512
    tok = jnp.arange(tn, dtype=I32)
    earlier = (tok[:, None] < tok[None, :]).astype(BF16)
    pair = lambda: pl.BlockSpec((TOP_K, tn), lambda i: (0, i))
    return pl.pallas_call(
        _router_body,
        grid=(n // tn,),
        in_specs=[pl.BlockSpec((tn, D_MODEL), lambda i: (i, 0)),
                  pl.BlockSpec((N_EXPERTS, D_MODEL), lambda i: (0, 0)),
                  pl.BlockSpec((N_EXPERTS, 1), lambda i: (0, 0)),
                  pl.BlockSpec((tn, tn), lambda i: (0, 0))],
        out_specs=[pair(), pair(), pair(), pl.BlockSpec((N_EXPERTS, 1), lambda i: (0, 0))],
        out_shape=[jax.ShapeDtypeStruct((TOP_K, n), I32), jax.ShapeDtypeStruct((TOP_K, n), F32),
                   jax.ShapeDtypeStruct((TOP_K, n), I32), jax.ShapeDtypeStruct((N_EXPERTS, 1), I32)],
        scratch_shapes=[pltpu.VMEM((N_EXPERTS, 1), F32)],
        compiler_params=_params(("arbitrary",)),
        name="router",
    )(x, rw_t, rb_col, earlier)


def _slot_rows_body(e_ref, r_ref, start_ref, pos_ref):
    tn = e_ref.shape[1]
    all_e = lax.broadcasted_iota(I32, (N_EXPERTS, tn), 0)
    rows = [jnp.sum(jnp.where(all_e == e_ref[k:k + 1, :], start_ref[...], 0), axis=0, keepdims=True)
            for k in range(TOP_K)]
    pos_ref[...] = r_ref[...] + jnp.concatenate(rows, axis=0)


def _slot_rows(eidx_t, rank_t, expert_start):
    n = eidx_t.shape[1]
    tn = 512
    pair = lambda: pl.BlockSpec((TOP_K, tn), lambda i: (0, i))
    return pl.pallas_call(
        _slot_rows_body,
        grid=(n // tn,),
        in_specs=[pair(), pair(), pl.BlockSpec((N_EXPERTS, 1), lambda i: (0, 0))],
        out_specs=pair(),
        out_shape=jax.ShapeDtypeStruct((TOP_K, n), I32),
        compiler_params=_params(("parallel",)),
        name="moe_slot_rows",
    )(eidx_t, rank_t, expert_start)


def _row_copy(src_hbm, src_row8, dst, dst_token, sem):
    src = src_hbm.at[pl.ds(pl.multiple_of(src_row8, TOKEN_ROWS), TOKEN_ROWS)]
    return pltpu.make_async_copy(src, dst.at[pl.ds(pl.multiple_of(dst_token * TOKEN_ROWS, TOKEN_ROWS), TOKEN_ROWS)], sem)


def _expert_body(be_ref, used_ref, tok_ref, tok_next_ref, x_hbm, w1_ref, w3_ref, w2_ref, y_ref,
                 buf_ref, w1b_ref, w3b_ref, w2b_ref, sem):
    i = pl.program_id(0)
    n_used = used_ref[0]
    slot = lax.rem(i, 2)

    def start_gather(ids_ref, s):
        def body(r, carry):
            _row_copy(x_hbm, ids_ref[0, 0, r], buf_ref.at[s], r, sem.at[s]).start()
            return carry
        lax.fori_loop(0, MOE_BLOCK, body, 0, unroll=8)

    @pl.when(jnp.logical_and(i == 0, n_used > 0))
    def _():
        start_gather(tok_ref, 0)

    @pl.when(i + 1 < n_used)
    def _():
        start_gather(tok_next_ref, 1 - slot)

    @pl.when(i < n_used)
    def _():
        @pl.when(jnp.logical_or(i == 0, be_ref[i] != be_ref[jnp.maximum(i - 1, 0)]))
        def _():
            w1b_ref[...] = w1_ref[0, 0].astype(BF16)
            w3b_ref[...] = w3_ref[0, 0].astype(BF16)
            w2b_ref[...] = w2_ref[0, 0].astype(BF16)

        def wait(r, carry):
            _row_copy(x_hbm, 0, buf_ref.at[slot], r, sem.at[slot]).wait()
            return carry
        lax.fori_loop(0, MOE_BLOCK, wait, 0, unroll=8)

        h = _load_token_tiles(buf_ref.at[slot], MOE_BLOCK).astype(BF16)
        a = _dot(h, w1b_ref[...])
        b = _dot(h, w3b_ref[...])
        act = (a * _sigmoid(a) * b).astype(BF16)
        _store_token_tiles(y_ref, _dot(act, w2b_ref[...]))

    @pl.when(i >= n_used)
    def _():
        y_ref[...] = jnp.zeros_like(y_ref)


def _experts(block_expert, n_used, tok_blocks, x, w1, w3, w2, layer):
    nb = block_expert.shape[0]
    ids = lambda f: pl.BlockSpec((1, 1, MOE_BLOCK), f, memory_space=pltpu.SMEM)
    grid_spec = pltpu.PrefetchScalarGridSpec(
        num_scalar_prefetch=2,
        grid=(nb,),
        in_specs=[ids(lambda i, be, nu: (i, 0, 0)),
                  ids(lambda i, be, nu: (jnp.minimum(i + 1, nb - 1), 0, 0)),
                  pl.BlockSpec(memory_space=pl.ANY),
                  pl.BlockSpec((1, 1, D_MODEL, D_EXPERT), lambda i, be, nu: (layer, be[i], 0, 0)),
                  pl.BlockSpec((1, 1, D_MODEL, D_EXPERT), lambda i, be, nu: (layer, be[i], 0, 0)),
                  pl.BlockSpec((1, 1, D_EXPERT, D_MODEL), lambda i, be, nu: (layer, be[i], 0, 0))],
        out_specs=pl.BlockSpec((MOE_BLOCK * TOKEN_ROWS, LANES), lambda i, be, nu: (i, 0)),
        scratch_shapes=[pltpu.VMEM((2, MOE_BLOCK * TOKEN_ROWS, LANES), F32),
                        pltpu.VMEM((D_MODEL, D_EXPERT), BF16), pltpu.VMEM((D_MODEL, D_EXPERT), BF16),
                        pltpu.VMEM((D_EXPERT, D_MODEL), BF16), pltpu.SemaphoreType.DMA((2,))],
    )
    return pl.pallas_call(
        _expert_body,
        grid_spec=grid_spec,
        out_shape=jax.ShapeDtypeStruct((nb * MOE_BLOCK * TOKEN_ROWS, LANES), F32),
        compiler_params=_params(("arbitrary",)),
        name="moe_experts",
    )(block_expert, n_used, tok_blocks, tok_blocks, x, w1, w3, w2)


COMBINE_TILE = 128


def _combine_body(pos_ref, pos_next_ref, y_hbm, x_ref, gate_ref, g_ref, b_ref, o_ref, buf_ref, sem):
    i = pl.program_id(0)
    slot = lax.rem(i, 2)

    def start_gather(ids_ref, s):
        def body(t, carry):
            for k in range(TOP_K):
                _row_copy(y_hbm, ids_ref[0, 0, TOP_K * t + k], buf_ref.at[s, k], t, sem.at[s]).start()
            return carry
        lax.fori_loop(0, COMBINE_TILE, body, 0, unroll=4)

    @pl.when(i == 0)
    def _():
        start_gather(pos_ref, 0)

    @pl.when(i + 1 < pl.num_programs(0))
    def _():
        start_gather(pos_next_ref, 1 - slot)

    def wait(t, carry):
        for k in range(TOP_K):
            _row_copy(y_hbm, 0, buf_ref.at[slot, k], t, sem.at[slot]).wait()
        return carry
    lax.fori_loop(0, COMBINE_TILE, wait, 0, unroll=4)

    gate = gate_ref[...]
    y = (gate[:, 0:1] * _load_token_tiles(buf_ref.at[slot, 0], COMBINE_TILE)
         + gate[:, 1:2] * _load_token_tiles(buf_ref.at[slot, 1], COMBINE_TILE))
    o_ref[...] = _layer_norm(ALPHA * x_ref[...] + y, g_ref[...], b_ref[...])


def _combine_ln(pos_blocks, y_pad, x, gate, g, b):
    n = x.shape[0]
    tm = COMBINE_TILE
    nt = n // tm
    vec = pl.BlockSpec((1, D_MODEL), lambda i: (0, 0))
    ids = lambda f: pl.BlockSpec((1, 1, TOP_K * tm), f, memory_space=pltpu.SMEM)
    return pl.pallas_call(
        _combine_body,
        grid=(nt,),
        in_specs=[ids(lambda i: (i, 0, 0)), ids(lambda i: (jnp.minimum(i + 1, nt - 1), 0, 0)),
                  pl.BlockSpec(memory_space=pl.ANY),
                  pl.BlockSpec((tm, D_MODEL), lambda i: (i, 0)),
                  pl.BlockSpec((tm, TOP_K), lambda i: (i, 0)), vec, vec],
        out_specs=pl.BlockSpec((tm, D_MODEL), lambda i: (i, 0)),
        out_shape=jax.ShapeDtypeStruct((n, D_MODEL), F32),
        scratch_shapes=[pltpu.VMEM((2, TOP_K, tm * TOKEN_ROWS, LANES), F32), pltpu.SemaphoreType.DMA((2,))],
        compiler_params=_params(("arbitrary",)),
        name="moe_combine_ln",
    )(pos_blocks, pos_blocks, y_pad, x, gate, g, b)


def _grouped_moe_ln(x, x_tiles, rw_t, rb_col, w1, w3, w2, layer, g, b):
    n = x.shape[0]
    eidx_t, gate_t, rank_t, counts = _router(x, rw_t, rb_col)
    padded = (counts[:, 0] + MOE_BLOCK - 1) // MOE_BLOCK * MOE_BLOCK
    ends = jnp.cumsum(padded)
    nb = -(-n * TOP_K // MOE_BLOCK) + N_EXPERTS
    block_start = jnp.arange(nb, dtype=I32) * MOE_BLOCK
    block_expert = jnp.minimum(jnp.sum(ends[None, :] <= block_start[:, None], axis=1), N_EXPERTS - 1).astype(I32)
    n_used = (ends[N_EXPERTS - 1:] // MOE_BLOCK).astype(I32)
    pos_t = _slot_rows(eidx_t, rank_t, (ends - padded).astype(I32).reshape(N_EXPERTS, 1))
    pos = pos_t.T
    tok_row = jnp.broadcast_to(jnp.arange(n, dtype=I32)[:, None] * TOKEN_ROWS, (n, TOP_K))
    tok_pad = jnp.zeros((nb * MOE_BLOCK,), I32).at[pos.reshape(-1)].set(tok_row.reshape(-1), unique_indices=True)
    y_pad = _experts(block_expert, n_used, tok_pad.reshape(nb, 1, MOE_BLOCK), x_tiles, w1, w3, w2, layer)
    pos_blocks = (pos * TOKEN_ROWS).reshape(n // COMBINE_TILE, 1, TOP_K * COMBINE_TILE)
    return _combine_ln(pos_blocks, y_pad, x, gate_t.T, g, b)


def kernel(x_prompt, x_sample, cache_fox_k, cache_fox_v, cache_fox_logf, state_rwkv, state_rwkv_shift, w_in, rwkv_mu, rwkv_w0, rwkv_w2, rwkv_a0, rwkv_a2, rwkv_g2, rwkv_k_k, rwkv_k_a, rwkv_r_k, rwkv_lnx_w, rwkv_lnx_b, fox_b_f, fox_q_g, fox_k_g, w_out, ln1_g, ln1_b, ln2_g, ln2_b, router_w, router_b, moe_w1, moe_w3, moe_w2):
    nb_p, seq, _ = x_prompt.shape
    nb_s, dec, _ = x_sample.shape
    depth = w_in.shape[0]
    past = cache_fox_k.shape[2]
    n_p, n_s = nb_p * seq, nb_s * dec
    n = n_p + n_s

    x = jnp.concatenate([x_prompt.reshape(n_p, D_MODEL), x_sample.reshape(n_s, D_MODEL)], axis=0)
    logf_rows = cache_fox_logf.transpose(0, 1, 3, 2)
    cache_k = cache_fox_k.transpose(0, 1, 3, 4, 2)
    cache_v = cache_fox_v.transpose(0, 1, 3, 4, 2)
    fox0 = RWKV_COLS
    fl0 = fox0 + 3 * D_F
    w_in_b = jnp.concatenate(
        [w_in[:, :, :fl0], w_in[:, :, fl0 + N_HEADS:], w_in[:, :, fl0:fl0 + N_HEADS],
         jnp.zeros((depth, D_MODEL, FL_PAD - N_HEADS), F32)], axis=-1).astype(BF16)
    w_out_b = w_out.astype(BF16)
    rw_t = router_w.T
    rb_col = router_b.reshape(N_EXPERTS, 1)
    head_of = jnp.arange(D_F, dtype=I32) // HEAD_DIM
    ones_bd = (head_of[:, None] == head_of[None, :]).astype(BF16)
    slot_consts = _slot_constants()
    zero_shift = jnp.zeros((nb_p, 1, RWKV_COLS), F32)
    zero_state = jnp.zeros((nb_p, N_HEADS, HEAD_DIM, HEAD_DIM), F32)
    row = lambda v: v.reshape(1, -1)

    outs = {k: [] for k in ('pk', 'pv', 'pl', 'pr', 'ps', 'sk', 'sv', 'sl', 'sr', 'ss')}
    for l in range(depth):
        lp = dict(mu=row(rwkv_mu[l]), w0=row(rwkv_w0[l]), w2=rwkv_w2[l], a0=row(rwkv_a0[l]), a2=rwkv_a2[l],
                  g2=rwkv_g2[l], k_k=row(rwkv_k_k[l]), k_a=row(rwkv_k_a[l]), r_k=row(rwkv_r_k[l]),
                  lnx_w=row(rwkv_lnx_w[l]), lnx_b=row(rwkv_lnx_b[l]), ones_bd=ones_bd)
        pr, q, k, v, og, fl = _in_proj(x, w_in_b[l])

        ry_p, st_p, sh_p = _rwkv(pr, zero_shift, zero_state, lp, nb_p, seq, 0)
        ry_s, st_s, sh_s = _rwkv(pr, state_rwkv_shift[l], state_rwkv[l], lp, nb_s, dec, n_p)

        b_f = jnp.concatenate([fox_b_f[l], jnp.zeros((FL_PAD - N_HEADS,), F32)]).reshape(1, FL_PAD)
        lf_p, c_p = _logf_cumsum(fl, b_f, nb_p, seq, 0)
        lf_s, c_s = _logf_cumsum(fl, b_f, nb_s, dec, n_p)
        qa, kn, ka, vb = _fox_prep(q, k, v, jnp.concatenate([c_p, c_s], axis=0), row(jnp.tile(fox_q_g[l], N_HEADS)),
                                   row(jnp.tile(fox_k_g[l], N_HEADS)), ones_bd, slot_consts)
        fo_p = _attn_prompt(qa, ka, vb, nb_p, seq)
        fo_s = _attn_sample(qa, ka, vb, cache_k, cache_v, logf_rows, l, nb_s, dec, n_p)

        x1, x1_tiles = _out_proj_ln(ry_p, ry_s, fo_p, fo_s, og, x, w_out_b[l], row(ln1_g[l]), row(ln1_b[l]))
        x = _grouped_moe_ln(x1, x1_tiles, rw_t, rb_col, moe_w1, moe_w3, moe_w2, l, row(ln2_g[l]), row(ln2_b[l]))

        outs['pk'].append(kn[:n_p].reshape(nb_p, seq, N_HEADS, HEAD_DIM))
        outs['pv'].append(v[:n_p].reshape(nb_p, seq, N_HEADS, HEAD_DIM))
        outs['pl'].append(lf_p[:, :N_HEADS].reshape(nb_p, seq, N_HEADS))
        outs['pr'].append(st_p)
        outs['ps'].append(sh_p)
        outs['sk'].append(kn[n_p:].reshape(nb_s, dec, N_HEADS, HEAD_DIM))
        outs['sv'].append(v[n_p:].reshape(nb_s, dec, N_HEADS, HEAD_DIM))
        outs['sl'].append(lf_s[:, :N_HEADS].reshape(nb_s, dec, N_HEADS))
        outs['sr'].append(st_s)
        outs['ss'].append(sh_s)

    stk = lambda key: jnp.stack(outs[key], axis=0)
    return (x[:n_p].reshape(nb_p, seq, D_MODEL), x[n_p:].reshape(nb_s, dec, D_MODEL),
            stk('pk'), stk('pv'), stk('pl'), stk('pr'), stk('ps'),
            stk('sk'), stk('sv'), stk('sl'), stk('sr'), stk('ss'))
```

```python
import functools

import jax
import jax.numpy as jnp
from jax import lax
from jax.experimental import pallas as pl
from jax.experimental.pallas import tpu as pltpu

F32 = jnp.float32
BF16 = jnp.bfloat16
I32 = jnp.int32

D_MODEL = 1024
HEAD_DIM = 64
N_HEADS = 8
D_R = N_HEADS * HEAD_DIM
D_F = N_HEADS * HEAD_DIM
DECAY_LORA = 64
A_LORA = 64
G_LORA = 128
RWKV_COLS = 3 * D_R + DECAY_LORA + A_LORA + G_LORA
FL_PAD = 128
IN_COLS_PAD = RWKV_COLS + 4 * D_F + FL_PAD
DEPTH = 2
N_EXPERTS = 32
N_GROUPS = 4
EXPERTS_PER_GROUP = N_EXPERTS // N_GROUPS
TOP_K = 2
D_EXPERT = D_MODEL // 2
MOE_BLOCK = 256
ALPHA = (2 * DEPTH) ** 0.25
LN_EPS = 1e-5
GN_EPS = 64e-5
QK_EPS = 1e-6
SCALE = HEAD_DIM ** -0.5
RWKV_CHUNK = 64
INV_BASE = 16
VMEM_LIMIT = 48 * 1024 * 1024

_NN = (((1,), (0,)), ((), ()))
_NT = (((1,), (1,)), ((), ()))
_TN = (((0,), (0,)), ((), ()))


def _dot(a, b, dims=_NN):
    return lax.dot_general(a, b, dims, preferred_element_type=F32)


def _split2(x):
    hi = x.astype(BF16)
    lo = (x - hi.astype(F32)).astype(BF16)
    return hi, lo


def _split3(x):
    hi = x.astype(BF16)
    r = x - hi.astype(F32)
    mid = r.astype(BF16)
    lo = (r - mid.astype(F32)).astype(BF16)
    return hi, mid, lo


def _dot3(a, b, dims=_NN):
    ah, al = _split2(a)
    bh, bl = _split2(b)
    return _dot(ah, bh, dims) + (_dot(ah, bl, dims) + _dot(al, bh, dims))


def _dot_exact_lhs(a_bf16, x, dims=_NN):
    hi, mid, lo = _split3(x)
    return _dot(a_bf16, hi, dims) + (_dot(a_bf16, mid, dims) + _dot(a_bf16, lo, dims))


def _sigmoid(x):
    return 1.0 / (1.0 + jnp.exp(-x))


def _softplus(x):
    return jnp.maximum(x, 0.0) + jnp.log(1.0 + jnp.exp(-jnp.abs(x)))


def _layer_norm(z, g, b):
    mu = jnp.mean(z, axis=-1, keepdims=True)
    zc = z - mu
    var = jnp.mean(zc * zc, axis=-1, keepdims=True)
    return zc * lax.rsqrt(var + LN_EPS) * g + b


def _params(sem):
    return pltpu.CompilerParams(dimension_semantics=sem, vmem_limit_bytes=VMEM_LIMIT)


_IN_SPLITS = (RWKV_COLS, D_F, D_F, D_F, D_F, FL_PAD)


def _in_proj_body(x_ref, w_ref, *out_refs):
    x = x_ref[...].astype(BF16)
    col = 0
    for ref, width in zip(out_refs, _IN_SPLITS):
        for c0 in range(0, width, 512):
            c1 = min(c0 + 512, width)
            ref[:, c0:c1] = _dot(x, w_ref[:, col + c0:col + c1])
        col += width


def _in_proj(x, w, row0, n):
    tm = 256
    blk0 = row0 // tm
    return pl.pallas_call(
        _in_proj_body,
        grid=(n // tm,),
        in_specs=[pl.BlockSpec((tm, D_MODEL), lambda i: (blk0 + i, 0)),
                  pl.BlockSpec((D_MODEL, IN_COLS_PAD), lambda i: (0, 0))],
        out_specs=[pl.BlockSpec((tm, wd), lambda i: (i, 0)) for wd in _IN_SPLITS],
        out_shape=[jax.ShapeDtypeStruct((n, wd), F32) for wd in _IN_SPLITS],
        compiler_params=_params(("parallel",)),
        name="in_proj",
    )(x, w)


GROUP = 4
GROUP_W = GROUP * HEAD_DIM
RWKV_SUB = 4


def _block_diag(x, keep):
    return jnp.where(keep, jnp.concatenate([x] * GROUP, axis=0), jnp.zeros((), x.dtype))


def _heads_mm(a, b, keep, dims=_NN, exact=True):
    if not exact:
        return _dot(a.astype(BF16), _block_diag(b.astype(BF16), keep), dims)
    ah, al = _split2(a)
    bh, bl = _split2(b)
    dh = _block_diag(bh, keep)
    dl = _block_diag(bl, keep)
    return _dot(ah, dh, dims) + (_dot(ah, dl, dims) + _dot(al, dh, dims))


def _fold_heads(f, keep):
    f = jnp.where(keep, f, 0.0)
    return (f[0:HEAD_DIM] + f[HEAD_DIM:2 * HEAD_DIM]) + (f[2 * HEAD_DIM:3 * HEAD_DIM] + f[3 * HEAD_DIM:4 * HEAD_DIM])


def _unit_lower_inverses(ls, c_len, t_row, t_col, keep_tt):
    shift = INV_BASE.bit_length() - 1
    same = (t_row >> shift) == (t_col >> shift)
    eye = jnp.where(t_row == t_col, 1.0, 0.0)
    p = [jnp.where(same, -l, 0.0) for l in ls]
    x = [eye + n for n in p]
    for _ in range(shift - 1):
        p = [_heads_mm(pi, pi, keep_tt) for pi in p]
        x = [xi + _heads_mm(xi, pi, keep_tt) for xi, pi in zip(x, p)]
    size = 2 * INV_BASE
    while size <= c_len:
        s_hi = size.bit_length() - 1
        off = ((t_row >> s_hi) == (t_col >> s_hi)) & ((t_row >> (s_hi - 1)) != (t_col >> (s_hi - 1)))
        xq = [_heads_mm(xi, jnp.where(off, l, 0.0), keep_tt) for xi, l in zip(x, ls)]
        x = [xi - _heads_mm(xqi, xi, keep_tt) for xi, xqi in zip(x, xq)]
        size *= 2
    return x


def _rwkv_body(c_len, n_sub, n_steps, pr_ref, sp_ref, s0_ref, mu_ref, w0_ref, w2_ref, a0_ref, a2_ref, g2_ref,
               kk_ref, ka_ref, rk_ref, lnw_ref, lnb_ref, ones_ref, out_ref, sout_ref, shift_ref, carry_ref, s_ref):
    ci = pl.program_id(1)
    rows = c_len * n_sub
    n_groups = N_HEADS // GROUP

    @pl.when(ci == 0)
    def _():
        carry_ref[...] = sp_ref[0]
        for gi in range(n_groups):
            s_ref[gi] = jnp.concatenate([s0_ref[0, GROUP * gi + j] for j in range(GROUP)], axis=-1)

    pr = pr_ref[...]
    trow = lax.broadcasted_iota(I32, (rows, 1), 0)
    prev = jnp.where(trow == 0, carry_ref[...], pltpu.roll(pr, 1, 0))
    carry_ref[...] = pr[rows - 1:rows, :]
    xs = pr + (prev - pr) * mu_ref[...]
    xr = xs[:, 0:D_R]
    xk = xs[:, D_R:2 * D_R]
    xv = xs[:, 2 * D_R:3 * D_R]
    o = 3 * D_R
    xw = xs[:, o:o + DECAY_LORA]
    xa = xs[:, o + DECAY_LORA:o + DECAY_LORA + A_LORA]
    xg = xs[:, o + DECAY_LORA + A_LORA:RWKV_COLS]

    z = w0_ref[...] + _dot3(jnp.tanh(xw), w2_ref[...])
    lw = -jnp.exp(-_softplus(-z) - 0.5)
    a = _sigmoid(a0_ref[...] + _dot3(xa, a2_ref[...]))
    g = _dot3(_sigmoid(xg), g2_ref[...])
    kk_raw = xk * kk_ref[...]
    k_mod = xk * (1.0 + (a - 1.0) * ka_ref[...])

    shift = c_len.bit_length() - 1
    r2 = lax.broadcasted_iota(I32, (rows, rows), 0)
    c2 = lax.broadcasted_iota(I32, (rows, rows), 1)
    within = ((r2 >> shift) == (c2 >> shift)) & (r2 >= c2)
    cl = _dot_exact_lhs(jnp.where(within, 1.0, 0.0).astype(BF16), lw)

    ones = ones_ref[...]

    def head_sum(x):
        hi, lo = _split2(x)
        return _dot(hi, ones) + _dot(lo, ones)

    kk = kk_raw / jnp.maximum(jnp.sqrt(head_sum(kk_raw * kk_raw)), 1e-12)
    eg = jnp.exp(cl)
    e_inv = jnp.exp(-cl)
    r_dec = xr * eg
    kk_dec = kk * jnp.exp(cl - lw)
    b_und = kk * a * e_inv
    k_und = k_mod * e_inv

    wt = GROUP * c_len
    t_row = lax.broadcasted_iota(I32, (c_len, wt), 0)
    t_col = lax.broadcasted_iota(I32, (c_len, wt), 1) & (c_len - 1)
    strict = t_row > t_col
    incl = t_row >= t_col
    hd_shift = HEAD_DIM.bit_length() - 1
    keep_tt = (lax.broadcasted_iota(I32, (wt, wt), 0) >> shift) == (lax.broadcasted_iota(I32, (wt, wt), 1) >> shift)
    keep_tf = (lax.broadcasted_iota(I32, (wt, GROUP_W), 0) >> shift) == (
        lax.broadcasted_iota(I32, (wt, GROUP_W), 1) >> hd_shift)
    keep_ff = (lax.broadcasted_iota(I32, (GROUP_W, GROUP_W), 0) >> hd_shift) == (
        lax.broadcasted_iota(I32, (GROUP_W, GROUP_W), 1) >> hd_shift)

    chains = [(c, gi) for c in range(n_sub) for gi in range(n_groups)]
    cut = lambda x, c, gi: x[c * c_len:(c + 1) * c_len, gi * GROUP_W:(gi + 1) * GROUP_W]
    lhs = [jnp.concatenate([cut(kk_dec, c, gi), cut(r_dec, c, gi)], axis=0) for c, gi in chains]
    bu = [cut(b_und, c, gi) for c, gi in chains]
    ku = [cut(k_und, c, gi) for c, gi in chains]
    vh = [cut(xv, c, gi) for c, gi in chains]
    n_ch = range(len(chains))
    gb = [_heads_mm(lhs[i], bu[i], keep_tf, _NT) for i in n_ch]
    gk = [_heads_mm(lhs[i], ku[i], keep_tf, _NT, exact=False) for i in n_ch]
    l_b = [jnp.where(strict, m[:c_len], 0.0) for m in gb]
    m_rb = [jnp.where(incl, m[c_len:], 0.0) for m in gb]
    l_k = [jnp.where(strict, m[:c_len], 0.0) for m in gk]
    m_rk = [jnp.where(incl, m[c_len:], 0.0) for m in gk]
    t_inv = _unit_lower_inverses(l_b, c_len, t_row, t_col, keep_tt)
    lkv = [_heads_mm(l_k[i], vh[i], keep_tf, exact=False) for i in n_ch]
    mkv = [_heads_mm(m_rk[i], vh[i], keep_tf, exact=False) for i in n_ch]
    fv = [_fold_heads(_dot(vh[i].astype(BF16), ku[i].astype(BF16), _TN), keep_ff) for i in n_ch]

    state = [s_ref[gi] for gi in range(n_groups)]
    y_rows = []
    for c in range(n_sub):
        ids = [c * n_groups + gi for gi in range(n_groups)]
        ps = [_heads_mm(lhs[i], state[gi], keep_ff, _NT, exact=False) for gi, i in enumerate(ids)]
        u = [-_heads_mm(t_inv[i], ps[gi][:c_len] + lkv[i], keep_tf, exact=False) for gi, i in enumerate(ids)]
        y = [ps[gi][c_len:] + mkv[i] + _heads_mm(m_rb[i], u[gi], keep_tf, exact=False) for gi, i in enumerate(ids)]
        fu = [_fold_heads(_dot(u[gi].astype(BF16), bu[i].astype(BF16), _TN), keep_ff) for gi, i in enumerate(ids)]
        last = c * c_len + c_len - 1
        state = [(state[gi] + fv[i] + fu[gi]) * eg[last:last + 1, gi * GROUP_W:(gi + 1) * GROUP_W]
                 for gi, i in enumerate(ids)]
        y_rows.append(jnp.concatenate(y, axis=-1))
    for gi in range(n_groups):
        s_ref[gi] = state[gi]

    y = jnp.concatenate(y_rows, axis=0)
    yc = y - head_sum(y) * (1.0 / HEAD_DIM)
    var = head_sum(yc * yc) * (1.0 / HEAD_DIM)
    yn = yc * lax.rsqrt(var + GN_EPS) * lnw_ref[...] + lnb_ref[...]
    bonus = head_sum(xr * k_mod * rk_ref[...]) * xv
    out_ref[...] = (yn + bonus) * g

    @pl.when(ci == n_steps - 1)
    def _():
        for gi in range(n_groups):
            for j in range(GROUP):
                sout_ref[0, GROUP * gi + j] = state[gi][:, j * HEAD_DIM:(j + 1) * HEAD_DIM]
        shift_ref[0] = pr[rows - 1:rows, :]


def _rwkv(pr, shift_prev, s0, lp, n_seq, seq_len, row0):
    c_len = min(RWKV_CHUNK, seq_len)
    n_sub = min(RWKV_SUB, seq_len // c_len)
    rows = c_len * n_sub
    n_steps = seq_len // rows
    blk0 = row0 // rows
    vec = lambda wd: pl.BlockSpec((1, wd), lambda b, c: (0, 0))
    mat = lambda r, wd: pl.BlockSpec((r, wd), lambda b, c: (0, 0))
    in_specs = [
        pl.BlockSpec((rows, RWKV_COLS), lambda b, c: (blk0 + b * n_steps + c, 0)),
        pl.BlockSpec((1, 1, RWKV_COLS), lambda b, c: (b, 0, 0)),
        pl.BlockSpec((1, N_HEADS, HEAD_DIM, HEAD_DIM), lambda b, c: (b, 0, 0, 0)),
        vec(RWKV_COLS), vec(D_R), mat(DECAY_LORA, D_R), vec(D_R), mat(A_LORA, D_R), mat(G_LORA, D_R),
        vec(D_R), vec(D_R), vec(D_R), vec(D_R), vec(D_R), mat(D_R, D_R),
    ]
    args = [pr, shift_prev, s0, lp['mu'], lp['w0'], lp['w2'], lp['a0'], lp['a2'], lp['g2'],
            lp['k_k'], lp['k_a'], lp['r_k'], lp['lnx_w'], lp['lnx_b'], lp['ones_bd']]
    return pl.pallas_call(
        functools.partial(_rwkv_body, c_len, n_sub, n_steps),
        grid=(n_seq, n_steps),
        in_specs=in_specs,
        out_specs=[pl.BlockSpec((rows, D_R), lambda b, c: (b * n_steps + c, 0)),
                   pl.BlockSpec((1, N_HEADS, HEAD_DIM, HEAD_DIM), lambda b, c: (b, 0, 0, 0)),
                   pl.BlockSpec((1, 1, RWKV_COLS), lambda b, c: (b, 0, 0))],
        out_shape=[jax.ShapeDtypeStruct((n_seq * seq_len, D_R), F32),
                   jax.ShapeDtypeStruct((n_seq, N_HEADS, HEAD_DIM, HEAD_DIM), F32),
                   jax.ShapeDtypeStruct((n_seq, 1, RWKV_COLS), F32)],
        scratch_shapes=[pltpu.VMEM((1, RWKV_COLS), F32), pltpu.VMEM((N_HEADS // GROUP, HEAD_DIM, GROUP_W), F32)],
        compiler_params=_params(("arbitrary", "arbitrary")),
        name="rwkv_mixer",
    )(*args)


SLOT = 2 * HEAD_DIM
C_LANE = HEAD_DIM


def _fox_prep_body(q_ref, k_ref, v_ref, c_ref, qg_ref, kg_ref, ones_ref, place_ref, pcq_ref, pck_ref, oneq_ref,
                   onek_ref, qa_ref, kn_ref, ka_ref, vb_ref):
    ones = ones_ref[...]

    def rms(x, gain):
        hi, lo = _split2(x * x)
        ss = _dot(hi, ones) + _dot(lo, ones)
        return x * lax.rsqrt(ss * (1.0 / HEAD_DIM) + QK_EPS) * gain

    qn = rms(q_ref[...], qg_ref[...]) * SCALE
    kn = rms(k_ref[...], kg_ref[...])
    kn_ref[...] = kn
    vb_ref[...] = v_ref[...].astype(BF16)
    c_parts = _split3(c_ref[...])

    def slots(xb, pc_ref, one_ref):
        acc = _dot(xb, place_ref[...]) + one_ref[...]
        for j in range(3):
            acc = acc + _dot(c_parts[j], pc_ref[j])
        return acc.astype(BF16)

    qa_ref[...] = slots(qn.astype(BF16), pcq_ref, oneq_ref)
    ka_ref[...] = slots(kn.astype(BF16), pck_ref, onek_ref)


def _slot_constants():
    d = jnp.arange(D_F, dtype=I32)
    lane = jnp.arange(N_HEADS * SLOT, dtype=I32)
    place = (lane[None, :] == (d // HEAD_DIM * SLOT + d % HEAD_DIM)[:, None]).astype(BF16)
    h = jnp.arange(FL_PAD, dtype=I32)[None, :, None]
    j = jnp.arange(3, dtype=I32)[:, None, None]
    is_head = h < N_HEADS
    pcq = ((lane[None, None, :] == h * SLOT + C_LANE + j) & is_head).astype(BF16)
    pck = -((lane[None, None, :] == h * SLOT + C_LANE + 3 + j) & is_head).astype(BF16)
    in_slot = lane % SLOT
    oneq = ((in_slot >= C_LANE + 3) & (in_slot < C_LANE + 6)).astype(F32).reshape(1, -1)
    onek = ((in_slot >= C_LANE) & (in_slot < C_LANE + 3)).astype(F32).reshape(1, -1)
    return place, pcq, pck, oneq, onek


def _fox_prep(q, k, v, c, q_gain, k_gain, ones_bd, slot_consts):
    n = q.shape[0]
    tm = 256
    wide = N_HEADS * SLOT
    row = lambda wd: pl.BlockSpec((tm, wd), lambda i: (i, 0))
    vec = lambda wd: pl.BlockSpec((1, wd), lambda i: (0, 0))
    full = lambda *shape: pl.BlockSpec(shape, lambda i: (0,) * len(shape))
    return pl.pallas_call(
        _fox_prep_body,
        grid=(n // tm,),
        in_specs=[row(D_F), row(D_F), row(D_F), row(FL_PAD), vec(D_F), vec(D_F), full(D_F, D_F),
                  full(D_F, wide), full(3, FL_PAD, wide), full(3, FL_PAD, wide), vec(wide), vec(wide)],
        out_specs=[row(wide), row(D_F), row(wide), row(D_F)],
        out_shape=[jax.ShapeDtypeStruct((n, wide), BF16), jax.ShapeDtypeStruct((n, D_F), F32),
                   jax.ShapeDtypeStruct((n, wide), BF16), jax.ShapeDtypeStruct((n, D_F), BF16)],
        compiler_params=_params(("parallel",)),
        name="fox_prep",
    )(q, k, v, c, q_gain, k_gain, ones_bd, *slot_consts)


def _logf_cumsum_body(fl_ref, bf_ref, lf_ref, c_ref, carry_ref):
    @pl.when(pl.program_id(1) == 0)
    def _():
        carry_ref[...] = jnp.zeros_like(carry_ref)

    lf = -_softplus(-(fl_ref[...] + bf_ref[...]))
    lf_ref[...] = lf
    t = lf.shape[0]
    row = lax.broadcasted_iota(I32, (t, t), 0)
    col = lax.broadcasted_iota(I32, (t, t), 1)
    cs = _dot_exact_lhs(jnp.where(row >= col, 1.0, 0.0).astype(BF16), lf) + carry_ref[...]
    c_ref[...] = cs
    carry_ref[...] = cs[t - 1:t, :]


def _logf_cumsum(fl, b_f, n_seq, seq_len, row0):
    tc = min(seq_len, 256)
    nt = seq_len // tc
    blk0 = row0 // tc
    out = pl.BlockSpec((tc, FL_PAD), lambda b, j: (b * nt + j, 0))
    return pl.pallas_call(
        _logf_cumsum_body,
        grid=(n_seq, nt),
        in_specs=[pl.BlockSpec((tc, FL_PAD), lambda b, j: (blk0 + b * nt + j, 0)),
                  pl.BlockSpec((1, FL_PAD), lambda b, j: (0, 0))],
        out_specs=[out, out],
        out_shape=[jax.ShapeDtypeStruct((n_seq * seq_len, FL_PAD), F32)] * 2,
        scratch_shapes=[pltpu.VMEM((1, FL_PAD), F32)],
        compiler_params=_params(("arbitrary", "arbitrary")),
        name="logf_cumsum",
    )(fl, b_f)


ATT_TILE = 512


def _attn_prompt_body(q_ref, k_ref, v_ref, o_ref):
    i = pl.program_id(2)
    t = ATT_TILE
    row = lax.broadcasted_iota(I32, (t, t), 0)
    col = lax.broadcasted_iota(I32, (t, t), 1)
    causal = row >= col
    pair = range(2)
    q = [q_ref[:, hh * SLOT:(hh + 1) * SLOT] for hh in pair]

    def tile(j, carry, masked):
        m, l, acc = carry
        j0 = pl.multiple_of(j * t, t)
        s = [_dot(q[hh], k_ref[pl.ds(j0, t), hh * SLOT:(hh + 1) * SLOT], _NT) for hh in pair]
        if masked:
            s = [jnp.where(causal, sh, -jnp.inf) for sh in s]
        m_new = [jnp.maximum(m[hh], jnp.max(s[hh], axis=-1, keepdims=True)) for hh in pair]
        alpha = [jnp.exp(m[hh] - m_new[hh]) for hh in pair]
        p = [jnp.exp(s[hh] - m_new[hh]) for hh in pair]
        l = [alpha[hh] * l[hh] + jnp.sum(p[hh], axis=-1, keepdims=True) for hh in pair]
        pv = [_dot(p[hh].astype(BF16), v_ref[pl.ds(j0, t), hh * HEAD_DIM:(hh + 1) * HEAD_DIM]) for hh in pair]
        acc = [alpha[hh] * acc[hh] + pv[hh] for hh in pair]
        return m_new, l, acc

    init = ([jnp.full((t, 1), -jnp.inf, F32)] * 2, [jnp.zeros((t, 1), F32)] * 2,
            [jnp.zeros((t, HEAD_DIM), F32)] * 2)
    carry = lax.fori_loop(0, i, lambda j, c: tile(j, c, False), init)
    _, l, acc = tile(i, carry, True)
    o_ref[...] = jnp.concatenate([acc[hh] / l[hh] for hh in pair], axis=-1)


def _attn_prompt(qa, ka, vb, n_seq, seq_len):
    t = ATT_TILE
    nq = seq_len // t
    return pl.pallas_call(
        _attn_prompt_body,
        grid=(n_seq, N_HEADS // 2, nq),
        in_specs=[pl.BlockSpec((t, 2 * SLOT), lambda b, p, i: (b * nq + i, p)),
                  pl.BlockSpec((seq_len, 2 * SLOT), lambda b, p, i: (b, p)),
                  pl.BlockSpec((seq_len, 2 * HEAD_DIM), lambda b, p, i: (b, p))],
        out_specs=pl.BlockSpec((t, 2 * HEAD_DIM), lambda b, p, i: (b * nq + i, p)),
        out_shape=jax.ShapeDtypeStruct((n_seq * seq_len, D_F), F32),
        compiler_params=_params(("parallel", "parallel", "arbitrary")),
        name="fox_attn_prompt",
    )(qa, ka, vb)


SAMPLE_CHUNK = 2048
TAIL_BLOCK = 1024


def _attn_sample_body(n_chunks, q_ref, kn_ref, vn_ref, kc_ref, vc_ref, lp_ref, after_ref, o_ref,
                      m_ref, l_ref, acc_ref, suffix_ref):
    j = pl.program_id(1)
    n = q_ref.shape[0]
    heads = range(N_HEADS)

    @pl.when(j == 0)
    def _():
        m_ref[...] = jnp.full(m_ref.shape, -jnp.inf, F32)
        l_ref[...] = jnp.zeros_like(l_ref)
        acc_ref[...] = jnp.zeros_like(acc_ref)
        suffix_ref[...] = jnp.zeros_like(suffix_ref)

    after = after_ref[...]
    tp = lp_ref.shape[3]
    tails = []
    suffix = suffix_ref[...]
    for b0 in range(tp - TAIL_BLOCK, -1, -TAIL_BLOCK):
        lp = lp_ref[0, 0, :, b0:b0 + TAIL_BLOCK]
        hi, mid, lo = _split3(lp)
        tails.insert(0, _dot(hi, after) + (_dot(mid, after) + _dot(lo, after)) + suffix)
        suffix = suffix + jnp.sum(lp, axis=-1, keepdims=True)
    suffix_ref[...] = suffix
    tail = jnp.concatenate(tails, axis=-1)

    q_slot = [q_ref[:, h * SLOT:(h + 1) * SLOT] for h in heads]
    c_col = [(qs[:, C_LANE:C_LANE + 1].astype(F32) + qs[:, C_LANE + 1:C_LANE + 2].astype(F32)
              + qs[:, C_LANE + 2:C_LANE + 3].astype(F32)) for qs in q_slot]
    s = [_dot(q_slot[h][:, :HEAD_DIM], kc_ref[0, 0, h].astype(BF16)) + c_col[h] + tail[h:h + 1, :] for h in heads]
    m_old = [m_ref[h] for h in heads]
    m_new = [jnp.maximum(m_old[h], jnp.max(s[h], axis=-1, keepdims=True)) for h in heads]
    alpha = [jnp.exp(m_old[h] - m_new[h]) for h in heads]
    p = [jnp.exp(s[h] - m_new[h]) for h in heads]
    pv = [_dot(p[h].astype(BF16), vc_ref[0, 0, h].astype(BF16), _NT) for h in heads]
    for h in heads:
        m_ref[h] = m_new[h]
        l_ref[h] = alpha[h] * l_ref[h] + jnp.sum(p[h], axis=-1, keepdims=True)
        acc_ref[h] = alpha[h] * acc_ref[h] + pv[h]

    @pl.when(j == n_chunks - 1)
    def _():
        row = lax.broadcasted_iota(I32, (n, n), 0)
        col = lax.broadcasted_iota(I32, (n, n), 1)
        s_new = [jnp.where(row >= col, _dot(q_slot[h], kn_ref[:, h * SLOT:(h + 1) * SLOT], _NT), -jnp.inf)
                 for h in heads]
        m_fin = [jnp.maximum(m_ref[h], jnp.max(s_new[h], axis=-1, keepdims=True)) for h in heads]
        a_fin = [jnp.exp(m_ref[h] - m_fin[h]) for h in heads]
        p_new = [jnp.exp(s_new[h] - m_fin[h]) for h in heads]
        l_fin = [a_fin[h] * l_ref[h] + jnp.sum(p_new[h], axis=-1, keepdims=True) for h in heads]
        acc = [a_fin[h] * acc_ref[h] + _dot(p_new[h].astype(BF16), vn_ref[:, h * HEAD_DIM:(h + 1) * HEAD_DIM])
               for h in heads]
        o_ref[...] = jnp.concatenate([acc[h] / l_fin[h] for h in heads], axis=-1)


def _attn_sample(qa, ka, vb, cache_k, cache_v, logf_rows, layer, n_seq, n_new, row0):
    past = cache_k.shape[4]
    tp = min(SAMPLE_CHUNK, past)
    n_chunks = past // tp
    blk0 = row0 // n_new
    frame = jnp.arange(TAIL_BLOCK, dtype=I32)
    after = (frame[:, None] > frame[None, :]).astype(BF16)
    rows = lambda wd: pl.BlockSpec((n_new, wd), lambda b, j: (blk0 + b, 0))
    cache = lambda: pl.BlockSpec((1, 1, N_HEADS, HEAD_DIM, tp), lambda b, j: (layer, b, 0, 0, n_chunks - 1 - j))
    return pl.pallas_call(
        functools.partial(_attn_sample_body, n_chunks),
        grid=(n_seq, n_chunks),
        in_specs=[rows(N_HEADS * SLOT), rows(N_HEADS * SLOT), rows(D_F), cache(), cache(),
                  pl.BlockSpec((1, 1, N_HEADS, tp), lambda b, j: (layer, b, 0, n_chunks - 1 - j)),
                  pl.BlockSpec((TAIL_BLOCK, TAIL_BLOCK), lambda b, j: (0, 0))],
        out_specs=pl.BlockSpec((n_new, D_F), lambda b, j: (b, 0)),
        out_shape=jax.ShapeDtypeStruct((n_seq * n_new, D_F), F32),
        scratch_shapes=[pltpu.VMEM((N_HEADS, n_new, 1), F32), pltpu.VMEM((N_HEADS, n_new, 1), F32),
                        pltpu.VMEM((N_HEADS, n_new, HEAD_DIM), F32), pltpu.VMEM((N_HEADS, 1), F32)],
        compiler_params=_params(("parallel", "arbitrary")),
        name="fox_attn_sample",
    )(qa, ka, vb, cache_k, cache_v, logf_rows, after)


LANES = 128
TOKEN_ROWS = D_MODEL // LANES


def _store_token_tiles(ref, x):
    m = x.shape[0]
    for c in range(TOKEN_ROWS):
        ref[pl.ds(c, m, stride=TOKEN_ROWS), :] = x[:, c * LANES:(c + 1) * LANES]


def _load_token_tiles(ref, m):
    return jnp.concatenate([ref[pl.ds(c, m, stride=TOKEN_ROWS), :] for c in range(TOKEN_ROWS)], axis=-1)


def _out_proj_body(tiles_p, n_x, *refs):
    ryp_ref, rys_ref, fop_ref, fos_ref, ogp_ref, ogs_ref = refs[:6]
    x_refs = refs[6:6 + n_x]
    w_ref, g_ref, b_ref, o_ref, ot_ref = refs[6 + n_x:]
    from_prompt = pl.program_id(0) < tiles_p
    pick = lambda p_ref, s_ref: jnp.where(from_prompt, p_ref[...], s_ref[...])
    ry = pick(ryp_ref, rys_ref).astype(BF16)
    fy = (pick(fop_ref, fos_ref) * _sigmoid(pick(ogp_ref, ogs_ref))).astype(BF16)
    x = pick(*x_refs) if n_x == 2 else x_refs[0][...]
    m = _dot(ry, w_ref[0:D_R, :]) + _dot(fy, w_ref[D_R:D_R + D_F, :])
    out = _layer_norm(ALPHA * x + m, g_ref[...], b_ref[...])
    o_ref[...] = out
    _store_token_tiles(ot_ref, out)


def _out_proj_ln(ry, fo, og, x, w, g, b):
    tm = 256
    tiles_p = ry[0].shape[0] // tm
    n = ry[0].shape[0] + ry[1].shape[0]
    row = lambda wd: pl.BlockSpec((tm, wd), lambda i: (i, 0))
    row_p = lambda wd: pl.BlockSpec((tm, wd), lambda i: (jnp.minimum(i, tiles_p - 1), 0))
    row_s = lambda wd: pl.BlockSpec((tm, wd), lambda i: (jnp.maximum(i - tiles_p, 0), 0))
    pair = lambda wd: [row_p(wd), row_s(wd)]
    vec = pl.BlockSpec((1, D_MODEL), lambda i: (0, 0))
    x_parts = tuple(x) if isinstance(x, (tuple, list)) else (x,)
    x_specs = pair(D_MODEL) if len(x_parts) == 2 else [row(D_MODEL)]
    return pl.pallas_call(
        functools.partial(_out_proj_body, tiles_p, len(x_parts)),
        grid=(n // tm,),
        in_specs=pair(D_R) + pair(D_F) + pair(D_F) + x_specs + [
            pl.BlockSpec((D_R + D_F, D_MODEL), lambda i: (0, 0)), vec, vec],
        out_specs=[row(D_MODEL), pl.BlockSpec((tm * TOKEN_ROWS, LANES), lambda i: (i, 0))],
        out_shape=[jax.ShapeDtypeStruct((n, D_MODEL), F32), jax.ShapeDtypeStruct((n * TOKEN_ROWS, LANES), F32)],
        compiler_params=_params(("parallel",)),
        name="out_proj_ln",
    )(*ry, *fo, *og, *x_parts, w, g, b)


def _router_body(x_ref, rw_ref, rb_ref, earlier_ref, e_ref, g_ref, r_ref, count_ref, seen_ref):
    tn = x_ref.shape[0]
    scores = _sigmoid(_dot3(rw_ref[...], x_ref[...], _NT))
    sel = scores + rb_ref[...]
    sel4 = sel.reshape(N_GROUPS, EXPERTS_PER_GROUP, tn)
    sc4 = scores.reshape(N_GROUPS, EXPERTS_PER_GROUP, tn)
    lane_e = lax.broadcasted_iota(I32, (N_GROUPS, EXPERTS_PER_GROUP, tn), 1)

    def top2(vals, idx_iota, axis):
        m1 = jnp.max(vals, axis=axis, keepdims=True)
        i1 = jnp.min(jnp.where(vals == m1, idx_iota, EXPERTS_PER_GROUP), axis=axis, keepdims=True)
        rest = jnp.where(idx_iota == i1, -jnp.inf, vals)
        m2 = jnp.max(rest, axis=axis, keepdims=True)
        i2 = jnp.min(jnp.where(rest == m2, idx_iota, EXPERTS_PER_GROUP), axis=axis, keepdims=True)
        return m1, i1, m2, i2

    m1, _, m2, _ = top2(sel4, lane_e, 1)
    gsum = m1 + m2
    g_iota = lax.broadcasted_iota(I32, (N_GROUPS, 1, tn), 0)
    gmax = jnp.max(gsum, axis=0, keepdims=True)
    g_idx = jnp.min(jnp.where(gsum == gmax, g_iota, N_GROUPS), axis=0, keepdims=True)
    pick = g_iota == g_idx
    sel_g = jnp.max(jnp.where(pick, sel4, -jnp.inf), axis=0)
    sc_g = jnp.max(jnp.where(pick, sc4, -jnp.inf), axis=0)
    e_iota = lax.broadcasted_iota(I32, (EXPERTS_PER_GROUP, tn), 0)
    _, i1, _, i2 = top2(sel_g, e_iota, 0)
    gate1 = jnp.sum(jnp.where(e_iota == i1, sc_g, 0.0), axis=0, keepdims=True)
    gate2 = jnp.sum(jnp.where(e_iota == i2, sc_g, 0.0), axis=0, keepdims=True)
    tot = gate1 + gate2
    base = g_idx[0] * EXPERTS_PER_GROUP
    e1 = base + i1
    e2 = base + i2
    e_ref[...] = jnp.concatenate([e1, e2], axis=0)
    g_ref[...] = jnp.concatenate([gate1 / tot, gate2 / tot], axis=0)

    @pl.when(pl.program_id(0) == 0)
    def _():
        seen_ref[...] = jnp.zeros_like(seen_ref)

    all_e = lax.broadcasted_iota(I32, (N_EXPERTS, tn), 0)
    hit1 = all_e == e1
    hit2 = all_e == e2
    hits = jnp.where(hit1 | hit2, 1.0, 0.0)
    before = _dot(hits.astype(BF16), earlier_ref[...]) + seen_ref[...]
    r1 = jnp.sum(jnp.where(hit1, before, 0.0), axis=0, keepdims=True)
    r2 = jnp.sum(jnp.where(hit2, before, 0.0), axis=0, keepdims=True)
    r_ref[...] = jnp.concatenate([r1, r2], axis=0).astype(I32)
    seen_ref[...] = seen_ref[...] + jnp.sum(hits, axis=-1, keepdims=True)
    count_ref[...] = seen_ref[...].astype(I32)


def _router(x, rw_t, rb_col):
    n = x.shape[0]
    tn = 512
    tok = jnp.arange(tn, dtype=I32)
    earlier = (tok[:, None] < tok[None, :]).astype(BF16)
    pair = lambda: pl.BlockSpec((TOP_K, tn), lambda i: (0, i))
    return pl.pallas_call(
        _router_body,
        grid=(n // tn,),
        in_specs=[pl.BlockSpec((tn, D_MODEL), lambda i: (i, 0)),
                  pl.BlockSpec((N_EXPERTS, D_MODEL), lambda i: (0, 0)),
                  pl.BlockSpec((N_EXPERTS, 1), lambda i: (0, 0)),
                  pl.BlockSpec((tn, tn), lambda i: (0, 0))],
        out_specs=[pair(), pair(), pair(), pl.BlockSpec((N_EXPERTS, 1), lambda i: (0, 0))],
        out_shape=[jax.ShapeDtypeStruct((TOP_K, n), I32), jax.ShapeDtypeStruct((TOP_K, n), F32),
                   jax.ShapeDtypeStruct((TOP_K, n), I32), jax.ShapeDtypeStruct((N_EXPERTS, 1), I32)],
        scratch_shapes=[pltpu.VMEM((N_EXPERTS, 1), F32)],
        compiler_params=_params(("arbitrary",)),
        name="router",
    )(x, rw_t, rb_col, earlier)


def _slot_rows_body(e_ref, r_ref, start_ref, pos_ref):
    tn = e_ref.shape[1]
    all_e = lax.broadcasted_iota(I32, (N_EXPERTS, tn), 0)
    rows = [jnp.sum(jnp.where(all_e == e_ref[k:k + 1, :], start_ref[...], 0), axis=0, keepdims=True)
            for k in range(TOP_K)]
    pos_ref[...] = r_ref[...] + jnp.concatenate(rows, axis=0)


def _slot_rows(eidx_t, rank_t, expert_start):
    n = eidx_t.shape[1]
    tn = 512
    pair = lambda: pl.BlockSpec((TOP_K, tn), lambda i: (0, i))
    return pl.pallas_call(
        _slot_rows_body,
        grid=(n // tn,),
        in_specs=[pair(), pair(), pl.BlockSpec((N_EXPERTS, 1), lambda i: (0, 0))],
        out_specs=pair(),
        out_shape=jax.ShapeDtypeStruct((TOP_K, n), I32),
        compiler_params=_params(("parallel",)),
        name="moe_slot_rows",
    )(eidx_t, rank_t, expert_start)


def _row_copy(src_hbm, src_row8, dst, dst_token, sem):
    src = src_hbm.at[pl.ds(pl.multiple_of(src_row8, TOKEN_ROWS), TOKEN_ROWS)]
    return pltpu.make_async_copy(src, dst.at[pl.ds(pl.multiple_of(dst_token * TOKEN_ROWS, TOKEN_ROWS), TOKEN_ROWS)], sem)


def _expert_body(be_ref, used_ref, tok_ref, tok_next_ref, x_hbm, w1_ref, w3_ref, w2_ref, y_ref,
                 buf_ref, w1b_ref, w3b_ref, w2b_ref, sem):
    i = pl.program_id(0)
    n_used = used_ref[0]
    slot = lax.rem(i, 2)

    def start_gather(ids_ref, s):
        def body(r, carry):
            _row_copy(x_hbm, ids_ref[0, 0, r], buf_ref.at[s], r, sem.at[s]).start()
            return carry
        lax.fori_loop(0, MOE_BLOCK, body, 0, unroll=8)

    @pl.when(jnp.logical_and(i == 0, n_used > 0))
    def _():
        start_gather(tok_ref, 0)

    @pl.when(i + 1 < n_used)
    def _():
        start_gather(tok_next_ref, 1 - slot)

    @pl.when(i < n_used)
    def _():
        @pl.when(jnp.logical_or(i == 0, be_ref[i] != be_ref[jnp.maximum(i - 1, 0)]))
        def _():
            w1b_ref[...] = w1_ref[0, 0].astype(BF16)
            w3b_ref[...] = w3_ref[0, 0].astype(BF16)
            w2b_ref[...] = w2_ref[0, 0].astype(BF16)

        def wait(r, carry):
            _row_copy(x_hbm, 0, buf_ref.at[slot], r, sem.at[slot]).wait()
            return carry
        lax.fori_loop(0, MOE_BLOCK, wait, 0, unroll=8)

        h = _load_token_tiles(buf_ref.at[slot], MOE_BLOCK).astype(BF16)
        a = _dot(h, w1b_ref[...])
        b = _dot(h, w3b_ref[...])
        act = (a * _sigmoid(a) * b).astype(BF16)
        _store_token_tiles(y_ref, _dot(act, w2b_ref[...]))

    @pl.when(i >= n_used)
    def _():
        y_ref[...] = jnp.zeros_like(y_ref)


def _experts(block_expert, n_used, tok_blocks, x, w1, w3, w2, layer):
    nb = block_expert.shape[0]
    ids = lambda f: pl.BlockSpec((1, 1, MOE_BLOCK), f, memory_space=pltpu.SMEM)
    grid_spec = pltpu.PrefetchScalarGridSpec(
        num_scalar_prefetch=2,
        grid=(nb,),
        in_specs=[ids(lambda i, be, nu: (i, 0, 0)),
                  ids(lambda i, be, nu: (jnp.minimum(i + 1, nb - 1), 0, 0)),
                  pl.BlockSpec(memory_space=pl.ANY),
                  pl.BlockSpec((1, 1, D_MODEL, D_EXPERT), lambda i, be, nu: (layer, be[i], 0, 0)),
                  pl.BlockSpec((1, 1, D_MODEL, D_EXPERT), lambda i, be, nu: (layer, be[i], 0, 0)),
                  pl.BlockSpec((1, 1, D_EXPERT, D_MODEL), lambda i, be, nu: (layer, be[i], 0, 0))],
        out_specs=pl.BlockSpec((MOE_BLOCK * TOKEN_ROWS, LANES), lambda i, be, nu: (i, 0)),
        scratch_shapes=[pltpu.VMEM((2, MOE_BLOCK * TOKEN_ROWS, LANES), F32),
                        pltpu.VMEM((D_MODEL, D_EXPERT), BF16), pltpu.VMEM((D_MODEL, D_EXPERT), BF16),
                        pltpu.VMEM((D_EXPERT, D_MODEL), BF16), pltpu.SemaphoreType.DMA((2,))],
    )
    return pl.pallas_call(
        _expert_body,
        grid_spec=grid_spec,
        out_shape=jax.ShapeDtypeStruct((nb * MOE_BLOCK * TOKEN_ROWS, LANES), F32),
        compiler_params=_params(("arbitrary",)),
        name="moe_experts",
    )(block_expert, n_used, tok_blocks, tok_blocks, x, w1, w3, w2)


COMBINE_TILE = 128


def _combine_body(pos_ref, pos_next_ref, y_hbm, x_ref, gate_ref, g_ref, b_ref, o_ref, buf_ref, sem):
    i = pl.program_id(0)
    slot = lax.rem(i, 2)

    def start_gather(ids_ref, s):
        def body(t, carry):
            for k in range(TOP_K):
                _row_copy(y_hbm, ids_ref[0, 0, TOP_K * t + k], buf_ref.at[s, k], t, sem.at[s]).start()
            return carry
        lax.fori_loop(0, COMBINE_TILE, body, 0, unroll=4)

    @pl.when(i == 0)
    def _():
        start_gather(pos_ref, 0)

    @pl.when(i + 1 < pl.num_programs(0))
    def _():
        start_gather(pos_next_ref, 1 - slot)

    def wait(t, carry):
        for k in range(TOP_K):
            _row_copy(y_hbm, 0, buf_ref.at[slot, k], t, sem.at[slot]).wait()
        return carry
    lax.fori_loop(0, COMBINE_TILE, wait, 0, unroll=4)

    gate = gate_ref[...]
    y = (gate[:, 0:1] * _load_token_tiles(buf_ref.at[slot, 0], COMBINE_TILE)
         + gate[:, 1:2] * _load_token_tiles(buf_ref.at[slot, 1], COMBINE_TILE))
    o_ref[...] = _layer_norm(ALPHA * x_ref[...] + y, g_ref[...], b_ref[...])


def _combine_ln(pos_blocks, y_pad, x, gate, g, b):
    n = x.shape[0]
    tm = COMBINE_TILE
    nt = n // tm
    vec = pl.BlockSpec((1, D_MODEL), lambda i: (0, 0))
    ids = lambda f: pl.BlockSpec((1, 1, TOP_K * tm), f, memory_space=pltpu.SMEM)
    return pl.pallas_call(
        _combine_body,
        grid=(nt,),
        in_specs=[ids(lambda i: (i, 0, 0)), ids(lambda i: (jnp.minimum(i + 1, nt - 1), 0, 0)),
                  pl.BlockSpec(memory_space=pl.ANY),
                  pl.BlockSpec((tm, D_MODEL), lambda i: (i, 0)),
                  pl.BlockSpec((tm, TOP_K), lambda i: (i, 0)), vec, vec],
        out_specs=pl.BlockSpec((tm, D_MODEL), lambda i: (i, 0)),
        out_shape=jax.ShapeDtypeStruct((n, D_MODEL), F32),
        scratch_shapes=[pltpu.VMEM((2, TOP_K, tm * TOKEN_ROWS, LANES), F32), pltpu.SemaphoreType.DMA((2,))],
        compiler_params=_params(("arbitrary",)),
        name="moe_combine_ln",
    )(pos_blocks, pos_blocks, y_pad, x, gate, g, b)


def _grouped_moe_ln(x, x_tiles, rw_t, rb_col, w1, w3, w2, layer, g, b):
    n = x.shape[0]
    eidx_t, gate_t, rank_t, counts = _router(x, rw_t, rb_col)
    padded = (counts[:, 0] + MOE_BLOCK - 1) // MOE_BLOCK * MOE_BLOCK
    ends = jnp.cumsum(padded)
    nb = -(-n * TOP_K // MOE_BLOCK) + N_EXPERTS
    block_start = jnp.arange(nb, dtype=I32) * MOE_BLOCK
    block_expert = jnp.minimum(jnp.sum(ends[None, :] <= block_start[:, None], axis=1), N_EXPERTS - 1).astype(I32)
    n_used = (ends[N_EXPERTS - 1:] // MOE_BLOCK).astype(I32)
    pos_t = _slot_rows(eidx_t, rank_t, (ends - padded).astype(I32).reshape(N_EXPERTS, 1))
    pos = pos_t.T
    tok_row = jnp.broadcast_to(jnp.arange(n, dtype=I32)[:, None] * TOKEN_ROWS, (n, TOP_K))
    tok_pad = jnp.zeros((nb * MOE_BLOCK,), I32).at[pos.reshape(-1)].set(tok_row.reshape(-1), unique_indices=True)
    y_pad = _experts(block_expert, n_used, tok_pad.reshape(nb, 1, MOE_BLOCK), x_tiles, w1, w3, w2, layer)
    pos_blocks = (pos * TOKEN_ROWS).reshape(n // COMBINE_TILE, 1, TOP_K * COMBINE_TILE)
    return _combine_ln(pos_blocks, y_pad, x, gate_t.T, g, b)


def kernel(x_prompt, x_sample, cache_fox_k, cache_fox_v, cache_fox_logf, state_rwkv, state_rwkv_shift, w_in, rwkv_mu, rwkv_w0, rwkv_w2, rwkv_a0, rwkv_a2, rwkv_g2, rwkv_k_k, rwkv_k_a, rwkv_r_k, rwkv_lnx_w, rwkv_lnx_b, fox_b_f, fox_q_g, fox_k_g, w_out, ln1_g, ln1_b, ln2_g, ln2_b, router_w, router_b, moe_w1, moe_w3, moe_w2):
    nb_p, seq, _ = x_prompt.shape
    nb_s, dec, _ = x_sample.shape
    depth = w_in.shape[0]
    n_p, n_s = nb_p * seq, nb_s * dec

    x = (x_prompt.reshape(n_p, D_MODEL), x_sample.reshape(n_s, D_MODEL))
    logf_rows = cache_fox_logf.transpose(0, 1, 3, 2)
    cache_k = cache_fox_k.transpose(0, 1, 3, 4, 2)
    cache_v = cache_fox_v.transpose(0, 1, 3, 4, 2)
    fox0 = RWKV_COLS
    fl0 = fox0 + 3 * D_F
    w_in_b = jnp.concatenate(
        [w_in[:, :, :fl0], w_in[:, :, fl0 + N_HEADS:], w_in[:, :, fl0:fl0 + N_HEADS],
         jnp.zeros((depth, D_MODEL, FL_PAD - N_HEADS), F32)], axis=-1).astype(BF16)
    w_out_b = w_out.astype(BF16)
    rw_t = router_w.T
    rb_col = router_b.reshape(N_EXPERTS, 1)
    head_of = jnp.arange(D_F, dtype=I32) // HEAD_DIM
    ones_bd = (head_of[:, None] == head_of[None, :]).astype(BF16)
    slot_consts = _slot_constants()
    zero_shift = jnp.zeros((nb_p, 1, RWKV_COLS), F32)
    zero_state = jnp.zeros((nb_p, N_HEADS, HEAD_DIM, HEAD_DIM), F32)
    row = lambda v: v.reshape(1, -1)

    outs = {k: [] for k in ('pk', 'pv', 'pl', 'pr', 'ps', 'sk', 'sv', 'sl', 'sr', 'ss')}
    for l in range(depth):
        lp = dict(mu=row(rwkv_mu[l]), w0=row(rwkv_w0[l]), w2=rwkv_w2[l], a0=row(rwkv_a0[l]), a2=rwkv_a2[l],
                  g2=rwkv_g2[l], k_k=row(rwkv_k_k[l]), k_a=row(rwkv_k_a[l]), r_k=row(rwkv_r_k[l]),
                  lnx_w=row(rwkv_lnx_w[l]), lnx_b=row(rwkv_lnx_b[l]), ones_bd=ones_bd)
        if l == 0:
            proj_p = _in_proj(x[0], w_in_b[l], 0, n_p)
            proj_s = _in_proj(x[1], w_in_b[l], 0, n_s)
        else:
            proj_p = _in_proj(x, w_in_b[l], 0, n_p)
            proj_s = _in_proj(x, w_in_b[l], n_p, n_s)
        pr_p, q_p, k_p, v_p, og_p, fl_p = proj_p
        pr_s, q_s, k_s, v_s, og_s, fl_s = proj_s

        ry_p, st_p, sh_p = _rwkv(pr_p, zero_shift, zero_state, lp, nb_p, seq, 0)
        ry_s, st_s, sh_s = _rwkv(pr_s, state_rwkv_shift[l], state_rwkv[l], lp, nb_s, dec, 0)

        b_f = jnp.concatenate([fox_b_f[l], jnp.zeros((FL_PAD - N_HEADS,), F32)]).reshape(1, FL_PAD)
        q_gain, k_gain = row(jnp.tile(fox_q_g[l], N_HEADS)), row(jnp.tile(fox_k_g[l], N_HEADS))
        lf_p, c_p = _logf_cumsum(fl_p, b_f, nb_p, seq, 0)
        lf_s, c_s = _logf_cumsum(fl_s, b_f, nb_s, dec, 0)
        qa_p, kn_p, ka_p, vb_p = _fox_prep(q_p, k_p, v_p, c_p, q_gain, k_gain, ones_bd, slot_consts)
        qa_s, kn_s, ka_s, vb_s = _fox_prep(q_s, k_s, v_s, c_s, q_gain, k_gain, ones_bd, slot_consts)
        fo_p = _attn_prompt(qa_p, ka_p, vb_p, nb_p, seq)
        fo_s = _attn_sample(qa_s, ka_s, vb_s, cache_k, cache_v, logf_rows, l, nb_s, dec, 0)

        x1, x1_tiles = _out_proj_ln((ry_p, ry_s), (fo_p, fo_s), (og_p, og_s), x, w_out_b[l],
                                    row(ln1_g[l]), row(ln1_b[l]))
        x = _grouped_moe_ln(x1, x1_tiles, rw_t, rb_col, moe_w1, moe_w3, moe_w2, l, row(ln2_g[l]), row(ln2_b[l]))

        outs['pk'].append(kn_p.reshape(nb_p, seq, N_HEADS, HEAD_DIM))
        outs['pv'].append(v_p.reshape(nb_p, seq, N_HEADS, HEAD_DIM))
        outs['pl'].append(lf_p[:, :N_HEADS].reshape(nb_p, seq, N_HEADS))
        outs['pr'].append(st_p)
        outs['ps'].append(sh_p)
        outs['sk'].append(kn_s.reshape(nb_s, dec, N_HEADS, HEAD_DIM))
        outs['sv'].append(v_s.reshape(nb_s, dec, N_HEADS, HEAD_DIM))
        outs['sl'].append(lf_s[:, :N_HEADS].reshape(nb_s, dec, N_HEADS))
        outs['sr'].append(st_s)
        outs['ss'].append(sh_s)

    stk = lambda key: jnp.stack(outs[key], axis=0)
    return (x[:n_p].reshape(nb_p, seq, D_MODEL), x[n_p:].reshape(nb_s, dec, D_MODEL),
            stk('pk'), stk('pv'), stk('pl'), stk('pr'), stk('ps'),
            stk('sk'), stk('sv'), stk('sl'), stk('sr'), stk('ss'))
```

```python
import functools

import jax
import jax.numpy as jnp
from jax import lax
from jax.experimental import pallas as pl
from jax.experimental.pallas import tpu as pltpu

F32 = jnp.float32
BF16 = jnp.bfloat16
I32 = jnp.int32

D_MODEL = 1024
HEAD_DIM = 64
N_HEADS = 8
D_R = N_HEADS * HEAD_DIM
D_F = N_HEADS * HEAD_DIM
DECAY_LORA = 64
A_LORA = 64
G_LORA = 128
RWKV_COLS = 3 * D_R + DECAY_LORA + A_LORA + G_LORA
FL_PAD = 128
IN_COLS_PAD = RWKV_COLS + 4 * D_F + FL_PAD
DEPTH = 2
N_EXPERTS = 32
N_GROUPS = 4
EXPERTS_PER_GROUP = N_EXPERTS // N_GROUPS
TOP_K = 2
D_EXPERT = D_MODEL // 2
MOE_BLOCK = 256
ALPHA = (2 * DEPTH) ** 0.25
LN_EPS = 1e-5
GN_EPS = 64e-5
QK_EPS = 1e-6
SCALE = HEAD_DIM ** -0.5
RWKV_CHUNK = 64
INV_BASE = 2
VMEM_LIMIT = 48 * 1024 * 1024

_NN = (((1,), (0,)), ((), ()))
_NT = (((1,), (1,)), ((), ()))
_TN = (((0,), (0,)), ((), ()))


def _dot(a, b, dims=_NN):
    return lax.dot_general(a, b, dims, preferred_element_type=F32)


def _split2(x):
    hi = x.astype(BF16)
    lo = (x - hi.astype(F32)).astype(BF16)
    return hi, lo


def _split3(x):
    hi = x.astype(BF16)
    r = x - hi.astype(F32)
    mid = r.astype(BF16)
    lo = (r - mid.astype(F32)).astype(BF16)
    return hi, mid, lo


def _dot3(a, b, dims=_NN):
    ah, al = _split2(a)
    bh, bl = _split2(b)
    return _dot(ah, bh, dims) + (_dot(ah, bl, dims) + _dot(al, bh, dims))


def _dot_exact_lhs(a_bf16, x, dims=_NN):
    hi, mid, lo = _split3(x)
    return _dot(a_bf16, hi, dims) + (_dot(a_bf16, mid, dims) + _dot(a_bf16, lo, dims))


def _sigmoid(x):
    return 1.0 / (1.0 + jnp.exp(-x))


def _softplus(x):
    return jnp.maximum(x, 0.0) + jnp.log(1.0 + jnp.exp(-jnp.abs(x)))


def _layer_norm(z, g, b):
    mu = jnp.mean(z, axis=-1, keepdims=True)
    zc = z - mu
    var = jnp.mean(zc * zc, axis=-1, keepdims=True)
    return zc * lax.rsqrt(var + LN_EPS) * g + b


def _params(sem):
    return pltpu.CompilerParams(dimension_semantics=sem, vmem_limit_bytes=VMEM_LIMIT)


_IN_SPLITS = (RWKV_COLS, D_F, D_F, D_F, D_F, FL_PAD)


def _in_proj_body(x_ref, w_ref, *out_refs):
    x = x_ref[...].astype(BF16)
    col = 0
    for ref, width in zip(out_refs, _IN_SPLITS):
        for c0 in range(0, width, 512):
            c1 = min(c0 + 512, width)
            ref[:, c0:c1] = _dot(x, w_ref[:, col + c0:col + c1])
        col += width


def _in_proj(x, w, row0, n):
    tm = 256
    blk0 = row0 // tm
    return pl.pallas_call(
        _in_proj_body,
        grid=(n // tm,),
        in_specs=[pl.BlockSpec((tm, D_MODEL), lambda i: (blk0 + i, 0)),
                  pl.BlockSpec((D_MODEL, IN_COLS_PAD), lambda i: (0, 0))],
        out_specs=[pl.BlockSpec((tm, wd), lambda i: (i, 0)) for wd in _IN_SPLITS],
        out_shape=[jax.ShapeDtypeStruct((n, wd), F32) for wd in _IN_SPLITS],
        compiler_params=_params(("parallel",)),
        name="in_proj",
    )(x, w)


GROUP = 4
GROUP_W = GROUP * HEAD_DIM
RWKV_SUB = 4


def _block_diag(x, keep):
    return jnp.where(keep, jnp.concatenate([x] * GROUP, axis=0), jnp.zeros((), x.dtype))


def _heads_mm(a, b, keep, dims=_NN, exact=True):
    if not exact:
        return _dot(a.astype(BF16), _block_diag(b.astype(BF16), keep), dims)
    ah, al = _split2(a)
    bh, bl = _split2(b)
    dh = _block_diag(bh, keep)
    dl = _block_diag(bl, keep)
    return _dot(ah, dh, dims) + (_dot(ah, dl, dims) + _dot(al, dh, dims))


def _fold_heads(f, keep):
    f = jnp.where(keep, f, 0.0)
    return (f[0:HEAD_DIM] + f[HEAD_DIM:2 * HEAD_DIM]) + (f[2 * HEAD_DIM:3 * HEAD_DIM] + f[3 * HEAD_DIM:4 * HEAD_DIM])


def _unit_lower_inverses(ls, c_len, t_row, t_col, keep_tt):
    shift = INV_BASE.bit_length() - 1
    same = (t_row >> shift) == (t_col >> shift)
    eye = jnp.where(t_row == t_col, 1.0, 0.0)
    p = [jnp.where(same, -l, 0.0) for l in ls]
    x = [eye + n for n in p]
    for _ in range(shift - 1):
        p = [_heads_mm(pi, pi, keep_tt) for pi in p]
        x = [xi + _heads_mm(xi, pi, keep_tt) for xi, pi in zip(x, p)]
    size = 2 * INV_BASE
    while size <= c_len:
        s_hi = size.bit_length() - 1
        off = ((t_row >> s_hi) == (t_col >> s_hi)) & ((t_row >> (s_hi - 1)) != (t_col >> (s_hi - 1)))
        xq = [_heads_mm(xi, jnp.where(off, l, 0.0), keep_tt, exact=False) for xi, l in zip(x, ls)]
        x = [xi - _heads_mm(xqi, xi, keep_tt, exact=False) for xi, xqi in zip(x, xq)]
        size *= 2
    return x


def _rwkv_body(c_len, n_sub, n_steps, pr_ref, sp_ref, s0_ref, mu_ref, w0_ref, w2_ref, a0_ref, a2_ref, g2_ref,
               kk_ref, ka_ref, rk_ref, lnw_ref, lnb_ref, ones_ref, out_ref, sout_ref, shift_ref, carry_ref, s_ref):
    ci = pl.program_id(1)
    rows = c_len * n_sub
    n_groups = N_HEADS // GROUP

    @pl.when(ci == 0)
    def _():
        carry_ref[...] = sp_ref[0]
        for gi in range(n_groups):
            s_ref[gi] = jnp.concatenate([s0_ref[0, GROUP * gi + j] for j in range(GROUP)], axis=-1)

    pr = pr_ref[...]
    trow = lax.broadcasted_iota(I32, (rows, 1), 0)
    prev = jnp.where(trow == 0, carry_ref[...], pltpu.roll(pr, 1, 0))
    carry_ref[...] = pr[rows - 1:rows, :]
    xs = pr + (prev - pr) * mu_ref[...]
    xr = xs[:, 0:D_R]
    xk = xs[:, D_R:2 * D_R]
    xv = xs[:, 2 * D_R:3 * D_R]
    o = 3 * D_R
    xw = xs[:, o:o + DECAY_LORA]
    xa = xs[:, o + DECAY_LORA:o + DECAY_LORA + A_LORA]
    xg = xs[:, o + DECAY_LORA + A_LORA:RWKV_COLS]

    z = w0_ref[...] + _dot3(jnp.tanh(xw), w2_ref[...])
    lw = -jnp.exp(-_softplus(-z) - 0.5)
    a = _sigmoid(a0_ref[...] + _dot3(xa, a2_ref[...]))
    g = _dot3(_sigmoid(xg), g2_ref[...])
    kk_raw = xk * kk_ref[...]
    k_mod = xk * (1.0 + (a - 1.0) * ka_ref[...])

    shift = c_len.bit_length() - 1
    r2 = lax.broadcasted_iota(I32, (rows, rows), 0)
    c2 = lax.broadcasted_iota(I32, (rows, rows), 1)
    within = ((r2 >> shift) == (c2 >> shift)) & (r2 >= c2)
    cl = _dot_exact_lhs(jnp.where(within, 1.0, 0.0).astype(BF16), lw)

    ones = ones_ref[...]

    def head_sum(x):
        hi, lo = _split2(x)
        return _dot(hi, ones) + _dot(lo, ones)

    kk = kk_raw / jnp.maximum(jnp.sqrt(head_sum(kk_raw * kk_raw)), 1e-12)
    eg = jnp.exp(cl)
    e_inv = jnp.exp(-cl)
    r_dec = xr * eg
    kk_dec = kk * jnp.exp(cl - lw)
    b_und = kk * a * e_inv
    k_und = k_mod * e_inv

    wt = GROUP * c_len
    t_row = lax.broadcasted_iota(I32, (c_len, wt), 0)
    t_col = lax.broadcasted_iota(I32, (c_len, wt), 1) & (c_len - 1)
    strict = t_row > t_col
    incl = t_row >= t_col
    hd_shift = HEAD_DIM.bit_length() - 1
    keep_tt = (lax.broadcasted_iota(I32, (wt, wt), 0) >> shift) == (lax.broadcasted_iota(I32, (wt, wt), 1) >> shift)
    keep_tf = (lax.broadcasted_iota(I32, (wt, GROUP_W), 0) >> shift) == (
        lax.broadcasted_iota(I32, (wt, GROUP_W), 1) >> hd_shift)
    keep_ff = (lax.broadcasted_iota(I32, (GROUP_W, GROUP_W), 0) >> hd_shift) == (
        lax.broadcasted_iota(I32, (GROUP_W, GROUP_W), 1) >> hd_shift)

    chains = [(c, gi) for c in range(n_sub) for gi in range(n_groups)]
    cut = lambda x, c, gi: x[c * c_len:(c + 1) * c_len, gi * GROUP_W:(gi + 1) * GROUP_W]
    lhs = [jnp.concatenate([cut(kk_dec, c, gi), cut(r_dec, c, gi)], axis=0) for c, gi in chains]
    bu = [cut(b_und, c, gi) for c, gi in chains]
    ku = [cut(k_und, c, gi) for c, gi in chains]
    vh = [cut(xv, c, gi) for c, gi in chains]
    n_ch = range(len(chains))
    gb = [_heads_mm(lhs[i], bu[i], keep_tf, _NT) for i in n_ch]
    gk = [_heads_mm(lhs[i], ku[i], keep_tf, _NT, exact=False) for i in n_ch]
    l_b = [jnp.where(strict, m[:c_len], 0.0) for m in gb]
    m_rb = [jnp.where(incl, m[c_len:], 0.0) for m in gb]
    l_k = [jnp.where(strict, m[:c_len], 0.0) for m in gk]
    m_rk = [jnp.where(incl, m[c_len:], 0.0) for m in gk]
    t_inv = _unit_lower_inverses(l_b, c_len, t_row, t_col, keep_tt)
    lkv = [_heads_mm(l_k[i], vh[i], keep_tf, exact=False) for i in n_ch]
    mkv = [_heads_mm(m_rk[i], vh[i], keep_tf, exact=False) for i in n_ch]
    fv = [_fold_heads(_dot(vh[i].astype(BF16), ku[i].astype(BF16), _TN), keep_ff) for i in n_ch]

    state = [s_ref[gi] for gi in range(n_groups)]
    y_rows = []
    for c in range(n_sub):
        ids = [c * n_groups + gi for gi in range(n_groups)]
        ps = [_heads_mm(lhs[i], state[gi], keep_ff, _NT, exact=False) for gi, i in enumerate(ids)]
        u = [-_heads_mm(t_inv[i], ps[gi][:c_len] + lkv[i], keep_tf, exact=False) for gi, i in enumerate(ids)]
        y = [ps[gi][c_len:] + mkv[i] + _heads_mm(m_rb[i], u[gi], keep_tf, exact=False) for gi, i in enumerate(ids)]
        fu = [_fold_heads(_dot(u[gi].astype(BF16), bu[i].astype(BF16), _TN), keep_ff) for gi, i in enumerate(ids)]
        last = c * c_len + c_len - 1
        state = [(state[gi] + fv[i] + fu[gi]) * eg[last:last + 1, gi * GROUP_W:(gi + 1) * GROUP_W]
                 for gi, i in enumerate(ids)]
        y_rows.append(jnp.concatenate(y, axis=-1))
    for gi in range(n_groups):
        s_ref[gi] = state[gi]

    y = jnp.concatenate(y_rows, axis=0)
    yc = y - head_sum(y) * (1.0 / HEAD_DIM)
    var = head_sum(yc * yc) * (1.0 / HEAD_DIM)
    yn = yc * lax.rsqrt(var + GN_EPS) * lnw_ref[...] + lnb_ref[...]
    bonus = head_sum(xr * k_mod * rk_ref[...]) * xv
    out_ref[...] = (yn + bonus) * g

    @pl.when(ci == n_steps - 1)
    def _():
        for gi in range(n_groups):
            for j in range(GROUP):
                sout_ref[0, GROUP * gi + j] = state[gi][:, j * HEAD_DIM:(j + 1) * HEAD_DIM]
        shift_ref[0] = pr[rows - 1:rows, :]


def _rwkv(pr, shift_prev, s0, lp, n_seq, seq_len, row0):
    c_len = min(RWKV_CHUNK, seq_len)
    n_sub = min(RWKV_SUB, seq_len // c_len)
    rows = c_len * n_sub
    n_steps = seq_len // rows
    blk0 = row0 // rows
    vec = lambda wd: pl.BlockSpec((1, wd), lambda b, c: (0, 0))
    mat = lambda r, wd: pl.BlockSpec((r, wd), lambda b, c: (0, 0))
    in_specs = [
        pl.BlockSpec((rows, RWKV_COLS), lambda b, c: (blk0 + b * n_steps + c, 0)),
        pl.BlockSpec((1, 1, RWKV_COLS), lambda b, c: (b, 0, 0)),
        pl.BlockSpec((1, N_HEADS, HEAD_DIM, HEAD_DIM), lambda b, c: (b, 0, 0, 0)),
        vec(RWKV_COLS), vec(D_R), mat(DECAY_LORA, D_R), vec(D_R), mat(A_LORA, D_R), mat(G_LORA, D_R),
        vec(D_R), vec(D_R), vec(D_R), vec(D_R), vec(D_R), mat(D_R, D_R),
    ]
    args = [pr, shift_prev, s0, lp['mu'], lp['w0'], lp['w2'], lp['a0'], lp['a2'], lp['g2'],
            lp['k_k'], lp['k_a'], lp['r_k'], lp['lnx_w'], lp['lnx_b'], lp['ones_bd']]
    return pl.pallas_call(
        functools.partial(_rwkv_body, c_len, n_sub, n_steps),
        grid=(n_seq, n_steps),
        in_specs=in_specs,
        out_specs=[pl.BlockSpec((rows, D_R), lambda b, c: (b * n_steps + c, 0)),
                   pl.BlockSpec((1, N_HEADS, HEAD_DIM, HEAD_DIM), lambda b, c: (b, 0, 0, 0)),
                   pl.BlockSpec((1, 1, RWKV_COLS), lambda b, c: (b, 0, 0))],
        out_shape=[jax.ShapeDtypeStruct((n_seq * seq_len, D_R), F32),
                   jax.ShapeDtypeStruct((n_seq, N_HEADS, HEAD_DIM, HEAD_DIM), F32),
                   jax.ShapeDtypeStruct((n_seq, 1, RWKV_COLS), F32)],
        scratch_shapes=[pltpu.VMEM((1, RWKV_COLS), F32), pltpu.VMEM((N_HEADS // GROUP, HEAD_DIM, GROUP_W), F32)],
        compiler_params=_params(("arbitrary", "arbitrary")),
        name="rwkv_mixer",
    )(*args)


SLOT = 2 * HEAD_DIM
C_LANE = HEAD_DIM


def _fox_prep_body(q_ref, k_ref, v_ref, c_ref, qg_ref, kg_ref, ones_ref, place_ref, pcq_ref, pck_ref, oneq_ref,
                   onek_ref, qa_ref, kn_ref, ka_ref, vb_ref):
    ones = ones_ref[...]

    def rms(x, gain):
        hi, lo = _split2(x * x)
        ss = _dot(hi, ones) + _dot(lo, ones)
        return x * lax.rsqrt(ss * (1.0 / HEAD_DIM) + QK_EPS) * gain

    qn = rms(q_ref[...], qg_ref[...]) * SCALE
    kn = rms(k_ref[...], kg_ref[...])
    kn_ref[...] = kn
    vb_ref[...] = v_ref[...].astype(BF16)
    c_parts = _split3(c_ref[...])

    def slots(xb, pc_ref, one_ref):
        acc = _dot(xb, place_ref[...]) + one_ref[...]
        for j in range(3):
            acc = acc + _dot(c_parts[j], pc_ref[j])
        return acc.astype(BF16)

    qa_ref[...] = slots(qn.astype(BF16), pcq_ref, oneq_ref)
    ka_ref[...] = slots(kn.astype(BF16), pck_ref, onek_ref)


def _slot_constants():
    d = jnp.arange(D_F, dtype=I32)
    lane = jnp.arange(N_HEADS * SLOT, dtype=I32)
    place = (lane[None, :] == (d // HEAD_DIM * SLOT + d % HEAD_DIM)[:, None]).astype(BF16)
    h = jnp.arange(FL_PAD, dtype=I32)[None, :, None]
    j = jnp.arange(3, dtype=I32)[:, None, None]
    is_head = h < N_HEADS
    pcq = ((lane[None, None, :] == h * SLOT + C_LANE + j) & is_head).astype(BF16)
    pck = -((lane[None, None, :] == h * SLOT + C_LANE + 3 + j) & is_head).astype(BF16)
    in_slot = lane % SLOT
    oneq = ((in_slot >= C_LANE + 3) & (in_slot < C_LANE + 6)).astype(F32).reshape(1, -1)
    onek = ((in_slot >= C_LANE) & (in_slot < C_LANE + 3)).astype(F32).reshape(1, -1)
    return place, pcq, pck, oneq, onek


def _fox_prep(q, k, v, c, q_gain, k_gain, ones_bd, slot_consts):
    n = q.shape[0]
    tm = 256
    wide = N_HEADS * SLOT
    row = lambda wd: pl.BlockSpec((tm, wd), lambda i: (i, 0))
    vec = lambda wd: pl.BlockSpec((1, wd), lambda i: (0, 0))
    full = lambda *shape: pl.BlockSpec(shape, lambda i: (0,) * len(shape))
    return pl.pallas_call(
        _fox_prep_body,
        grid=(n // tm,),
        in_specs=[row(D_F), row(D_F), row(D_F), row(FL_PAD), vec(D_F), vec(D_F), full(D_F, D_F),
                  full(D_F, wide), full(3, FL_PAD, wide), full(3, FL_PAD, wide), vec(wide), vec(wide)],
        out_specs=[row(wide), row(D_F), row(wide), row(D_F)],
        out_shape=[jax.ShapeDtypeStruct((n, wide), BF16), jax.ShapeDtypeStruct((n, D_F), F32),
                   jax.ShapeDtypeStruct((n, wide), BF16), jax.ShapeDtypeStruct((n, D_F), BF16)],
        compiler_params=_params(("parallel",)),
        name="fox_prep",
    )(q, k, v, c, q_gain, k_gain, ones_bd, *slot_consts)


def _logf_cumsum_body(fl_ref, bf_ref, lf_ref, c_ref, carry_ref):
    @pl.when(pl.program_id(1) == 0)
    def _():
        carry_ref[...] = jnp.zeros_like(carry_ref)

    lf = -_softplus(-(fl_ref[...] + bf_ref[...]))
    lf_ref[...] = lf
    t = lf.shape[0]
    row = lax.broadcasted_iota(I32, (t, t), 0)
    col = lax.broadcasted_iota(I32, (t, t), 1)
    cs = _dot_exact_lhs(jnp.where(row >= col, 1.0, 0.0).astype(BF16), lf) + carry_ref[...]
    c_ref[...] = cs
    carry_ref[...] = cs[t - 1:t, :]


def _logf_cumsum(fl, b_f, n_seq, seq_len, row0):
    tc = min(seq_len, 256)
    nt = seq_len // tc
    blk0 = row0 // tc
    out = pl.BlockSpec((tc, FL_PAD), lambda b, j: (b * nt + j, 0))
    return pl.pallas_call(
        _logf_cumsum_body,
        grid=(n_seq, nt),
        in_specs=[pl.BlockSpec((tc, FL_PAD), lambda b, j: (blk0 + b * nt + j, 0)),
                  pl.BlockSpec((1, FL_PAD), lambda b, j: (0, 0))],
        out_specs=[out, out],
        out_shape=[jax.ShapeDtypeStruct((n_seq * seq_len, FL_PAD), F32)] * 2,
        scratch_shapes=[pltpu.VMEM((1, FL_PAD), F32)],
        compiler_params=_params(("arbitrary", "arbitrary")),
        name="logf_cumsum",
    )(fl, b_f)


ATT_TILE = 512


def _attn_prompt_body(q_ref, k_ref, v_ref, o_ref):
    i = pl.program_id(2)
    t = ATT_TILE
    row = lax.broadcasted_iota(I32, (t, t), 0)
    col = lax.broadcasted_iota(I32, (t, t), 1)
    causal = row >= col
    pair = range(2)
    q = [q_ref[:, hh * SLOT:(hh + 1) * SLOT] for hh in pair]

    def tile(j, carry, masked):
        m, l, acc = carry
        j0 = pl.multiple_of(j * t, t)
        s = [_dot(q[hh], k_ref[pl.ds(j0, t), hh * SLOT:(hh + 1) * SLOT], _NT) for hh in pair]
        if masked:
            s = [jnp.where(causal, sh, -jnp.inf) for sh in s]
        m_new = [jnp.maximum(m[hh], jnp.max(s[hh], axis=-1, keepdims=True)) for hh in pair]
        alpha = [jnp.exp(m[hh] - m_new[hh]) for hh in pair]
        p = [jnp.exp(s[hh] - m_new[hh]) for hh in pair]
        l = [alpha[hh] * l[hh] + jnp.sum(p[hh], axis=-1, keepdims=True) for hh in pair]
        pv = [_dot(p[hh].astype(BF16), v_ref[pl.ds(j0, t), hh * HEAD_DIM:(hh + 1) * HEAD_DIM]) for hh in pair]
        acc = [alpha[hh] * acc[hh] + pv[hh] for hh in pair]
        return m_new, l, acc

    init = ([jnp.full((t, 1), -jnp.inf, F32)] * 2, [jnp.zeros((t, 1), F32)] * 2,
            [jnp.zeros((t, HEAD_DIM), F32)] * 2)
    carry = lax.fori_loop(0, i, lambda j, c: tile(j, c, False), init)
    _, l, acc = tile(i, carry, True)
    o_ref[...] = jnp.concatenate([acc[hh] / l[hh] for hh in pair], axis=-1)


def _attn_prompt(qa, ka, vb, n_seq, seq_len):
    t = ATT_TILE
    nq = seq_len // t
    return pl.pallas_call(
        _attn_prompt_body,
        grid=(n_seq, N_HEADS // 2, nq),
        in_specs=[pl.BlockSpec((t, 2 * SLOT), lambda b, p, i: (b * nq + i, p)),
                  pl.BlockSpec((seq_len, 2 * SLOT), lambda b, p, i: (b, p)),
                  pl.BlockSpec((seq_len, 2 * HEAD_DIM), lambda b, p, i: (b, p))],
        out_specs=pl.BlockSpec((t, 2 * HEAD_DIM), lambda b, p, i: (b * nq + i, p)),
        out_shape=jax.ShapeDtypeStruct((n_seq * seq_len, D_F), F32),
        compiler_params=_params(("parallel", "parallel", "arbitrary")),
        name="fox_attn_prompt",
    )(qa, ka, vb)


SAMPLE_CHUNK = 2048
TAIL_BLOCK = 1024


def _attn_sample_body(n_chunks, q_ref, kn_ref, vn_ref, kc_ref, vc_ref, lp_ref, after_ref, o_ref,
                      m_ref, l_ref, acc_ref, suffix_ref):
    j = pl.program_id(1)
    n = q_ref.shape[0]
    heads = range(N_HEADS)

    @pl.when(j == 0)
    def _():
        m_ref[...] = jnp.full(m_ref.shape, -jnp.inf, F32)
        l_ref[...] = jnp.zeros_like(l_ref)
        acc_ref[...] = jnp.zeros_like(acc_ref)
        suffix_ref[...] = jnp.zeros_like(suffix_ref)

    after = after_ref[...]
    tp = lp_ref.shape[3]
    tails = []
    suffix = suffix_ref[...]
    for b0 in range(tp - TAIL_BLOCK, -1, -TAIL_BLOCK):
        lp = lp_ref[0, 0, :, b0:b0 + TAIL_BLOCK]
        hi, mid, lo = _split3(lp)
        tails.insert(0, _dot(hi, after) + (_dot(mid, after) + _dot(lo, after)) + suffix)
        suffix = suffix + jnp.sum(lp, axis=-1, keepdims=True)
    suffix_ref[...] = suffix
    tail = jnp.concatenate(tails, axis=-1)

    q_slot = [q_ref[:, h * SLOT:(h + 1) * SLOT] for h in heads]
    c_col = [(qs[:, C_LANE:C_LANE + 1].astype(F32) + qs[:, C_LANE + 1:C_LANE + 2].astype(F32)
              + qs[:, C_LANE + 2:C_LANE + 3].astype(F32)) for qs in q_slot]
    s = [_dot(q_slot[h][:, :HEAD_DIM], kc_ref[0, 0, h].astype(BF16)) + c_col[h] + tail[h:h + 1, :] for h in heads]
    m_old = [m_ref[h] for h in heads]
    m_new = [jnp.maximum(m_old[h], jnp.max(s[h], axis=-1, keepdims=True)) for h in heads]
    alpha = [jnp.exp(m_old[h] - m_new[h]) for h in heads]
    p = [jnp.exp(s[h] - m_new[h]) for h in heads]
    pv = [_dot(p[h].astype(BF16), vc_ref[0, 0, h].astype(BF16), _NT) for h in heads]
    for h in heads:
        m_ref[h] = m_new[h]
        l_ref[h] = alpha[h] * l_ref[h] + jnp.sum(p[h], axis=-1, keepdims=True)
        acc_ref[h] = alpha[h] * acc_ref[h] + pv[h]

    @pl.when(j == n_chunks - 1)
    def _():
        row = lax.broadcasted_iota(I32, (n, n), 0)
        col = lax.broadcasted_iota(I32, (n, n), 1)
        s_new = [jnp.where(row >= col, _dot(q_slot[h], kn_ref[:, h * SLOT:(h + 1) * SLOT], _NT), -jnp.inf)
                 for h in heads]
        m_fin = [jnp.maximum(m_ref[h], jnp.max(s_new[h], axis=-1, keepdims=True)) for h in heads]
        a_fin = [jnp.exp(m_ref[h] - m_fin[h]) for h in heads]
        p_new = [jnp.exp(s_new[h] - m_fin[h]) for h in heads]
        l_fin = [a_fin[h] * l_ref[h] + jnp.sum(p_new[h], axis=-1, keepdims=True) for h in heads]
        acc = [a_fin[h] * acc_ref[h] + _dot(p_new[h].astype(BF16), vn_ref[:, h * HEAD_DIM:(h + 1) * HEAD_DIM])
               for h in heads]
        o_ref[...] = jnp.concatenate([acc[h] / l_fin[h] for h in heads], axis=-1)


def _attn_sample(qa, ka, vb, cache_k, cache_v, logf_rows, layer, n_seq, n_new, row0):
    past = cache_k.shape[4]
    tp = min(SAMPLE_CHUNK, past)
    n_chunks = past // tp
    blk0 = row0 // n_new
    frame = jnp.arange(TAIL_BLOCK, dtype=I32)
    after = (frame[:, None] > frame[None, :]).astype(BF16)
    rows = lambda wd: pl.BlockSpec((n_new, wd), lambda b, j: (blk0 + b, 0))
    cache = lambda: pl.BlockSpec((1, 1, N_HEADS, HEAD_DIM, tp), lambda b, j: (layer, b, 0, 0, n_chunks - 1 - j))
    return pl.pallas_call(
        functools.partial(_attn_sample_body, n_chunks),
        grid=(n_seq, n_chunks),
        in_specs=[rows(N_HEADS * SLOT), rows(N_HEADS * SLOT), rows(D_F), cache(), cache(),
                  pl.BlockSpec((1, 1, N_HEADS, tp), lambda b, j: (layer, b, 0, n_chunks - 1 - j)),
                  pl.BlockSpec((TAIL_BLOCK, TAIL_BLOCK), lambda b, j: (0, 0))],
        out_specs=pl.BlockSpec((n_new, D_F), lambda b, j: (b, 0)),
        out_shape=jax.ShapeDtypeStruct((n_seq * n_new, D_F), F32),
        scratch_shapes=[pltpu.VMEM((N_HEADS, n_new, 1), F32), pltpu.VMEM((N_HEADS, n_new, 1), F32),
                        pltpu.VMEM((N_HEADS, n_new, HEAD_DIM), F32), pltpu.VMEM((N_HEADS, 1), F32)],
        compiler_params=_params(("parallel", "arbitrary")),
        name="fox_attn_sample",
    )(qa, ka, vb, cache_k, cache_v, logf_rows, after)


LANES = 128
TOKEN_ROWS = D_MODEL // LANES


def _store_token_tiles(ref, x):
    m = x.shape[0]
    for c in range(TOKEN_ROWS):
        ref[pl.ds(c, m, stride=TOKEN_ROWS), :] = x[:, c * LANES:(c + 1) * LANES]


def _load_token_tiles(ref, m):
    return jnp.concatenate([ref[pl.ds(c, m, stride=TOKEN_ROWS), :] for c in range(TOKEN_ROWS)], axis=-1)


def _out_proj_body(tiles_p, n_x, *refs):
    ryp_ref, rys_ref, fop_ref, fos_ref, ogp_ref, ogs_ref = refs[:6]
    x_refs = refs[6:6 + n_x]
    w_ref, g_ref, b_ref, o_ref, ot_ref = refs[6 + n_x:]
    from_prompt = pl.program_id(0) < tiles_p
    pick = lambda p_ref, s_ref: jnp.where(from_prompt, p_ref[...], s_ref[...])
    ry = pick(ryp_ref, rys_ref).astype(BF16)
    fy = (pick(fop_ref, fos_ref) * _sigmoid(pick(ogp_ref, ogs_ref))).astype(BF16)
    x = pick(*x_refs) if n_x == 2 else x_refs[0][...]
    m = _dot(ry, w_ref[0:D_R, :]) + _dot(fy, w_ref[D_R:D_R + D_F, :])
    out = _layer_norm(ALPHA * x + m, g_ref[...], b_ref[...])
    o_ref[...] = out
    _store_token_tiles(ot_ref, out)


def _out_proj_ln(ry, fo, og, x, w, g, b):
    tm = 256
    tiles_p = ry[0].shape[0] // tm
    n = ry[0].shape[0] + ry[1].shape[0]
    row = lambda wd: pl.BlockSpec((tm, wd), lambda i: (i, 0))
    row_p = lambda wd: pl.BlockSpec((tm, wd), lambda i: (jnp.minimum(i, tiles_p - 1), 0))
    row_s = lambda wd: pl.BlockSpec((tm, wd), lambda i: (jnp.maximum(i - tiles_p, 0), 0))
    pair = lambda wd: [row_p(wd), row_s(wd)]
    vec = pl.BlockSpec((1, D_MODEL), lambda i: (0, 0))
    x_parts = tuple(x) if isinstance(x, (tuple, list)) else (x,)
    x_specs = pair(D_MODEL) if len(x_parts) == 2 else [row(D_MODEL)]
    return pl.pallas_call(
        functools.partial(_out_proj_body, tiles_p, len(x_parts)),
        grid=(n // tm,),
        in_specs=pair(D_R) + pair(D_F) + pair(D_F) + x_specs + [
            pl.BlockSpec((D_R + D_F, D_MODEL), lambda i: (0, 0)), vec, vec],
        out_specs=[row(D_MODEL), pl.BlockSpec((tm * TOKEN_ROWS, LANES), lambda i: (i, 0))],
        out_shape=[jax.ShapeDtypeStruct((n, D_MODEL), F32), jax.ShapeDtypeStruct((n * TOKEN_ROWS, LANES), F32)],
        compiler_params=_params(("parallel",)),
        name="out_proj_ln",
    )(*ry, *fo, *og, *x_parts, w, g, b)


def _router_body(x_ref, rw_ref, rb_ref, earlier_ref, e_ref, g_ref, r_ref, count_ref, seen_ref):
    tn = x_ref.shape[0]
    scores = _sigmoid(_dot3(rw_ref[...], x_ref[...], _NT))
    sel = scores + rb_ref[...]
    sel4 = sel.reshape(N_GROUPS, EXPERTS_PER_GROUP, tn)
    sc4 = scores.reshape(N_GROUPS, EXPERTS_PER_GROUP, tn)
    lane_e = lax.broadcasted_iota(I32, (N_GROUPS, EXPERTS_PER_GROUP, tn), 1)

    def top2(vals, idx_iota, axis):
        m1 = jnp.max(vals, axis=axis, keepdims=True)
        i1 = jnp.min(jnp.where(vals == m1, idx_iota, EXPERTS_PER_GROUP), axis=axis, keepdims=True)
        rest = jnp.where(idx_iota == i1, -jnp.inf, vals)
        m2 = jnp.max(rest, axis=axis, keepdims=True)
        i2 = jnp.min(jnp.where(rest == m2, idx_iota, EXPERTS_PER_GROUP), axis=axis, keepdims=True)
        return m1, i1, m2, i2

    m1, _, m2, _ = top2(sel4, lane_e, 1)
    gsum = m1 + m2
    g_iota = lax.broadcasted_iota(I32, (N_GROUPS, 1, tn), 0)
    gmax = jnp.max(gsum, axis=0, keepdims=True)
    g_idx = jnp.min(jnp.where(gsum == gmax, g_iota, N_GROUPS), axis=0, keepdims=True)
    pick = g_iota == g_idx
    sel_g = jnp.max(jnp.where(pick, sel4, -jnp.inf), axis=0)
    sc_g = jnp.max(jnp.where(pick, sc4, -jnp.inf), axis=0)
    e_iota = lax.broadcasted_iota(I32, (EXPERTS_PER_GROUP, tn), 0)
    _, i1, _, i2 = top2(sel_g, e_iota, 0)
    gate1 = jnp.sum(jnp.where(e_iota == i1, sc_g, 0.0), axis=0, keepdims=True)
    gate2 = jnp.sum(jnp.where(e_iota == i2, sc_g, 0.0), axis=0, keepdims=True)
    tot = gate1 + gate2
    base = g_idx[0] * EXPERTS_PER_GROUP
    e1 = base + i1
    e2 = base + i2
    e_ref[...] = jnp.concatenate([e1, e2], axis=0)
    g_ref[...] = jnp.concatenate([gate1 / tot, gate2 / tot], axis=0)

    @pl.when(pl.program_id(0) == 0)
    def _():
        seen_ref[...] = jnp.zeros_like(seen_ref)

    all_e = lax.broadcasted_iota(I32, (N_EXPERTS, tn), 0)
    hit1 = all_e == e1
    hit2 = all_e == e2
    hits = jnp.where(hit1 | hit2, 1.0, 0.0)
    before = _dot(hits.astype(BF16), earlier_ref[...]) + seen_ref[...]
    r1 = jnp.sum(jnp.where(hit1, before, 0.0), axis=0, keepdims=True)
    r2 = jnp.sum(jnp.where(hit2, before, 0.0), axis=0, keepdims=True)
    r_ref[...] = jnp.concatenate([r1, r2], axis=0).astype(I32)
    seen_ref[...] = seen_ref[...] + jnp.sum(hits, axis=-1, keepdims=True)
    count_ref[...] = seen_ref[...].astype(I32)


def _router(x, rw_t, rb_col):
    n = x.shape[0]
    tn = 512
    tok = jnp.arange(tn, dtype=I32)
    earlier = (tok[:, None] < tok[None, :]).astype(BF16)
    pair = lambda: pl.BlockSpec((TOP_K, tn), lambda i: (0, i))
    return pl.pallas_call(
        _router_body,
        grid=(n // tn,),
        in_specs=[pl.BlockSpec((tn, D_MODEL), lambda i: (i, 0)),
                  pl.BlockSpec((N_EXPERTS, D_MODEL), lambda i: (0, 0)),
                  pl.BlockSpec((N_EXPERTS, 1), lambda i: (0, 0)),
                  pl.BlockSpec((tn, tn), lambda i: (0, 0))],
        out_specs=[pair(), pair(), pair(), pl.BlockSpec((N_EXPERTS, 1), lambda i: (0, 0))],
        out_shape=[jax.ShapeDtypeStruct((TOP_K, n), I32), jax.ShapeDtypeStruct((TOP_K, n), F32),
                   jax.ShapeDtypeStruct((TOP_K, n), I32), jax.ShapeDtypeStruct((N_EXPERTS, 1), I32)],
        scratch_shapes=[pltpu.VMEM((N_EXPERTS, 1), F32)],
        compiler_params=_params(("arbitrary",)),
        name="router",
    )(x, rw_t, rb_col, earlier)


def _slot_rows_body(e_ref, r_ref, start_ref, pos_ref):
    tn = e_ref.shape[1]
    all_e = lax.broadcasted_iota(I32, (N_EXPERTS, tn), 0)
    rows = [jnp.sum(jnp.where(all_e == e_ref[k:k + 1, :], start_ref[...], 0), axis=0, keepdims=True)
            for k in range(TOP_K)]
    pos_ref[...] = r_ref[...] + jnp.concatenate(rows, axis=0)


def _slot_rows(eidx_t, rank_t, expert_start):
    n = eidx_t.shape[1]
    tn = 512
    pair = lambda: pl.BlockSpec((TOP_K, tn), lambda i: (0, i))
    return pl.pallas_call(
        _slot_rows_body,
        grid=(n // tn,),
        in_specs=[pair(), pair(), pl.BlockSpec((N_EXPERTS, 1), lambda i: (0, 0))],
        out_specs=pair(),
        out_shape=jax.ShapeDtypeStruct((TOP_K, n), I32),
        compiler_params=_params(("parallel",)),
        name="moe_slot_rows",
    )(eidx_t, rank_t, expert_start)


def _row_copy(src_hbm, src_row8, dst, dst_token, sem):
    src = src_hbm.at[pl.ds(pl.multiple_of(src_row8, TOKEN_ROWS), TOKEN_ROWS)]
    return pltpu.make_async_copy(src, dst.at[pl.ds(pl.multiple_of(dst_token * TOKEN_ROWS, TOKEN_ROWS), TOKEN_ROWS)], sem)


def _expert_body(be_ref, used_ref, tok_ref, tok_next_ref, x_hbm, w1_ref, w3_ref, w2_ref, y_ref,
                 buf_ref, w1b_ref, w3b_ref, w2b_ref, sem):
    i = pl.program_id(0)
    n_used = used_ref[0]
    slot = lax.rem(i, 2)

    def start_gather(ids_ref, s):
        def body(r, carry):
            _row_copy(x_hbm, ids_ref[0, 0, r], buf_ref.at[s], r, sem.at[s]).start()
            return carry
        lax.fori_loop(0, MOE_BLOCK, body, 0, unroll=8)

    @pl.when(jnp.logical_and(i == 0, n_used > 0))
    def _():
        start_gather(tok_ref, 0)

    @pl.when(i + 1 < n_used)
    def _():
        start_gather(tok_next_ref, 1 - slot)

    @pl.when(i < n_used)
    def _():
        @pl.when(jnp.logical_or(i == 0, be_ref[i] != be_ref[jnp.maximum(i - 1, 0)]))
        def _():
            w1b_ref[...] = w1_ref[0, 0].astype(BF16)
            w3b_ref[...] = w3_ref[0, 0].astype(BF16)
            w2b_ref[...] = w2_ref[0, 0].astype(BF16)

        def wait(r, carry):
            _row_copy(x_hbm, 0, buf_ref.at[slot], r, sem.at[slot]).wait()
            return carry
        lax.fori_loop(0, MOE_BLOCK, wait, 0, unroll=8)

        h = _load_token_tiles(buf_ref.at[slot], MOE_BLOCK).astype(BF16)
        a = _dot(h, w1b_ref[...])
        b = _dot(h, w3b_ref[...])
        act = (a * _sigmoid(a) * b).astype(BF16)
        _store_token_tiles(y_ref, _dot(act, w2b_ref[...]))

    @pl.when(i >= n_used)
    def _():
        y_ref[...] = jnp.zeros_like(y_ref)


def _experts(block_expert, n_used, tok_blocks, x, w1, w3, w2, layer):
    nb = block_expert.shape[0]
    ids = lambda f: pl.BlockSpec((1, 1, MOE_BLOCK), f, memory_space=pltpu.SMEM)
    grid_spec = pltpu.PrefetchScalarGridSpec(
        num_scalar_prefetch=2,
        grid=(nb,),
        in_specs=[ids(lambda i, be, nu: (i, 0, 0)),
                  ids(lambda i, be, nu: (jnp.minimum(i + 1, nb - 1), 0, 0)),
                  pl.BlockSpec(memory_space=pl.ANY),
                  pl.BlockSpec((1, 1, D_MODEL, D_EXPERT), lambda i, be, nu: (layer, be[i], 0, 0)),
                  pl.BlockSpec((1, 1, D_MODEL, D_EXPERT), lambda i, be, nu: (layer, be[i], 0, 0)),
                  pl.BlockSpec((1, 1, D_EXPERT, D_MODEL), lambda i, be, nu: (layer, be[i], 0, 0))],
        out_specs=pl.BlockSpec((MOE_BLOCK * TOKEN_ROWS, LANES), lambda i, be, nu: (i, 0)),
        scratch_shapes=[pltpu.VMEM((2, MOE_BLOCK * TOKEN_ROWS, LANES), F32),
                        pltpu.VMEM((D_MODEL, D_EXPERT), BF16), pltpu.VMEM((D_MODEL, D_EXPERT), BF16),
                        pltpu.VMEM((D_EXPERT, D_MODEL), BF16), pltpu.SemaphoreType.DMA((2,))],
    )
    return pl.pallas_call(
        _expert_body,
        grid_spec=grid_spec,
        out_shape=jax.ShapeDtypeStruct((nb * MOE_BLOCK * TOKEN_ROWS, LANES), F32),
        compiler_params=_params(("arbitrary",)),
        name="moe_experts",
    )(block_expert, n_used, tok_blocks, tok_blocks, x, w1, w3, w2)


COMBINE_TILE = 128


def _combine_body(pos_ref, pos_next_ref, y_hbm, x_ref, gate_ref, g_ref, b_ref, o_ref, buf_ref, sem):
    i = pl.program_id(0)
    slot = lax.rem(i, 2)

    def start_gather(ids_ref, s):
        def body(t, carry):
            for k in range(TOP_K):
                _row_copy(y_hbm, ids_ref[0, 0, TOP_K * t + k], buf_ref.at[s, k], t, sem.at[s]).start()
            return carry
        lax.fori_loop(0, COMBINE_TILE, body, 0, unroll=4)

    @pl.when(i == 0)
    def _():
        start_gather(pos_ref, 0)

    @pl.when(i + 1 < pl.num_programs(0))
    def _():
        start_gather(pos_next_ref, 1 - slot)

    def wait(t, carry):
        for k in range(TOP_K):
            _row_copy(y_hbm, 0, buf_ref.at[slot, k], t, sem.at[slot]).wait()
        return carry
    lax.fori_loop(0, COMBINE_TILE, wait, 0, unroll=4)

    gate = gate_ref[...]
    y = (gate[:, 0:1] * _load_token_tiles(buf_ref.at[slot, 0], COMBINE_TILE)
         + gate[:, 1:2] * _load_token_tiles(buf_ref.at[slot, 1], COMBINE_TILE))
    o_ref[...] = _layer_norm(ALPHA * x_ref[...] + y, g_ref[...], b_ref[...])


def _combine_ln(pos_blocks, y_pad, x, gate, g, b):
    n = x.shape[0]
    tm = COMBINE_TILE
    nt = n // tm
    vec = pl.BlockSpec((1, D_MODEL), lambda i: (0, 0))
    ids = lambda f: pl.BlockSpec((1, 1, TOP_K * tm), f, memory_space=pltpu.SMEM)
    return pl.pallas_call(
        _combine_body,
        grid=(nt,),
        in_specs=[ids(lambda i: (i, 0, 0)), ids(lambda i: (jnp.minimum(i + 1, nt - 1), 0, 0)),
                  pl.BlockSpec(memory_space=pl.ANY),
                  pl.BlockSpec((tm, D_MODEL), lambda i: (i, 0)),
                  pl.BlockSpec((tm, TOP_K), lambda i: (i, 0)), vec, vec],
        out_specs=pl.BlockSpec((tm, D_MODEL), lambda i: (i, 0)),
        out_shape=jax.ShapeDtypeStruct((n, D_MODEL), F32),
        scratch_shapes=[pltpu.VMEM((2, TOP_K, tm * TOKEN_ROWS, LANES), F32), pltpu.SemaphoreType.DMA((2,))],
        compiler_params=_params(("arbitrary",)),
        name="moe_combine_ln",
    )(pos_blocks, pos_blocks, y_pad, x, gate, g, b)


def _grouped_moe_ln(x, x_tiles, rw_t, rb_col, w1, w3, w2, layer, g, b):
    n = x.shape[0]
    eidx_t, gate_t, rank_t, counts = _router(x, rw_t, rb_col)
    padded = (counts[:, 0] + MOE_BLOCK - 1) // MOE_BLOCK * MOE_BLOCK
    ends = jnp.cumsum(padded)
    nb = -(-n * TOP_K // MOE_BLOCK) + N_EXPERTS
    block_start = jnp.arange(nb, dtype=I32) * MOE_BLOCK
    block_expert = jnp.minimum(jnp.sum(ends[None, :] <= block_start[:, None], axis=1), N_EXPERTS - 1).astype(I32)
    n_used = (ends[N_EXPERTS - 1:] // MOE_BLOCK).astype(I32)
    pos_t = _slot_rows(eidx_t, rank_t, (ends - padded).astype(I32).reshape(N_EXPERTS, 1))
    pos = pos_t.T
    tok_row = jnp.broadcast_to(jnp.arange(n, dtype=I32)[:, None] * TOKEN_ROWS, (n, TOP_K))
    tok_pad = jnp.zeros((nb * MOE_BLOCK,), I32).at[pos.reshape(-1)].set(tok_row.reshape(-1), unique_indices=True)
    y_pad = _experts(block_expert, n_used, tok_pad.reshape(nb, 1, MOE_BLOCK), x_tiles, w1, w3, w2, layer)
    pos_blocks = (pos * TOKEN_ROWS).reshape(n // COMBINE_TILE, 1, TOP_K * COMBINE_TILE)
    return _combine_ln(pos_blocks, y_pad, x, gate_t.T, g, b)


def kernel(x_prompt, x_sample, cache_fox_k, cache_fox_v, cache_fox_logf, state_rwkv, state_rwkv_shift, w_in, rwkv_mu, rwkv_w0, rwkv_w2, rwkv_a0, rwkv_a2, rwkv_g2, rwkv_k_k, rwkv_k_a, rwkv_r_k, rwkv_lnx_w, rwkv_lnx_b, fox_b_f, fox_q_g, fox_k_g, w_out, ln1_g, ln1_b, ln2_g, ln2_b, router_w, router_b, moe_w1, moe_w3, moe_w2):
    nb_p, seq, _ = x_prompt.shape
    nb_s, dec, _ = x_sample.shape
    depth = w_in.shape[0]
    n_p, n_s = nb_p * seq, nb_s * dec

    x = (x_prompt.reshape(n_p, D_MODEL), x_sample.reshape(n_s, D_MODEL))
    logf_rows = cache_fox_logf.transpose(0, 1, 3, 2)
    cache_k = cache_fox_k.transpose(0, 1, 3, 4, 2)
    cache_v = cache_fox_v.transpose(0, 1, 3, 4, 2)
    fox0 = RWKV_COLS
    fl0 = fox0 + 3 * D_F
    w_in_b = jnp.concatenate(
        [w_in[:, :, :fl0], w_in[:, :, fl0 + N_HEADS:], w_in[:, :, fl0:fl0 + N_HEADS],
         jnp.zeros((depth, D_MODEL, FL_PAD - N_HEADS), F32)], axis=-1).astype(BF16)
    w_out_b = w_out.astype(BF16)
    rw_t = router_w.T
    rb_col = router_b.reshape(N_EXPERTS, 1)
    head_of = jnp.arange(D_F, dtype=I32) // HEAD_DIM
    ones_bd = (head_of[:, None] == head_of[None, :]).astype(BF16)
    slot_consts = _slot_constants()
    zero_shift = jnp.zeros((nb_p, 1, RWKV_COLS), F32)
    zero_state = jnp.zeros((nb_p, N_HEADS, HEAD_DIM, HEAD_DIM), F32)
    row = lambda v: v.reshape(1, -1)

    outs = {k: [] for k in ('pk', 'pv', 'pl', 'pr', 'ps', 'sk', 'sv', 'sl', 'sr', 'ss')}
    for l in range(depth):
        lp = dict(mu=row(rwkv_mu[l]), w0=row(rwkv_w0[l]), w2=rwkv_w2[l], a0=row(rwkv_a0[l]), a2=rwkv_a2[l],
                  g2=rwkv_g2[l], k_k=row(rwkv_k_k[l]), k_a=row(rwkv_k_a[l]), r_k=row(rwkv_r_k[l]),
                  lnx_w=row(rwkv_lnx_w[l]), lnx_b=row(rwkv_lnx_b[l]), ones_bd=ones_bd)
        if l == 0:
            proj_p = _in_proj(x[0], w_in_b[l], 0, n_p)
            proj_s = _in_proj(x[1], w_in_b[l], 0, n_s)
        else:
            proj_p = _in_proj(x, w_in_b[l], 0, n_p)
            proj_s = _in_proj(x, w_in_b[l], n_p, n_s)
        pr_p, q_p, k_p, v_p, og_p, fl_p = proj_p
        pr_s, q_s, k_s, v_s, og_s, fl_s = proj_s

        ry_p, st_p, sh_p = _rwkv(pr_p, zero_shift, zero_state, lp, nb_p, seq, 0)
        ry_s, st_s, sh_s = _rwkv(pr_s, state_rwkv_shift[l], state_rwkv[l], lp, nb_s, dec, 0)

        b_f = jnp.concatenate([fox_b_f[l], jnp.zeros((FL_PAD - N_HEADS,), F32)]).reshape(1, FL_PAD)
        q_gain, k_gain = row(jnp.tile(fox_q_g[l], N_HEADS)), row(jnp.tile(fox_k_g[l], N_HEADS))
        lf_p, c_p = _logf_cumsum(fl_p, b_f, nb_p, seq, 0)
        lf_s, c_s = _logf_cumsum(fl_s, b_f, nb_s, dec, 0)
        qa_p, kn_p, ka_p, vb_p = _fox_prep(q_p, k_p, v_p, c_p, q_gain, k_gain, ones_bd, slot_consts)
        qa_s, kn_s, ka_s, vb_s = _fox_prep(q_s, k_s, v_s, c_s, q_gain, k_gain, ones_bd, slot_consts)
        fo_p = _attn_prompt(qa_p, ka_p, vb_p, nb_p, seq)
        fo_s = _attn_sample(qa_s, ka_s, vb_s, cache_k, cache_v, logf_rows, l, nb_s, dec, 0)

        x1, x1_tiles = _out_proj_ln((ry_p, ry_s), (fo_p, fo_s), (og_p, og_s), x, w_out_b[l],
                                    row(ln1_g[l]), row(ln1_b[l]))
        x = _grouped_moe_ln(x1, x1_tiles, rw_t, rb_col, moe_w1, moe_w3, moe_w2, l, row(ln2_g[l]), row(ln2_b[l]))

        outs['pk'].append(kn_p.reshape(nb_p, seq, N_HEADS, HEAD_DIM))
        outs['pv'].append(v_p.reshape(nb_p, seq, N_HEADS, HEAD_DIM))
        outs['pl'].append(lf_p[:, :N_HEADS].reshape(nb_p, seq, N_HEADS))
        outs['pr'].append(st_p)
        outs['ps'].append(sh_p)
        outs['sk'].append(kn_s.reshape(nb_s, dec, N_HEADS, HEAD_DIM))
        outs['sv'].append(v_s.reshape(nb_s, dec, N_HEADS, HEAD_DIM))
        outs['sl'].append(lf_s[:, :N_HEADS].reshape(nb_s, dec, N_HEADS))
        outs['sr'].append(st_s)
        outs['ss'].append(sh_s)

    stk = lambda key: jnp.stack(outs[key], axis=0)
    return (x[:n_p].reshape(nb_p, seq, D_MODEL), x[n_p:].reshape(nb_s, dec, D_MODEL),
            stk('pk'), stk('pv'), stk('pl'), stk('pr'), stk('ps'),
            stk('sk'), stk('sv'), stk('sl'), stk('sr'), stk('ss'))
```

```python
import functools

import jax
import jax.numpy as jnp
from jax import lax
from jax.experimental import pallas as pl
from jax.experimental.pallas import tpu as pltpu

F32 = jnp.float32
BF16 = jnp.bfloat16
I32 = jnp.int32

D_MODEL = 1024
HEAD_DIM = 64
N_HEADS = 8
D_R = N_HEADS * HEAD_DIM
D_F = N_HEADS * HEAD_DIM
DECAY_LORA = 64
A_LORA = 64
G_LORA = 128
RWKV_COLS = 3 * D_R + DECAY_LORA + A_LORA + G_LORA
FL_PAD = 128
IN_COLS_PAD = RWKV_COLS + 4 * D_F + FL_PAD
DEPTH = 2
N_EXPERTS = 32
N_GROUPS = 4
EXPERTS_PER_GROUP = N_EXPERTS // N_GROUPS
TOP_K = 2
D_EXPERT = D_MODEL // 2
MOE_BLOCK = 256
ALPHA = (2 * DEPTH) ** 0.25
LN_EPS = 1e-5
GN_EPS = 64e-5
QK_EPS = 1e-6
SCALE = HEAD_DIM ** -0.5
RWKV_CHUNK = 64
INV_BASE = 2
VMEM_LIMIT = 48 * 1024 * 1024

_NN = (((1,), (0,)), ((), ()))
_NT = (((1,), (1,)), ((), ()))
_TN = (((0,), (0,)), ((), ()))


def _dot(a, b, dims=_NN):
    return lax.dot_general(a, b, dims, preferred_element_type=F32)


def _split2(x):
    hi = x.astype(BF16)
    lo = (x - hi.astype(F32)).astype(BF16)
    return hi, lo


def _split3(x):
    hi = x.astype(BF16)
    r = x - hi.astype(F32)
    mid = r.astype(BF16)
    lo = (r - mid.astype(F32)).astype(BF16)
    return hi, mid, lo


def _dot3(a, b, dims=_NN):
    ah, al = _split2(a)
    bh, bl = _split2(b)
    return _dot(ah, bh, dims) + (_dot(ah, bl, dims) + _dot(al, bh, dims))


def _dot_exact_lhs(a_bf16, x, dims=_NN):
    hi, mid, lo = _split3(x)
    return _dot(a_bf16, hi, dims) + (_dot(a_bf16, mid, dims) + _dot(a_bf16, lo, dims))


def _sigmoid(x):
    return 1.0 / (1.0 + jnp.exp(-x))


def _softplus(x):
    return jnp.maximum(x, 0.0) + jnp.log(1.0 + jnp.exp(-jnp.abs(x)))


def _layer_norm(z, g, b):
    mu = jnp.mean(z, axis=-1, keepdims=True)
    zc = z - mu
    var = jnp.mean(zc * zc, axis=-1, keepdims=True)
    return zc * lax.rsqrt(var + LN_EPS) * g + b


def _params(sem):
    return pltpu.CompilerParams(dimension_semantics=sem, vmem_limit_bytes=VMEM_LIMIT)


_IN_SPLITS = (RWKV_COLS, D_F, D_F, D_F, D_F, FL_PAD)


def _in_proj_body(x_ref, w_ref, *out_refs):
    x = x_ref[...].astype(BF16)
    col = 0
    for ref, width in zip(out_refs, _IN_SPLITS):
        for c0 in range(0, width, 512):
            c1 = min(c0 + 512, width)
            ref[:, c0:c1] = _dot(x, w_ref[:, col + c0:col + c1])
        col += width


def _in_proj(x, w, row0, n):
    tm = 256
    blk0 = row0 // tm
    return pl.pallas_call(
        _in_proj_body,
        grid=(n // tm,),
        in_specs=[pl.BlockSpec((tm, D_MODEL), lambda i: (blk0 + i, 0)),
                  pl.BlockSpec((D_MODEL, IN_COLS_PAD), lambda i: (0, 0))],
        out_specs=[pl.BlockSpec((tm, wd), lambda i: (i, 0)) for wd in _IN_SPLITS],
        out_shape=[jax.ShapeDtypeStruct((n, wd), F32) for wd in _IN_SPLITS],
        compiler_params=_params(("parallel",)),
        name="in_proj",
    )(x, w)


GROUP = 4
GROUP_W = GROUP * HEAD_DIM
RWKV_SUB = 4


def _block_diag(x, keep):
    return jnp.where(keep, jnp.concatenate([x] * GROUP, axis=0), jnp.zeros((), x.dtype))


def _heads_mm(a, b, keep, dims=_NN, exact=True):
    if not exact:
        return _dot(a.astype(BF16), _block_diag(b.astype(BF16), keep), dims)
    ah, al = _split2(a)
    bh, bl = _split2(b)
    dh = _block_diag(bh, keep)
    dl = _block_diag(bl, keep)
    return _dot(ah, dh, dims) + (_dot(ah, dl, dims) + _dot(al, dh, dims))


def _fold_heads(f, keep):
    f = jnp.where(keep, f, 0.0)
    return (f[0:HEAD_DIM] + f[HEAD_DIM:2 * HEAD_DIM]) + (f[2 * HEAD_DIM:3 * HEAD_DIM] + f[3 * HEAD_DIM:4 * HEAD_DIM])


def _unit_lower_inverses(ls, c_len, t_row, t_col, keep_tt):
    shift = INV_BASE.bit_length() - 1
    same = (t_row >> shift) == (t_col >> shift)
    eye = jnp.where(t_row == t_col, 1.0, 0.0)
    p = [jnp.where(same, -l, 0.0) for l in ls]
    x = [eye + n for n in p]
    for _ in range(shift - 1):
        p = [_heads_mm(pi, pi, keep_tt) for pi in p]
        x = [xi + _heads_mm(xi, pi, keep_tt) for xi, pi in zip(x, p)]
    size = 2 * INV_BASE
    while size <= c_len:
        s_hi = size.bit_length() - 1
        off = ((t_row >> s_hi) == (t_col >> s_hi)) & ((t_row >> (s_hi - 1)) != (t_col >> (s_hi - 1)))
        xq = [_heads_mm(xi, jnp.where(off, l, 0.0), keep_tt, exact=False) for xi, l in zip(x, ls)]
        x = [xi - _heads_mm(xqi, xi, keep_tt, exact=False) for xi, xqi in zip(x, xq)]
        size *= 2
    return x


def _rwkv_body(c_len, n_sub, n_steps, pr_ref, sp_ref, s0_ref, mu_ref, w0_ref, w2_ref, a0_ref, a2_ref, g2_ref,
               kk_ref, ka_ref, rk_ref, lnw_ref, lnb_ref, ones_ref, out_ref, sout_ref, shift_ref, carry_ref, s_ref):
    ci = pl.program_id(1)
    rows = c_len * n_sub
    n_groups = N_HEADS // GROUP

    @pl.when(ci == 0)
    def _():
        carry_ref[...] = sp_ref[0]
        for gi in range(n_groups):
            s_ref[gi] = jnp.concatenate([s0_ref[0, GROUP * gi + j] for j in range(GROUP)], axis=-1)

    pr = pr_ref[...]
    trow = lax.broadcasted_iota(I32, (rows, 1), 0)
    prev = jnp.where(trow == 0, carry_ref[...], pltpu.roll(pr, 1, 0))
    carry_ref[...] = pr[rows - 1:rows, :]
    xs = pr + (prev - pr) * mu_ref[...]
    xr = xs[:, 0:D_R]
    xk = xs[:, D_R:2 * D_R]
    xv = xs[:, 2 * D_R:3 * D_R]
    o = 3 * D_R
    xw = xs[:, o:o + DECAY_LORA]
    xa = xs[:, o + DECAY_LORA:o + DECAY_LORA + A_LORA]
    xg = xs[:, o + DECAY_LORA + A_LORA:RWKV_COLS]

    z = w0_ref[...] + _dot3(jnp.tanh(xw), w2_ref[...])
    lw = -jnp.exp(-_softplus(-z) - 0.5)
    a = _sigmoid(a0_ref[...] + _dot3(xa, a2_ref[...]))
    g = _dot3(_sigmoid(xg), g2_ref[...])
    kk_raw = xk * kk_ref[...]
    k_mod = xk * (1.0 + (a - 1.0) * ka_ref[...])

    shift = c_len.bit_length() - 1
    r2 = lax.broadcasted_iota(I32, (rows, rows), 0)
    c2 = lax.broadcasted_iota(I32, (rows, rows), 1)
    within = ((r2 >> shift) == (c2 >> shift)) & (r2 >= c2)
    cl = _dot_exact_lhs(jnp.where(within, 1.0, 0.0).astype(BF16), lw)

    ones = ones_ref[...]

    def head_sum(x):
        hi, lo = _split2(x)
        return _dot(hi, ones) + _dot(lo, ones)

    kk = kk_raw / jnp.maximum(jnp.sqrt(head_sum(kk_raw * kk_raw)), 1e-12)
    eg = jnp.exp(cl)
    e_inv = jnp.exp(-cl)
    r_dec = xr * eg
    kk_dec = kk * jnp.exp(cl - lw)
    b_und = kk * a * e_inv
    k_und = k_mod * e_inv

    wt = GROUP * c_len
    t_row = lax.broadcasted_iota(I32, (c_len, wt), 0)
    t_col = lax.broadcasted_iota(I32, (c_len, wt), 1) & (c_len - 1)
    strict = t_row > t_col
    incl = t_row >= t_col
    hd_shift = HEAD_DIM.bit_length() - 1
    keep_tt = (lax.broadcasted_iota(I32, (wt, wt), 0) >> shift) == (lax.broadcasted_iota(I32, (wt, wt), 1) >> shift)
    keep_tf = (lax.broadcasted_iota(I32, (wt, GROUP_W), 0) >> shift) == (
        lax.broadcasted_iota(I32, (wt, GROUP_W), 1) >> hd_shift)
    keep_ff = (lax.broadcasted_iota(I32, (GROUP_W, GROUP_W), 0) >> hd_shift) == (
        lax.broadcasted_iota(I32, (GROUP_W, GROUP_W), 1) >> hd_shift)

    chains = [(c, gi) for c in range(n_sub) for gi in range(n_groups)]
    cut = lambda x, c, gi: x[c * c_len:(c + 1) * c_len, gi * GROUP_W:(gi + 1) * GROUP_W]
    lhs = [jnp.concatenate([cut(kk_dec, c, gi), cut(r_dec, c, gi)], axis=0) for c, gi in chains]
    bu = [cut(b_und, c, gi) for c, gi in chains]
    ku = [cut(k_und, c, gi) for c, gi in chains]
    vh = [cut(xv, c, gi) for c, gi in chains]
    n_ch = range(len(chains))
    gb = [_heads_mm(lhs[i], bu[i], keep_tf, _NT) for i in n_ch]
    gk = [_heads_mm(lhs[i], ku[i], keep_tf, _NT, exact=False) for i in n_ch]
    l_b = [jnp.where(strict, m[:c_len], 0.0) for m in gb]
    m_rb = [jnp.where(incl, m[c_len:], 0.0) for m in gb]
    l_k = [jnp.where(strict, m[:c_len], 0.0) for m in gk]
    m_rk = [jnp.where(incl, m[c_len:], 0.0) for m in gk]
    t_inv = _unit_lower_inverses(l_b, c_len, t_row, t_col, keep_tt)
    lkv = [_heads_mm(l_k[i], vh[i], keep_tf, exact=False) for i in n_ch]
    mkv = [_heads_mm(m_rk[i], vh[i], keep_tf, exact=False) for i in n_ch]
    fv = [_fold_heads(_dot(vh[i].astype(BF16), ku[i].astype(BF16), _TN), keep_ff) for i in n_ch]

    state = [s_ref[gi] for gi in range(n_groups)]
    y_rows = []
    for c in range(n_sub):
        ids = [c * n_groups + gi for gi in range(n_groups)]
        ps = [_heads_mm(lhs[i], state[gi], keep_ff, _NT, exact=False) for gi, i in enumerate(ids)]
        u = [-_heads_mm(t_inv[i], ps[gi][:c_len] + lkv[i], keep_tf, exact=False) for gi, i in enumerate(ids)]
        y = [ps[gi][c_len:] + mkv[i] + _heads_mm(m_rb[i], u[gi], keep_tf, exact=False) for gi, i in enumerate(ids)]
        fu = [_fold_heads(_dot(u[gi].astype(BF16), bu[i].astype(BF16), _TN), keep_ff) for gi, i in enumerate(ids)]
        last = c * c_len + c_len - 1
        state = [(state[gi] + fv[i] + fu[gi]) * eg[last:last + 1, gi * GROUP_W:(gi + 1) * GROUP_W]
                 for gi, i in enumerate(ids)]
        y_rows.append(jnp.concatenate(y, axis=-1))
    for gi in range(n_groups):
        s_ref[gi] = state[gi]

    y = jnp.concatenate(y_rows, axis=0)
    yc = y - head_sum(y) * (1.0 / HEAD_DIM)
    var = head_sum(yc * yc) * (1.0 / HEAD_DIM)
    yn = yc * lax.rsqrt(var + GN_EPS) * lnw_ref[...] + lnb_ref[...]
    bonus = head_sum(xr * k_mod * rk_ref[...]) * xv
    out_ref[...] = (yn + bonus) * g

    @pl.when(ci == n_steps - 1)
    def _():
        for gi in range(n_groups):
            for j in range(GROUP):
                sout_ref[0, GROUP * gi + j] = state[gi][:, j * HEAD_DIM:(j + 1) * HEAD_DIM]
        shift_ref[0] = pr[rows - 1:rows, :]


def _rwkv(pr, shift_prev, s0, lp, n_seq, seq_len, row0):
    c_len = min(RWKV_CHUNK, seq_len)
    n_sub = min(RWKV_SUB, seq_len // c_len)
    rows = c_len * n_sub
    n_steps = seq_len // rows
    blk0 = row0 // rows
    vec = lambda wd: pl.BlockSpec((1, wd), lambda b, c: (0, 0))
    mat = lambda r, wd: pl.BlockSpec((r, wd), lambda b, c: (0, 0))
    in_specs = [
        pl.BlockSpec((rows, RWKV_COLS), lambda b, c: (blk0 + b * n_steps + c, 0)),
        pl.BlockSpec((1, 1, RWKV_COLS), lambda b, c: (b, 0, 0)),
        pl.BlockSpec((1, N_HEADS, HEAD_DIM, HEAD_DIM), lambda b, c: (b, 0, 0, 0)),
        vec(RWKV_COLS), vec(D_R), mat(DECAY_LORA, D_R), vec(D_R), mat(A_LORA, D_R), mat(G_LORA, D_R),
        vec(D_R), vec(D_R), vec(D_R), vec(D_R), vec(D_R), mat(D_R, D_R),
    ]
    args = [pr, shift_prev, s0, lp['mu'], lp['w0'], lp['w2'], lp['a0'], lp['a2'], lp['g2'],
            lp['k_k'], lp['k_a'], lp['r_k'], lp['lnx_w'], lp['lnx_b'], lp['ones_bd']]
    return pl.pallas_call(
        functools.partial(_rwkv_body, c_len, n_sub, n_steps),
        grid=(n_seq, n_steps),
        in_specs=in_specs,
        out_specs=[pl.BlockSpec((rows, D_R), lambda b, c: (b * n_steps + c, 0)),
                   pl.BlockSpec((1, N_HEADS, HEAD_DIM, HEAD_DIM), lambda b, c: (b, 0, 0, 0)),
                   pl.BlockSpec((1, 1, RWKV_COLS), lambda b, c: (b, 0, 0))],
        out_shape=[jax.ShapeDtypeStruct((n_seq * seq_len, D_R), F32),
                   jax.ShapeDtypeStruct((n_seq, N_HEADS, HEAD_DIM, HEAD_DIM), F32),
                   jax.ShapeDtypeStruct((n_seq, 1, RWKV_COLS), F32)],
        scratch_shapes=[pltpu.VMEM((1, RWKV_COLS), F32), pltpu.VMEM((N_HEADS // GROUP, HEAD_DIM, GROUP_W), F32)],
        compiler_params=_params(("arbitrary", "arbitrary")),
        name="rwkv_mixer",
    )(*args)


SLOT = 2 * HEAD_DIM
C_LANE = HEAD_DIM


def _fox_prep_body(q_ref, k_ref, v_ref, c_ref, qg_ref, kg_ref, ones_ref, place_ref, pcq_ref, pck_ref, oneq_ref,
                   onek_ref, qa_ref, kn_ref, ka_ref, vb_ref):
    ones = ones_ref[...]

    def rms(x, gain):
        hi, lo = _split2(x * x)
        ss = _dot(hi, ones) + _dot(lo, ones)
        return x * lax.rsqrt(ss * (1.0 / HEAD_DIM) + QK_EPS) * gain

    qn = rms(q_ref[...], qg_ref[...]) * SCALE
    kn = rms(k_ref[...], kg_ref[...])
    kn_ref[...] = kn
    vb_ref[...] = v_ref[...].astype(BF16)
    c_parts = _split3(c_ref[...])

    def slots(xb, pc_ref, one_ref):
        acc = _dot(xb, place_ref[...]) + one_ref[...]
        for j in range(3):
            acc = acc + _dot(c_parts[j], pc_ref[j])
        return acc.astype(BF16)

    qa_ref[...] = slots(qn.astype(BF16), pcq_ref, oneq_ref)
    ka_ref[...] = slots(kn.astype(BF16), pck_ref, onek_ref)


def _slot_constants():
    d = jnp.arange(D_F, dtype=I32)
    lane = jnp.arange(N_HEADS * SLOT, dtype=I32)
    place = (lane[None, :] == (d // HEAD_DIM * SLOT + d % HEAD_DIM)[:, None]).astype(BF16)
    h = jnp.arange(FL_PAD, dtype=I32)[None, :, None]
    j = jnp.arange(3, dtype=I32)[:, None, None]
    is_head = h < N_HEADS
    pcq = ((lane[None, None, :] == h * SLOT + C_LANE + j) & is_head).astype(BF16)
    pck = -((lane[None, None, :] == h * SLOT + C_LANE + 3 + j) & is_head).astype(BF16)
    in_slot = lane % SLOT
    oneq = ((in_slot >= C_LANE + 3) & (in_slot < C_LANE + 6)).astype(F32).reshape(1, -1)
    onek = ((in_slot >= C_LANE) & (in_slot < C_LANE + 3)).astype(F32).reshape(1, -1)
    return place, pcq, pck, oneq, onek


def _fox_prep(q, k, v, c, q_gain, k_gain, ones_bd, slot_consts):
    n = q.shape[0]
    tm = 256
    wide = N_HEADS * SLOT
    row = lambda wd: pl.BlockSpec((tm, wd), lambda i: (i, 0))
    vec = lambda wd: pl.BlockSpec((1, wd), lambda i: (0, 0))
    full = lambda *shape: pl.BlockSpec(shape, lambda i: (0,) * len(shape))
    return pl.pallas_call(
        _fox_prep_body,
        grid=(n // tm,),
        in_specs=[row(D_F), row(D_F), row(D_F), row(FL_PAD), vec(D_F), vec(D_F), full(D_F, D_F),
                  full(D_F, wide), full(3, FL_PAD, wide), full(3, FL_PAD, wide), vec(wide), vec(wide)],
        out_specs=[row(wide), row(D_F), row(wide), row(D_F)],
        out_shape=[jax.ShapeDtypeStruct((n, wide), BF16), jax.ShapeDtypeStruct((n, D_F), F32),
                   jax.ShapeDtypeStruct((n, wide), BF16), jax.ShapeDtypeStruct((n, D_F), BF16)],
        compiler_params=_params(("parallel",)),
        name="fox_prep",
    )(q, k, v, c, q_gain, k_gain, ones_bd, *slot_consts)


def _logf_cumsum_body(fl_ref, bf_ref, lf_ref, c_ref, carry_ref):
    @pl.when(pl.program_id(1) == 0)
    def _():
        carry_ref[...] = jnp.zeros_like(carry_ref)

    lf = -_softplus(-(fl_ref[...] + bf_ref[...]))
    lf_ref[...] = lf
    t = lf.shape[0]
    row = lax.broadcasted_iota(I32, (t, t), 0)
    col = lax.broadcasted_iota(I32, (t, t), 1)
    cs = _dot_exact_lhs(jnp.where(row >= col, 1.0, 0.0).astype(BF16), lf) + carry_ref[...]
    c_ref[...] = cs
    carry_ref[...] = cs[t - 1:t, :]


def _logf_cumsum(fl, b_f, n_seq, seq_len, row0):
    tc = min(seq_len, 256)
    nt = seq_len // tc
    blk0 = row0 // tc
    out = pl.BlockSpec((tc, FL_PAD), lambda b, j: (b * nt + j, 0))
    return pl.pallas_call(
        _logf_cumsum_body,
        grid=(n_seq, nt),
        in_specs=[pl.BlockSpec((tc, FL_PAD), lambda b, j: (blk0 + b * nt + j, 0)),
                  pl.BlockSpec((1, FL_PAD), lambda b, j: (0, 0))],
        out_specs=[out, out],
        out_shape=[jax.ShapeDtypeStruct((n_seq * seq_len, FL_PAD), F32)] * 2,
        scratch_shapes=[pltpu.VMEM((1, FL_PAD), F32)],
        compiler_params=_params(("arbitrary", "arbitrary")),
        name="logf_cumsum",
    )(fl, b_f)


ATT_TILE = 512


def _attn_prompt_body(q_ref, k_ref, v_ref, o_ref):
    i = pl.program_id(2)
    t = ATT_TILE
    row = lax.broadcasted_iota(I32, (t, t), 0)
    col = lax.broadcasted_iota(I32, (t, t), 1)
    causal = row >= col
    pair = range(2)
    q = [q_ref[:, hh * SLOT:(hh + 1) * SLOT] for hh in pair]

    def tile(j, carry, masked):
        m, l, acc = carry
        j0 = pl.multiple_of(j * t, t)
        s = [_dot(q[hh], k_ref[pl.ds(j0, t), hh * SLOT:(hh + 1) * SLOT], _NT) for hh in pair]
        if masked:
            s = [jnp.where(causal, sh, -jnp.inf) for sh in s]
        m_new = [jnp.maximum(m[hh], jnp.max(s[hh], axis=-1, keepdims=True)) for hh in pair]
        alpha = [jnp.exp(m[hh] - m_new[hh]) for hh in pair]
        p = [jnp.exp(s[hh] - m_new[hh]) for hh in pair]
        l = [alpha[hh] * l[hh] + jnp.sum(p[hh], axis=-1, keepdims=True) for hh in pair]
        pv = [_dot(p[hh].astype(BF16), v_ref[pl.ds(j0, t), hh * HEAD_DIM:(hh + 1) * HEAD_DIM]) for hh in pair]
        acc = [alpha[hh] * acc[hh] + pv[hh] for hh in pair]
        return m_new, l, acc

    init = ([jnp.full((t, 1), -jnp.inf, F32)] * 2, [jnp.zeros((t, 1), F32)] * 2,
            [jnp.zeros((t, HEAD_DIM), F32)] * 2)
    carry = lax.fori_loop(0, i, lambda j, c: tile(j, c, False), init)
    _, l, acc = tile(i, carry, True)
    o_ref[...] = jnp.concatenate([acc[hh] / l[hh] for hh in pair], axis=-1)


def _attn_prompt(qa, ka, vb, n_seq, seq_len):
    t = ATT_TILE
    nq = seq_len // t
    return pl.pallas_call(
        _attn_prompt_body,
        grid=(n_seq, N_HEADS // 2, nq),
        in_specs=[pl.BlockSpec((t, 2 * SLOT), lambda b, p, i: (b * nq + i, p)),
                  pl.BlockSpec((seq_len, 2 * SLOT), lambda b, p, i: (b, p)),
                  pl.BlockSpec((seq_len, 2 * HEAD_DIM), lambda b, p, i: (b, p))],
        out_specs=pl.BlockSpec((t, 2 * HEAD_DIM), lambda b, p, i: (b * nq + i, p)),
        out_shape=jax.ShapeDtypeStruct((n_seq * seq_len, D_F), F32),
        compiler_params=_params(("parallel", "parallel", "arbitrary")),
        name="fox_attn_prompt",
    )(qa, ka, vb)


SAMPLE_CHUNK = 2048
TAIL_BLOCK = 1024


def _attn_sample_body(n_chunks, q_ref, kn_ref, vn_ref, kc_ref, vc_ref, lp_ref, after_ref, o_ref,
                      m_ref, l_ref, acc_ref, suffix_ref):
    j = pl.program_id(1)
    n = q_ref.shape[0]
    heads = range(N_HEADS)

    @pl.when(j == 0)
    def _():
        m_ref[...] = jnp.full(m_ref.shape, -jnp.inf, F32)
        l_ref[...] = jnp.zeros_like(l_ref)
        acc_ref[...] = jnp.zeros_like(acc_ref)
        suffix_ref[...] = jnp.zeros_like(suffix_ref)

    after = after_ref[...]
    tp = lp_ref.shape[3]
    tails = []
    suffix = suffix_ref[...]
    for b0 in range(tp - TAIL_BLOCK, -1, -TAIL_BLOCK):
        lp = lp_ref[0, 0, :, b0:b0 + TAIL_BLOCK]
        hi, mid, lo = _split3(lp)
        tails.insert(0, _dot(hi, after) + (_dot(mid, after) + _dot(lo, after)) + suffix)
        suffix = suffix + jnp.sum(lp, axis=-1, keepdims=True)
    suffix_ref[...] = suffix
    tail = jnp.concatenate(tails, axis=-1)

    q_slot = [q_ref[:, h * SLOT:(h + 1) * SLOT] for h in heads]
    c_col = [(qs[:, C_LANE:C_LANE + 1].astype(F32) + qs[:, C_LANE + 1:C_LANE + 2].astype(F32)
              + qs[:, C_LANE + 2:C_LANE + 3].astype(F32)) for qs in q_slot]
    s = [_dot(q_slot[h][:, :HEAD_DIM], kc_ref[0, 0, h].astype(BF16)) + c_col[h] + tail[h:h + 1, :] for h in heads]
    m_old = [m_ref[h] for h in heads]
    m_new = [jnp.maximum(m_old[h], jnp.max(s[h], axis=-1, keepdims=True)) for h in heads]
    alpha = [jnp.exp(m_old[h] - m_new[h]) for h in heads]
    p = [jnp.exp(s[h] - m_new[h]) for h in heads]
    pv = [_dot(p[h].astype(BF16), vc_ref[0, 0, h].astype(BF16), _NT) for h in heads]
    for h in heads:
        m_ref[h] = m_new[h]
        l_ref[h] = alpha[h] * l_ref[h] + jnp.sum(p[h], axis=-1, keepdims=True)
        acc_ref[h] = alpha[h] * acc_ref[h] + pv[h]

    @pl.when(j == n_chunks - 1)
    def _():
        row = lax.broadcasted_iota(I32, (n, n), 0)
        col = lax.broadcasted_iota(I32, (n, n), 1)
        s_new = [jnp.where(row >= col, _dot(q_slot[h], kn_ref[:, h * SLOT:(h + 1) * SLOT], _NT), -jnp.inf)
                 for h in heads]
        m_fin = [jnp.maximum(m_ref[h], jnp.max(s_new[h], axis=-1, keepdims=True)) for h in heads]
        a_fin = [jnp.exp(m_ref[h] - m_fin[h]) for h in heads]
        p_new = [jnp.exp(s_new[h] - m_fin[h]) for h in heads]
        l_fin = [a_fin[h] * l_ref[h] + jnp.sum(p_new[h], axis=-1, keepdims=True) for h in heads]
        acc = [a_fin[h] * acc_ref[h] + _dot(p_new[h].astype(BF16), vn_ref[:, h * HEAD_DIM:(h + 1) * HEAD_DIM])
               for h in heads]
        o_ref[...] = jnp.concatenate([acc[h] / l_fin[h] for h in heads], axis=-1)


def _attn_sample(qa, ka, vb, cache_k, cache_v, logf_rows, layer, n_seq, n_new, row0):
    past = cache_k.shape[4]
    tp = min(SAMPLE_CHUNK, past)
    n_chunks = past // tp
    blk0 = row0 // n_new
    frame = jnp.arange(TAIL_BLOCK, dtype=I32)
    after = (frame[:, None] > frame[None, :]).astype(BF16)
    rows = lambda wd: pl.BlockSpec((n_new, wd), lambda b, j: (blk0 + b, 0))
    cache = lambda: pl.BlockSpec((1, 1, N_HEADS, HEAD_DIM, tp), lambda b, j: (layer, b, 0, 0, n_chunks - 1 - j))
    return pl.pallas_call(
        functools.partial(_attn_sample_body, n_chunks),
        grid=(n_seq, n_chunks),
        in_specs=[rows(N_HEADS * SLOT), rows(N_HEADS * SLOT), rows(D_F), cache(), cache(),
                  pl.BlockSpec((1, 1, N_HEADS, tp), lambda b, j: (layer, b, 0, n_chunks - 1 - j)),
                  pl.BlockSpec((TAIL_BLOCK, TAIL_BLOCK), lambda b, j: (0, 0))],
        out_specs=pl.BlockSpec((n_new, D_F), lambda b, j: (b, 0)),
        out_shape=jax.ShapeDtypeStruct((n_seq * n_new, D_F), F32),
        scratch_shapes=[pltpu.VMEM((N_HEADS, n_new, 1), F32), pltpu.VMEM((N_HEADS, n_new, 1), F32),
                        pltpu.VMEM((N_HEADS, n_new, HEAD_DIM), F32), pltpu.VMEM((N_HEADS, 1), F32)],
        compiler_params=_params(("parallel", "arbitrary")),
        name="fox_attn_sample",
    )(qa, ka, vb, cache_k, cache_v, logf_rows, after)


LANES = 128
TOKEN_ROWS = D_MODEL // LANES


def _store_token_tiles(ref, x):
    m = x.shape[0]
    for c in range(TOKEN_ROWS):
        ref[pl.ds(c, m, stride=TOKEN_ROWS), :] = x[:, c * LANES:(c + 1) * LANES]


def _load_token_tiles(ref, m):
    return jnp.concatenate([ref[pl.ds(c, m, stride=TOKEN_ROWS), :] for c in range(TOKEN_ROWS)], axis=-1)


def _out_proj_body(tiles_p, n_x, *refs):
    ryp_ref, rys_ref, fop_ref, fos_ref, ogp_ref, ogs_ref = refs[:6]
    x_refs = refs[6:6 + n_x]
    w_ref, g_ref, b_ref, o_ref, ot_ref = refs[6 + n_x:]
    from_prompt = pl.program_id(0) < tiles_p
    pick = lambda p_ref, s_ref: jnp.where(from_prompt, p_ref[...], s_ref[...])
    ry = pick(ryp_ref, rys_ref).astype(BF16)
    fy = (pick(fop_ref, fos_ref) * _sigmoid(pick(ogp_ref, ogs_ref))).astype(BF16)
    x = pick(*x_refs) if n_x == 2 else x_refs[0][...]
    m = _dot(ry, w_ref[0:D_R, :]) + _dot(fy, w_ref[D_R:D_R + D_F, :])
    out = _layer_norm(ALPHA * x + m, g_ref[...], b_ref[...])
    o_ref[...] = out
    _store_token_tiles(ot_ref, out)


def _out_proj_ln(ry, fo, og, x, w, g, b):
    tm = 256
    tiles_p = ry[0].shape[0] // tm
    n = ry[0].shape[0] + ry[1].shape[0]
    row = lambda wd: pl.BlockSpec((tm, wd), lambda i: (i, 0))
    row_p = lambda wd: pl.BlockSpec((tm, wd), lambda i: (jnp.minimum(i, tiles_p - 1), 0))
    row_s = lambda wd: pl.BlockSpec((tm, wd), lambda i: (jnp.maximum(i - tiles_p, 0), 0))
    pair = lambda wd: [row_p(wd), row_s(wd)]
    vec = pl.BlockSpec((1, D_MODEL), lambda i: (0, 0))
    x_parts = tuple(x) if isinstance(x, (tuple, list)) else (x,)
    x_specs = pair(D_MODEL) if len(x_parts) == 2 else [row(D_MODEL)]
    return pl.pallas_call(
        functools.partial(_out_proj_body, tiles_p, len(x_parts)),
        grid=(n // tm,),
        in_specs=pair(D_R) + pair(D_F) + pair(D_F) + x_specs + [
            pl.BlockSpec((D_R + D_F, D_MODEL), lambda i: (0, 0)), vec, vec],
        out_specs=[row(D_MODEL), pl.BlockSpec((tm * TOKEN_ROWS, LANES), lambda i: (i, 0))],
        out_shape=[jax.ShapeDtypeStruct((n, D_MODEL), F32), jax.ShapeDtypeStruct((n * TOKEN_ROWS, LANES), F32)],
        compiler_params=_params(("parallel",)),
        name="out_proj_ln",
    )(*ry, *fo, *og, *x_parts, w, g, b)


def _router_body(x_ref, rw_ref, rb_ref, earlier_ref, e_ref, g_ref, r_ref, count_ref, seen_ref):
    tn = x_ref.shape[0]
    scores = _sigmoid(_dot3(rw_ref[...], x_ref[...], _NT))
    sel = scores + rb_ref[...]
    sel4 = sel.reshape(N_GROUPS, EXPERTS_PER_GROUP, tn)
    sc4 = scores.reshape(N_GROUPS, EXPERTS_PER_GROUP, tn)
    lane_e = lax.broadcasted_iota(I32, (N_GROUPS, EXPERTS_PER_GROUP, tn), 1)

    def top2(vals, idx_iota, axis):
        m1 = jnp.max(vals, axis=axis, keepdims=True)
        i1 = jnp.min(jnp.where(vals == m1, idx_iota, EXPERTS_PER_GROUP), axis=axis, keepdims=True)
        rest = jnp.where(idx_iota == i1, -jnp.inf, vals)
        m2 = jnp.max(rest, axis=axis, keepdims=True)
        i2 = jnp.min(jnp.where(rest == m2, idx_iota, EXPERTS_PER_GROUP), axis=axis, keepdims=True)
        return m1, i1, m2, i2

    m1, _, m2, _ = top2(sel4, lane_e, 1)
    gsum = m1 + m2
    g_iota = lax.broadcasted_iota(I32, (N_GROUPS, 1, tn), 0)
    gmax = jnp.max(gsum, axis=0, keepdims=True)
    g_idx = jnp.min(jnp.where(gsum == gmax, g_iota, N_GROUPS), axis=0, keepdims=True)
    pick = g_iota == g_idx
    sel_g = jnp.max(jnp.where(pick, sel4, -jnp.inf), axis=0)
    sc_g = jnp.max(jnp.where(pick, sc4, -jnp.inf), axis=0)
    e_iota = lax.broadcasted_iota(I32, (EXPERTS_PER_GROUP, tn), 0)
    _, i1, _, i2 = top2(sel_g, e_iota, 0)
    gate1 = jnp.sum(jnp.where(e_iota == i1, sc_g, 0.0), axis=0, keepdims=True)
    gate2 = jnp.sum(jnp.where(e_iota == i2, sc_g, 0.0), axis=0, keepdims=True)
    tot = gate1 + gate2
    base = g_idx[0] * EXPERTS_PER_GROUP
    e1 = base + i1
    e2 = base + i2
    e_ref[...] = jnp.concatenate([e1, e2], axis=0)
    g_ref[...] = jnp.concatenate([gate1 / tot, gate2 / tot], axis=0)

    @pl.when(pl.program_id(0) == 0)
    def _():
        seen_ref[...] = jnp.zeros_like(seen_ref)

    all_e = lax.broadcasted_iota(I32, (N_EXPERTS, tn), 0)
    hit1 = all_e == e1
    hit2 = all_e == e2
    hits = jnp.where(hit1 | hit2, 1.0, 0.0)
    before = _dot(hits.astype(BF16), earlier_ref[...]) + seen_ref[...]
    r1 = jnp.sum(jnp.where(hit1, before, 0.0), axis=0, keepdims=True)
    r2 = jnp.sum(jnp.where(hit2, before, 0.0), axis=0, keepdims=True)
    r_ref[...] = jnp.concatenate([r1, r2], axis=0).astype(I32)
    seen_ref[...] = seen_ref[...] + jnp.sum(hits, axis=-1, keepdims=True)
    count_ref[...] = seen_ref[...].astype(I32)


def _router(x, rw_t, rb_col):
    n = x.shape[0]
    tn = 512
    tok = jnp.arange(tn, dtype=I32)
    earlier = (tok[:, None] < tok[None, :]).astype(BF16)
    pair = lambda: pl.BlockSpec((TOP_K, tn), lambda i: (0, i))
    return pl.pallas_call(
        _router_body,
        grid=(n // tn,),
        in_specs=[pl.BlockSpec((tn, D_MODEL), lambda i: (i, 0)),
                  pl.BlockSpec((N_EXPERTS, D_MODEL), lambda i: (0, 0)),
                  pl.BlockSpec((N_EXPERTS, 1), lambda i: (0, 0)),
                  pl.BlockSpec((tn, tn), lambda i: (0, 0))],
        out_specs=[pair(), pair(), pair(), pl.BlockSpec((N_EXPERTS, 1), lambda i: (0, 0))],
        out_shape=[jax.ShapeDtypeStruct((TOP_K, n), I32), jax.ShapeDtypeStruct((TOP_K, n), F32),
                   jax.ShapeDtypeStruct((TOP_K, n), I32), jax.ShapeDtypeStruct((N_EXPERTS, 1), I32)],
        scratch_shapes=[pltpu.VMEM((N_EXPERTS, 1), F32)],
        compiler_params=_params(("arbitrary",)),
        name="router",
    )(x, rw_t, rb_col, earlier)


def _slot_rows_body(e_ref, r_ref, start_ref, pos_ref):
    tn = e_ref.shape[1]
    all_e = lax.broadcasted_iota(I32, (N_EXPERTS, tn), 0)
    rows = [jnp.sum(jnp.where(all_e == e_ref[k:k + 1, :], start_ref[...], 0), axis=0, keepdims=True)
            for k in range(TOP_K)]
    pos_ref[...] = r_ref[...] + jnp.concatenate(rows, axis=0)


def _slot_rows(eidx_t, rank_t, expert_start):
    n = eidx_t.shape[1]
    tn = 512
    pair = lambda: pl.BlockSpec((TOP_K, tn), lambda i: (0, i))
    return pl.pallas_call(
        _slot_rows_body,
        grid=(n // tn,),
        in_specs=[pair(), pair(), pl.BlockSpec((N_EXPERTS, 1), lambda i: (0, 0))],
        out_specs=pair(),
        out_shape=jax.ShapeDtypeStruct((TOP_K, n), I32),
        compiler_params=_params(("parallel",)),
        name="moe_slot_rows",
    )(eidx_t, rank_t, expert_start)


DMA_THREADS = 2


def _row_copy(src_hbm, src_row8, dst, dst_token, sem):
    src = src_hbm.at[pl.ds(pl.multiple_of(src_row8, TOKEN_ROWS), TOKEN_ROWS)]
    return pltpu.make_async_copy(src, dst.at[pl.ds(pl.multiple_of(dst_token * TOKEN_ROWS, TOKEN_ROWS), TOKEN_ROWS)], sem)


def _expert_body(be_ref, used_ref, tok_ref, tok_next_ref, x_hbm, w1_ref, w3_ref, w2_ref, y_ref,
                 buf_ref, w1b_ref, w3b_ref, w2b_ref, sem):
    i = pl.program_id(0)
    n_used = used_ref[0]
    slot = lax.rem(i, 2)

    def start_gather(ids_ref, s):
        def body(r2, carry):
            for j in range(DMA_THREADS):
                r = DMA_THREADS * r2 + j
                _row_copy(x_hbm, ids_ref[0, 0, r], buf_ref.at[s], r, sem.at[s]).start(priority=j)
            return carry
        lax.fori_loop(0, MOE_BLOCK // DMA_THREADS, body, 0, unroll=4)

    @pl.when(jnp.logical_and(i == 0, n_used > 0))
    def _():
        start_gather(tok_ref, 0)

    @pl.when(i + 1 < n_used)
    def _():
        start_gather(tok_next_ref, 1 - slot)

    @pl.when(i < n_used)
    def _():
        @pl.when(jnp.logical_or(i == 0, be_ref[i] != be_ref[jnp.maximum(i - 1, 0)]))
        def _():
            w1b_ref[...] = w1_ref[0, 0].astype(BF16)
            w3b_ref[...] = w3_ref[0, 0].astype(BF16)
            w2b_ref[...] = w2_ref[0, 0].astype(BF16)

        def wait(r, carry):
            _row_copy(x_hbm, 0, buf_ref.at[slot], r, sem.at[slot]).wait()
            return carry
        lax.fori_loop(0, MOE_BLOCK, wait, 0, unroll=8)

        h = _load_token_tiles(buf_ref.at[slot], MOE_BLOCK).astype(BF16)
        a = _dot(h, w1b_ref[...])
        b = _dot(h, w3b_ref[...])
        act = (a * _sigmoid(a) * b).astype(BF16)
        _store_token_tiles(y_ref, _dot(act, w2b_ref[...]))

    @pl.when(i >= n_used)
    def _():
        y_ref[...] = jnp.zeros_like(y_ref)


def _experts(block_expert, n_used, tok_blocks, x, w1, w3, w2, layer):
    nb = block_expert.shape[0]
    ids = lambda f: pl.BlockSpec((1, 1, MOE_BLOCK), f, memory_space=pltpu.SMEM)
    grid_spec = pltpu.PrefetchScalarGridSpec(
        num_scalar_prefetch=2,
        grid=(nb,),
        in_specs=[ids(lambda i, be, nu: (i, 0, 0)),
                  ids(lambda i, be, nu: (jnp.minimum(i + 1, nb - 1), 0, 0)),
                  pl.BlockSpec(memory_space=pl.ANY),
                  pl.BlockSpec((1, 1, D_MODEL, D_EXPERT), lambda i, be, nu: (layer, be[i], 0, 0)),
                  pl.BlockSpec((1, 1, D_MODEL, D_EXPERT), lambda i, be, nu: (layer, be[i], 0, 0)),
                  pl.BlockSpec((1, 1, D_EXPERT, D_MODEL), lambda i, be, nu: (layer, be[i], 0, 0))],
        out_specs=pl.BlockSpec((MOE_BLOCK * TOKEN_ROWS, LANES), lambda i, be, nu: (i, 0)),
        scratch_shapes=[pltpu.VMEM((2, MOE_BLOCK * TOKEN_ROWS, LANES), F32),
                        pltpu.VMEM((D_MODEL, D_EXPERT), BF16), pltpu.VMEM((D_MODEL, D_EXPERT), BF16),
                        pltpu.VMEM((D_EXPERT, D_MODEL), BF16), pltpu.SemaphoreType.DMA((2,))],
    )
    return pl.pallas_call(
        _expert_body,
        grid_spec=grid_spec,
        out_shape=jax.ShapeDtypeStruct((nb * MOE_BLOCK * TOKEN_ROWS, LANES), F32),
        compiler_params=_params(("arbitrary",)),
        name="moe_experts",
    )(block_expert, n_used, tok_blocks, tok_blocks, x, w1, w3, w2)


COMBINE_TILE = 128


def _combine_body(pos_ref, pos_next_ref, y_hbm, x_ref, gate_ref, g_ref, b_ref, o_ref, buf_ref, sem):
    i = pl.program_id(0)
    slot = lax.rem(i, 2)

    def start_gather(ids_ref, s):
        def body(t, carry):
            for k in range(TOP_K):
                _row_copy(y_hbm, ids_ref[0, 0, TOP_K * t + k], buf_ref.at[s, k], t, sem.at[s]).start(
                    priority=k % DMA_THREADS)
            return carry
        lax.fori_loop(0, COMBINE_TILE, body, 0, unroll=4)

    @pl.when(i == 0)
    def _():
        start_gather(pos_ref, 0)

    @pl.when(i + 1 < pl.num_programs(0))
    def _():
        start_gather(pos_next_ref, 1 - slot)

    def wait(t, carry):
        for k in range(TOP_K):
            _row_copy(y_hbm, 0, buf_ref.at[slot, k], t, sem.at[slot]).wait()
        return carry
    lax.fori_loop(0, COMBINE_TILE, wait, 0, unroll=4)

    gate = gate_ref[...]
    y = (gate[:, 0:1] * _load_token_tiles(buf_ref.at[slot, 0], COMBINE_TILE)
         + gate[:, 1:2] * _load_token_tiles(buf_ref.at[slot, 1], COMBINE_TILE))
    o_ref[...] = _layer_norm(ALPHA * x_ref[...] + y, g_ref[...], b_ref[...])


def _combine_ln(pos_blocks, y_pad, x, gate, g, b):
    n = x.shape[0]
    tm = COMBINE_TILE
    nt = n // tm
    vec = pl.BlockSpec((1, D_MODEL), lambda i: (0, 0))
    ids = lambda f: pl.BlockSpec((1, 1, TOP_K * tm), f, memory_space=pltpu.SMEM)
    return pl.pallas_call(
        _combine_body,
        grid=(nt,),
        in_specs=[ids(lambda i: (i, 0, 0)), ids(lambda i: (jnp.minimum(i + 1, nt - 1), 0, 0)),
                  pl.BlockSpec(memory_space=pl.ANY),
                  pl.BlockSpec((tm, D_MODEL), lambda i: (i, 0)),
                  pl.BlockSpec((tm, TOP_K), lambda i: (i, 0)), vec, vec],
        out_specs=pl.BlockSpec((tm, D_MODEL), lambda i: (i, 0)),
        out_shape=jax.ShapeDtypeStruct((n, D_MODEL), F32),
        scratch_shapes=[pltpu.VMEM((2, TOP_K, tm * TOKEN_ROWS, LANES), F32), pltpu.SemaphoreType.DMA((2,))],
        compiler_params=_params(("arbitrary",)),
        name="moe_combine_ln",
    )(pos_blocks, pos_blocks, y_pad, x, gate, g, b)


def _grouped_moe_ln(x, x_tiles, rw_t, rb_col, w1, w3, w2, layer, g, b):
    n = x.shape[0]
    eidx_t, gate_t, rank_t, counts = _router(x, rw_t, rb_col)
    padded = (counts[:, 0] + MOE_BLOCK - 1) // MOE_BLOCK * MOE_BLOCK
    ends = jnp.cumsum(padded)
    nb = -(-n * TOP_K // MOE_BLOCK) + N_EXPERTS
    block_start = jnp.arange(nb, dtype=I32) * MOE_BLOCK
    block_expert = jnp.minimum(jnp.sum(ends[None, :] <= block_start[:, None], axis=1), N_EXPERTS - 1).astype(I32)
    n_used = (ends[N_EXPERTS - 1:] // MOE_BLOCK).astype(I32)
    pos_t = _slot_rows(eidx_t, rank_t, (ends - padded).astype(I32).reshape(N_EXPERTS, 1))
    pos = pos_t.T
    tok_row = jnp.broadcast_to(jnp.arange(n, dtype=I32)[:, None] * TOKEN_ROWS, (n, TOP_K))
    tok_pad = jnp.zeros((nb * MOE_BLOCK,), I32).at[pos.reshape(-1)].set(tok_row.reshape(-1), unique_indices=True)
    y_pad = _experts(block_expert, n_used, tok_pad.reshape(nb, 1, MOE_BLOCK), x_tiles, w1, w3, w2, layer)
    pos_blocks = (pos * TOKEN_ROWS).reshape(n // COMBINE_TILE, 1, TOP_K * COMBINE_TILE)
    return _combine_ln(pos_blocks, y_pad, x, gate_t.T, g, b)


def kernel(x_prompt, x_sample, cache_fox_k, cache_fox_v, cache_fox_logf, state_rwkv, state_rwkv_shift, w_in, rwkv_mu, rwkv_w0, rwkv_w2, rwkv_a0, rwkv_a2, rwkv_g2, rwkv_k_k, rwkv_k_a, rwkv_r_k, rwkv_lnx_w, rwkv_lnx_b, fox_b_f, fox_q_g, fox_k_g, w_out, ln1_g, ln1_b, ln2_g, ln2_b, router_w, router_b, moe_w1, moe_w3, moe_w2):
    nb_p, seq, _ = x_prompt.shape
    nb_s, dec, _ = x_sample.shape
    depth = w_in.shape[0]
    n_p, n_s = nb_p * seq, nb_s * dec

    x = (x_prompt.reshape(n_p, D_MODEL), x_sample.reshape(n_s, D_MODEL))
    logf_rows = cache_fox_logf.transpose(0, 1, 3, 2)
    cache_k = cache_fox_k.transpose(0, 1, 3, 4, 2)
    cache_v = cache_fox_v.transpose(0, 1, 3, 4, 2)
    fox0 = RWKV_COLS
    fl0 = fox0 + 3 * D_F
    w_in_b = jnp.concatenate(
        [w_in[:, :, :fl0], w_in[:, :, fl0 + N_HEADS:], w_in[:, :, fl0:fl0 + N_HEADS],
         jnp.zeros((depth, D_MODEL, FL_PAD - N_HEADS), F32)], axis=-1).astype(BF16)
    w_out_b = w_out.astype(BF16)
    rw_t = router_w.T
    rb_col = router_b.reshape(N_EXPERTS, 1)
    head_of = jnp.arange(D_F, dtype=I32) // HEAD_DIM
    ones_bd = (head_of[:, None] == head_of[None, :]).astype(BF16)
    slot_consts = _slot_constants()
    zero_shift = jnp.zeros((nb_p, 1, RWKV_COLS), F32)
    zero_state = jnp.zeros((nb_p, N_HEADS, HEAD_DIM, HEAD_DIM), F32)
    row = lambda v: v.reshape(1, -1)

    outs = {k: [] for k in ('pk', 'pv', 'pl', 'pr', 'ps', 'sk', 'sv', 'sl', 'sr', 'ss')}
    for l in range(depth):
        lp = dict(mu=row(rwkv_mu[l]), w0=row(rwkv_w0[l]), w2=rwkv_w2[l], a0=row(rwkv_a0[l]), a2=rwkv_a2[l],
                  g2=rwkv_g2[l], k_k=row(rwkv_k_k[l]), k_a=row(rwkv_k_a[l]), r_k=row(rwkv_r_k[l]),
                  lnx_w=row(rwkv_lnx_w[l]), lnx_b=row(rwkv_lnx_b[l]), ones_bd=ones_bd)
        if l == 0:
            proj_p = _in_proj(x[0], w_in_b[l], 0, n_p)
            proj_s = _in_proj(x[1], w_in_b[l], 0, n_s)
        else:
            proj_p = _in_proj(x, w_in_b[l], 0, n_p)
            proj_s = _in_proj(x, w_in_b[l], n_p, n_s)
        pr_p, q_p, k_p, v_p, og_p, fl_p = proj_p
        pr_s, q_s, k_s, v_s, og_s, fl_s = proj_s

        ry_p, st_p, sh_p = _rwkv(pr_p, zero_shift, zero_state, lp, nb_p, seq, 0)
        ry_s, st_s, sh_s = _rwkv(pr_s, state_rwkv_shift[l], state_rwkv[l], lp, nb_s, dec, 0)

        b_f = jnp.concatenate([fox_b_f[l], jnp.zeros((FL_PAD - N_HEADS,), F32)]).reshape(1, FL_PAD)
        q_gain, k_gain = row(jnp.tile(fox_q_g[l], N_HEADS)), row(jnp.tile(fox_k_g[l], N_HEADS))
        lf_p, c_p = _logf_cumsum(fl_p, b_f, nb_p, seq, 0)
        lf_s, c_s = _logf_cumsum(fl_s, b_f, nb_s, dec, 0)
        qa_p, kn_p, ka_p, vb_p = _fox_prep(q_p, k_p, v_p, c_p, q_gain, k_gain, ones_bd, slot_consts)
        qa_s, kn_s, ka_s, vb_s = _fox_prep(q_s, k_s, v_s, c_s, q_gain, k_gain, ones_bd, slot_consts)
        fo_p = _attn_prompt(qa_p, ka_p, vb_p, nb_p, seq)
        fo_s = _attn_sample(qa_s, ka_s, vb_s, cache_k, cache_v, logf_rows, l, nb_s, dec, 0)

        x1, x1_tiles = _out_proj_ln((ry_p, ry_s), (fo_p, fo_s), (og_p, og_s), x, w_out_b[l],
                                    row(ln1_g[l]), row(ln1_b[l]))
        x = _grouped_moe_ln(x1, x1_tiles, rw_t, rb_col, moe_w1, moe_w3, moe_w2, l, row(ln2_g[l]), row(ln2_b[l]))

        outs['pk'].append(kn_p.reshape(nb_p, seq, N_HEADS, HEAD_DIM))
        outs['pv'].append(v_p.reshape(nb_p, seq, N_HEADS, HEAD_DIM))
        outs['pl'].append(lf_p[:, :N_HEADS].reshape(nb_p, seq, N_HEADS))
        outs['pr'].append(st_p)
        outs['ps'].append(sh_p)
        outs['sk'].append(kn_s.reshape(nb_s, dec, N_HEADS, HEAD_DIM))
        outs['sv'].append(v_s.reshape(nb_s, dec, N_HEADS, HEAD_DIM))
        outs['sl'].append(lf_s[:, :N_HEADS].reshape(nb_s, dec, N_HEADS))
        outs['sr'].append(st_s)
        outs['ss'].append(sh_s)

    stk = lambda key: jnp.stack(outs[key], axis=0)
    return (x[:n_p].reshape(nb_p, seq, D_MODEL), x[n_p:].reshape(nb_s, dec, D_MODEL),
            stk('pk'), stk('pv'), stk('pl'), stk('pr'), stk('ps'),
            stk('sk'), stk('sv'), stk('sl'), stk('sr'), stk('ss'))
```
